```python
import math
import jax, jax.numpy as jnp
from jax import lax
import numpy as np

D_MODEL = 1024
BATCH = 4
SEQ = 4096
DEPTH = 2
DEC_BATCH = 128
DEC_SEQ = 8
PAST_LEN = 2048
PAGE_SIZE = 128

N_MIXERS = 2
N_A_LAYERS = (DEPTH + 1) // 2
N_B_LAYERS = DEPTH // 2

A_HEADS = 8
A_KDIM = 128
A_VDIM = 256
A_QK_WIDTH = A_HEADS * A_KDIM
A_V_WIDTH = A_HEADS * A_VDIM
A_CHUNK = 32

B_GROUPS = ((128, 1), (512, 4), (2048, 16))
B_HEADS_PER_GROUP = 4
B_HEAD_DIM = 128
B_N_HEADS = len(B_GROUPS) * B_HEADS_PER_GROUP
B_WIDTH = B_N_HEADS * B_HEAD_DIM
B_QBLOCK = 128
ROPE_THETA = 10000.0
EPS = 1e-6

kernel_name = 'hgrn2_dilated_swa_adaln_decode_step'


def rmsnorm(x, w):
    xf = x.astype(jnp.float32)
    y = xf * lax.rsqrt(jnp.mean(xf * xf, axis=-1, keepdims=True) + EPS)
    return (y * w.astype(jnp.float32)).astype(x.dtype)


def adaln(x, c, norm_w, w_ada, b_ada):
    ada = jax.nn.silu(c) @ w_ada + b_ada
    shift, scale, gate = jnp.split(ada, 3, axis=-1)
    h = rmsnorm(x, norm_w) * (1 + scale[:, None]) + shift[:, None]
    return h, gate[:, None]


def rope(x, pos):
    half = x.shape[-1] // 2
    inv = ROPE_THETA ** (-jnp.arange(half, dtype=jnp.float32) / half)
    ang = pos.astype(jnp.float32)[:, None] * inv[None]
    cos = jnp.cos(ang)[None, :, None, :]
    sin = jnp.sin(ang)[None, :, None, :]
    xf = x.astype(jnp.float32)
    x1, x2 = xf[..., :half], xf[..., half:]
    return jnp.concatenate([x1 * cos - x2 * sin, x2 * cos + x1 * sin], axis=-1).astype(x.dtype)


def hgrn2_scan(q, k, v, logf, s0):
    B, L, H, dk = q.shape
    dv = v.shape[-1]
    C = math.gcd(L, A_CHUNK)
    n = L // C

    def to_chunks(a):
        return a.reshape(B, n, C, H, a.shape[-1]).transpose(1, 0, 3, 2, 4)

    causal = jnp.tril(jnp.ones((C, C), dtype=bool))

    def step(S, inp):
        qc, kc, vc, gc = inp
        b = jnp.cumsum(gc, axis=2)
        rel = jnp.where(causal[None, None, :, :, None],
                        b[:, :, :, None, :] - b[:, :, None, :, :], -jnp.inf)
        A = jnp.einsum('bhtc,bhsc,bhtsc->bhts', qc, kc, jnp.exp(rel))
        o = (jnp.einsum('bhts,bhsv->bhtv', A, vc)
             + jnp.einsum('bhtc,bhcv->bhtv', qc * jnp.exp(b), S))
        b_last = b[:, :, -1:, :]
        S_new = (jnp.exp(b_last[:, :, 0, :])[..., None] * S
                 + jnp.einsum('bhsc,bhsv->bhcv', kc * jnp.exp(b_last - b), vc))
        return S_new, o

    S, o = lax.scan(step, s0, (to_chunks(q), to_chunks(k), to_chunks(v), to_chunks(logf)))
    o = o.transpose(1, 0, 3, 2, 4).reshape(B, L, H, dv)
    return o, S


def hgrn2_mixer(h, s0, lb, w_in, b_f, g_norm_w, w_out):
    B, L, _ = h.shape
    proj = h @ w_in
    q, f, i, z = jnp.split(proj, [A_QK_WIDTH, 2 * A_QK_WIDTH, 2 * A_QK_WIDTH + A_V_WIDTH], axis=-1)
    q = jax.nn.silu(q.astype(jnp.float32)).reshape(B, L, A_HEADS, A_KDIM)
    f = lb + (1.0 - lb) * jax.nn.sigmoid((f + b_f).astype(jnp.float32))
    logf = jnp.log(f).reshape(B, L, A_HEADS, A_KDIM)
    k = (1.0 - f).reshape(B, L, A_HEADS, A_KDIM)
    v = i.astype(jnp.float32).reshape(B, L, A_HEADS, A_VDIM)
    o, S = hgrn2_scan(q, k, v, logf, s0.astype(jnp.float32))
    o = rmsnorm(o, g_norm_w).reshape(B, L, A_V_WIDTH).astype(h.dtype)
    y = (o * jax.nn.silu(z)) @ w_out
    return y, S.astype(s0.dtype)


def dilated_window_attn(q, k, v, q_start, k_start, dilation, n_keys):
    B, Lq, Hg, hd = q.shape
    QB = math.gcd(Lq, B_QBLOCK)
    nb = Lq // QB
    offs = jnp.arange(n_keys, dtype=jnp.int32) * dilation
    scale = hd ** -0.5

    def block(args):
        qb, bi = args
        qpos = q_start + bi * QB + jnp.arange(QB, dtype=jnp.int32)
        kidx = qpos[:, None] - offs[None, :] - k_start
        valid = kidx >= 0
        kidx = jnp.maximum(kidx, 0)
        kg = k[:, kidx].astype(jnp.float32)
        vg = v[:, kidx].astype(jnp.float32)
        s = jnp.einsum('bqhd,bqnhd->bqhn', qb.astype(jnp.float32), kg) * scale
        s = jnp.where(valid[None, :, None, :], s, -jnp.inf)
        lse = jax.nn.logsumexp(s, axis=-1)
        p = jnp.exp(s - lse[..., None])
        o = jnp.einsum('bqhn,bqnhd->bqhd', p, vg)
        return o, lse

    qblocks = q.reshape(B, nb, QB, Hg, hd).transpose(1, 0, 2, 3, 4)
    o, lse = lax.map(block, (qblocks, jnp.arange(nb, dtype=jnp.int32)))
    o = o.transpose(1, 0, 2, 3, 4).reshape(B, Lq, Hg, hd)
    lse = lse.transpose(1, 0, 2, 3).reshape(B, Lq, Hg)
    return o, lse


def dilated_mixer(h, pos, start, bufs, w_in, q_norm_w, k_norm_w, w_out):
    B, L, _ = h.shape
    proj = h @ w_in
    q, k, v, z = jnp.split(proj, 4, axis=-1)
    q = rope(rmsnorm(q.reshape(B, L, B_N_HEADS, B_HEAD_DIM), q_norm_w), pos)
    k = rope(rmsnorm(k.reshape(B, L, B_N_HEADS, B_HEAD_DIM), k_norm_w), pos)
    v = v.reshape(B, L, B_N_HEADS, B_HEAD_DIM)
    outs, lses, rows = [], [], []
    for g, (window, dil) in enumerate(B_GROUPS):
        sl = slice(g * B_HEADS_PER_GROUP, (g + 1) * B_HEADS_PER_GROUP)
        qg, kg, vg = q[:, :, sl], k[:, :, sl], v[:, :, sl]
        rows.append(jnp.stack([kg, vg], axis=2))
        if bufs is None:
            k_all, v_all, k_start = kg, vg, start
        else:
            buf = bufs[g]
            k_all = jnp.concatenate([buf[:, :, 0].astype(kg.dtype), kg], axis=1)
            v_all = jnp.concatenate([buf[:, :, 1].astype(vg.dtype), vg], axis=1)
            k_start = start - buf.shape[1]
        o, lse = dilated_window_attn(qg, k_all, v_all, start, k_start, dil, window // dil + 1)
        outs.append(o)
        lses.append(lse)
    lam = jax.nn.softmax(jnp.stack(lses, axis=0), axis=0)
    o = jnp.concatenate([outs[g] * lam[g][..., None] for g in range(len(B_GROUPS))], axis=2)
    o = o.reshape(B, L, B_WIDTH).astype(h.dtype)
    y = (o * jax.nn.silu(z)) @ w_out
    return y, rows


def trunk(x, c, start, state_hgrn, kv_bufs, norm_w, ada_w, ada_b, a_lower_bounds,
          a_w_in, a_b_f, a_g_norm_w, a_w_out, b_w_in, b_q_norm_w, b_k_norm_w, b_w_out):
    B, L, _ = x.shape
    pos = start + jnp.arange(L, dtype=jnp.int32)
    lb_all = jnp.cumsum(jax.nn.softmax(a_lower_bounds.astype(jnp.float32), axis=0), axis=0)
    new_hgrn = []
    new_kv = [[] for _ in B_GROUPS]
    for i in range(DEPTH):
        h, gate = adaln(x, c, norm_w[i], ada_w[i], ada_b[i])
        j = i // N_MIXERS
        if i % N_MIXERS == 0:
            if state_hgrn is None:
                s0 = jnp.zeros((B, A_HEADS, A_KDIM, A_VDIM), x.dtype)
            else:
                s0 = state_hgrn[j]
            y, s_new = hgrn2_mixer(h, s0, lb_all[i], a_w_in[j], a_b_f[j], a_g_norm_w[j], a_w_out[j])
            new_hgrn.append(s_new)
        else:
            bufs = None if kv_bufs is None else tuple(buf[j] for buf in kv_bufs)
            y, rows = dilated_mixer(h, pos, start, bufs, b_w_in[j], b_q_norm_w[j], b_k_norm_w[j], b_w_out[j])
            for g, (window, _) in enumerate(B_GROUPS):
                if kv_bufs is None:
                    new_kv[g].append(rows[g][:, L - min(window, L):])
                else:
                    new_kv[g].append(rows[g])
        x = x + gate * y
    kv_out = [jnp.stack(r, axis=0) for r in new_kv]
    return x, jnp.stack(new_hgrn, axis=0), kv_out[0], kv_out[1], kv_out[2]


def setup_inputs(seed: int = 0) -> dict:
    key = jax.random.key(seed)
    ks = jax.random.split(key, 24)
    nrm = jax.random.normal
    f32 = jnp.float32
    Hg, hd = B_HEADS_PER_GROUP, B_HEAD_DIM
    return {
        'x_prompt': nrm(ks[0], (BATCH, SEQ, D_MODEL), f32),
        'x_sample': nrm(ks[1], (DEC_BATCH, DEC_SEQ, D_MODEL), f32),
        'state_hgrn': 0.5 * nrm(ks[2], (N_A_LAYERS, DEC_BATCH, A_HEADS, A_KDIM, A_VDIM), f32),
        'cache_kv_w128': nrm(ks[3], (N_B_LAYERS, DEC_BATCH, min(128, PAST_LEN), 2, Hg, hd), f32),
        'cache_kv_w512': nrm(ks[4], (N_B_LAYERS, DEC_BATCH, min(512, PAST_LEN), 2, Hg, hd), f32),
        'cache_kv_w2048': nrm(ks[5], (N_B_LAYERS, DEC_BATCH, min(2048, PAST_LEN), 2, Hg, hd), f32),
        'c_prompt': nrm(ks[6], (BATCH, D_MODEL), f32),
        'c_sample': nrm(ks[7], (DEC_BATCH, D_MODEL), f32),
        'norm_w': 1.0 + 0.01 * nrm(ks[8], (DEPTH, D_MODEL), f32),
        'ada_w': 0.5 * D_MODEL ** -0.5 * nrm(ks[9], (DEPTH, D_MODEL, 3 * D_MODEL), f32),
        'ada_b': 0.01 * nrm(ks[10], (DEPTH, 3 * D_MODEL), f32),
        'a_lower_bounds': 0.1 * nrm(ks[11], (DEPTH, A_QK_WIDTH), f32),
        'a_w_in': D_MODEL ** -0.5 * nrm(ks[12], (N_A_LAYERS, D_MODEL, 2 * A_QK_WIDTH + 2 * A_V_WIDTH), f32),
        'a_b_f': 0.1 * nrm(ks[13], (N_A_LAYERS, A_QK_WIDTH), f32),
        'a_g_norm_w': 1.0 + 0.01 * nrm(ks[14], (N_A_LAYERS, A_VDIM), f32),
        'a_w_out': A_V_WIDTH ** -0.5 * nrm(ks[15], (N_A_LAYERS, A_V_WIDTH, D_MODEL), f32),
        'b_w_in': D_MODEL ** -0.5 * nrm(ks[16], (N_B_LAYERS, D_MODEL, 4 * B_WIDTH), f32),
        'b_q_norm_w': 1.0 + 0.01 * nrm(ks[17], (N_B_LAYERS, B_HEAD_DIM), f32),
        'b_k_norm_w': 1.0 + 0.01 * nrm(ks[18], (N_B_LAYERS, B_HEAD_DIM), f32),
        'b_w_out': B_WIDTH ** -0.5 * nrm(ks[19], (N_B_LAYERS, B_WIDTH, D_MODEL), f32),
    }


def reference(x_prompt, x_sample, state_hgrn, cache_kv_w128, cache_kv_w512, cache_kv_w2048,
              c_prompt, c_sample, norm_w, ada_w, ada_b, a_lower_bounds, a_w_in, a_b_f,
              a_g_norm_w, a_w_out, b_w_in, b_q_norm_w, b_k_norm_w, b_w_out):
    y_prompt, hgrn_prompt, kv128_prompt, kv512_prompt, kv2048_prompt = trunk(
        x_prompt, c_prompt, 0, None, None, norm_w, ada_w, ada_b, a_lower_bounds,
        a_w_in, a_b_f, a_g_norm_w, a_w_out, b_w_in, b_q_norm_w, b_k_norm_w, b_w_out)
    y_sample, hgrn_sample, kv128_sample, kv512_sample, kv2048_sample = trunk(
        x_sample, c_sample, PAST_LEN, state_hgrn, (cache_kv_w128, cache_kv_w512, cache_kv_w2048),
        norm_w, ada_w, ada_b, a_lower_bounds, a_w_in, a_b_f, a_g_norm_w, a_w_out,
        b_w_in, b_q_norm_w, b_k_norm_w, b_w_out)
    return (y_prompt, y_sample, hgrn_prompt, hgrn_sample, kv128_prompt, kv128_sample,
            kv512_prompt, kv512_sample, kv2048_prompt, kv2048_sample)
```

```python
import functools
import math

import numpy as np
import jax
import jax.numpy as jnp
from jax import lax
from jax.experimental import pallas as pl
from jax.experimental.pallas import tpu as pltpu

F32 = jnp.float32
BF16 = jnp.bfloat16

D_MODEL = 1024
EPS = 1e-6
A_HEADS = 8
A_KDIM = 128
A_VDIM = 256
A_QK = A_HEADS * A_KDIM
A_V = A_HEADS * A_VDIM
B_GROUPS = ((128, 1), (512, 4), (2048, 16))
B_HG = 4
B_HD = 128
B_GW = B_HG * B_HD
B_WIDTH = len(B_GROUPS) * B_GW
ROPE_THETA = 10000.0
PAST_LEN = 2048
N_KEYS = 129

COL_TILE = 512
VMEM_LIMIT = 56 * 1024 * 1024

_NT = (((1,), (1,)), ((), ()))
_TN = (((0,), (0,)), ((), ()))


def _sigmoid(x):
    return 1.0 / (1.0 + jnp.exp(-x))


def _silu(x):
    return x * _sigmoid(x)


def _dot(a, b, dims=None):
    if dims is None:
        return jnp.dot(a, b, preferred_element_type=F32)
    return lax.dot_general(a, b, dims, preferred_element_type=F32)


def _params(sem):
    return pltpu.CompilerParams(dimension_semantics=sem, vmem_limit_bytes=VMEM_LIMIT)


def _ada_kernel(c_ref, w_ref, b_ref, o_ref):
    a = _silu(c_ref[...]).astype(BF16)
    o_ref[0] = _dot(a, w_ref[0].astype(BF16)) + b_ref[0]


def _ada_vectors(c_all, ada_w, ada_b):
    n_layers, _, width = ada_w.shape
    rows = c_all.shape[0]
    return pl.pallas_call(
        _ada_kernel,
        grid=(n_layers, width // COL_TILE),
        in_specs=[
            pl.BlockSpec((rows, D_MODEL), lambda l, j: (0, 0)),
            pl.BlockSpec((1, D_MODEL, COL_TILE), lambda l, j: (l, 0, j)),
            pl.BlockSpec((1, 1, COL_TILE), lambda l, j: (l, 0, j)),
        ],
        out_specs=pl.BlockSpec((1, rows, COL_TILE), lambda l, j: (l, 0, j)),
        out_shape=jax.ShapeDtypeStruct((n_layers, rows, width), F32),
        compiler_params=_params(("parallel", "parallel")),
        name="ada_vectors",
    )(c_all, ada_w, ada_b.reshape(n_layers, 1, width))


def _modulated_norm(x_ref, ada_ref, nw_ref):
    x = x_ref[...]
    ada = ada_ref[...]
    if x.ndim == 2:
        ada = ada.reshape(1, ada.shape[-1])
    shift = ada[..., :D_MODEL]
    scale = ada[..., D_MODEL:2 * D_MODEL]
    ms = jnp.mean(x * x, axis=-1, keepdims=True)
    y = x * lax.rsqrt(ms + EPS) * nw_ref[...].reshape((1,) * (x.ndim - 1) + (D_MODEL,))
    h = y * (1.0 + scale) + shift
    return h.reshape(-1, D_MODEL).astype(BF16)


def _token_specs(batch, seq, tm):
    if seq >= tm:
        assert seq % tm == 0
        per = seq // tm
        n_tiles = batch * per
        x_block = lambda w: pl.BlockSpec((tm, w), lambda i, j: (i, 0))
        ada_block = pl.BlockSpec((1, 1, 3 * D_MODEL), lambda i, j: (i // per, 0, 0))
        col_block = lambda w, f: pl.BlockSpec((tm, w), lambda i, j: (i, f(j)))
        gate_block = lambda w, f: pl.BlockSpec((1, 1, w), lambda i, j: (i // per, 0, f(j)))
        shape = lambda w: (batch * seq, w)
        return n_tiles, x_block, ada_block, col_block, gate_block, shape, per
    tb = tm // seq
    assert batch % tb == 0
    n_tiles = batch // tb
    x_block = lambda w: pl.BlockSpec((tb, seq, w), lambda i, j: (i, 0, 0))
    ada_block = pl.BlockSpec((tb, 1, 3 * D_MODEL), lambda i, j: (i, 0, 0))
    col_block = lambda w, f: pl.BlockSpec((tb, seq, w), lambda i, j: (i, 0, f(j)))
    gate_block = lambda w, f: pl.BlockSpec((tb, 1, w), lambda i, j: (i, 0, f(j)))
    shape = lambda w: (batch, seq, w)
    return n_tiles, x_block, ada_block, col_block, gate_block, shape, None


def _clip(lo, n):
    return lambda j: jnp.clip(j - lo, 0, n - 1)


_A_NQ = A_QK // COL_TILE
_A_NF = A_QK // COL_TILE
_A_NV = A_V // COL_TILE
_A_NZ = A_V // COL_TILE


def _l0_inproj_kernel(x_ref, ada_ref, nw_ref, w_ref, alb_ref, bf_ref,
                      q_ref, k_ref, lf_ref, v_ref, zs_ref, h_scr):
    j = pl.program_id(1)

    @pl.when(j == 0)
    def _():
        h_scr[...] = _modulated_norm(x_ref, ada_ref, nw_ref)

    acc = _dot(h_scr[...], w_ref[...])

    @pl.when(j < _A_NQ)
    def _():
        q_ref[...] = _silu(acc).reshape(q_ref.shape)

    @pl.when((j >= _A_NQ) & (j < _A_NQ + _A_NF))
    def _():
        a = alb_ref[...]
        m = jnp.max(a, axis=0, keepdims=True)
        e = jnp.exp(a - m)
        lb = e[0:1] / jnp.sum(e, axis=0, keepdims=True)
        f = lb + (1.0 - lb) * _sigmoid(acc + bf_ref[...])
        lf_ref[...] = jnp.log(f).reshape(lf_ref.shape)
        k_ref[...] = (1.0 - f).reshape(k_ref.shape)

    @pl.when((j >= _A_NQ + _A_NF) & (j < _A_NQ + _A_NF + _A_NV))
    def _():
        v_ref[...] = acc.reshape(v_ref.shape)

    @pl.when(j >= _A_NQ + _A_NF + _A_NV)
    def _():
        zs_ref[...] = _silu(acc).reshape(zs_ref.shape)


def _l0_inproj(x, ada, nw, w_bf, alb, b_f, batch, seq, tm):
    n_tiles, x_block, ada_block, col_block, _, shape, _ = _token_specs(batch, seq, tm)
    n_col = w_bf.shape[1] // COL_TILE
    f0, v0, z0 = _A_NQ, _A_NQ + _A_NF, _A_NQ + _A_NF + _A_NV
    return pl.pallas_call(
        _l0_inproj_kernel,
        grid=(n_tiles, n_col),
        in_specs=[
            x_block(D_MODEL),
            ada_block,
            pl.BlockSpec((1, D_MODEL), lambda i, j: (0, 0)),
            pl.BlockSpec((D_MODEL, COL_TILE), lambda i, j: (0, j)),
            pl.BlockSpec((alb.shape[0], COL_TILE), lambda i, j: (0, _clip(f0, _A_NF)(j))),
            pl.BlockSpec((1, COL_TILE), lambda i, j: (0, _clip(f0, _A_NF)(j))),
        ],
        out_specs=[
            col_block(COL_TILE, _clip(0, _A_NQ)),
            col_block(COL_TILE, _clip(f0, _A_NF)),
            col_block(COL_TILE, _clip(f0, _A_NF)),
            col_block(COL_TILE, _clip(v0, _A_NV)),
            col_block(COL_TILE, _clip(z0, _A_NZ)),
        ],
        out_shape=[
            jax.ShapeDtypeStruct(shape(A_QK), F32),
            jax.ShapeDtypeStruct(shape(A_QK), F32),
            jax.ShapeDtypeStruct(shape(A_QK), F32),
            jax.ShapeDtypeStruct(shape(A_V), F32),
            jax.ShapeDtypeStruct(shape(A_V), F32),
        ],
        scratch_shapes=[pltpu.VMEM((tm, D_MODEL), BF16)],
        compiler_params=_params(("parallel", "arbitrary")),
        name="l0_inproj",
    )(x, ada, nw, w_bf, alb, b_f)


def _scan_levels(chunk):
    return [1 << l for l in range(int(math.log2(chunk)))]


def _exponent_matrix(chunk):
    r = np.arange(chunk)[:, None]
    c = np.arange(chunk)[None, :]
    blocks = [c <= r, c > r]
    for s in _scan_levels(chunk):
        bound = (r & ~(2 * s - 1)) | s
        upper = (r & s) != 0
        blocks.append(np.where(upper, (c > bound) & (c <= r), (c > r) & (c <= bound)))
    return np.concatenate(blocks, axis=0).astype(np.float32)


def _hgrn_chunk(q, k, lf, v, state, emat, chunk):
    dk = q.shape[-1]
    ex = jnp.exp(jnp.dot(emat, lf, preferred_element_type=F32, precision=lax.Precision.HIGHEST))
    e_b = ex[0:chunk]
    e_u = ex[chunk:2 * chunk]
    row = lax.broadcasted_iota(jnp.int32, (chunk, 1), 0)
    col = lax.broadcasted_iota(jnp.int32, (1, chunk), 1)

    att = jnp.where(row == col, jnp.sum(q * k, axis=-1, keepdims=True), 0.0)
    for l, s in enumerate(_scan_levels(chunk)):
        g = ex[(2 + l) * chunk:(3 + l) * chunk]
        upper = (row & s) != 0
        xb = (jnp.where(upper, q, k) * g).astype(BF16)
        part = _dot(xb, xb, _NT)
        keep = upper & ((col & s) == 0) & ((row >> (l + 1)) == (col >> (l + 1)))
        att = att + jnp.where(keep, part, 0.0)

    o = _dot(att.astype(BF16), v.astype(BF16)) + _dot((q * e_b).astype(BF16), state.astype(BF16))

    e_last = e_b[chunk - 1:chunk]
    eye = lax.broadcasted_iota(jnp.int32, (dk, dk), 0) == lax.broadcasted_iota(jnp.int32, (dk, dk), 1)
    e_col = jnp.sum(jnp.where(eye, e_last, 0.0), axis=-1, keepdims=True)
    new_state = state * e_col + _dot((k * e_u).astype(BF16), v.astype(BF16), _TN)
    return o, new_state


def _hgrn_scan_kernel(*refs, chunk, n_chunks, has_state):
    if has_state:
        q_ref, k_ref, lf_ref, v_ref, zs_ref, gw_ref, e_ref, s0_ref, og_ref, so_ref, s_scr = refs
    else:
        q_ref, k_ref, lf_ref, v_ref, zs_ref, gw_ref, e_ref, og_ref, so_ref, s_scr = refs
    t = pl.program_id(1)

    @pl.when(t == 0)
    def _():
        if has_state:
            s_scr[...] = s0_ref[0]
        else:
            s_scr[...] = jnp.zeros_like(s_scr)

    emat = e_ref[...]
    gw = gw_ref[...]

    def body(c, carry):
        rows = slice(0, chunk) if n_chunks == 1 else pl.ds(pl.multiple_of(c * chunk, chunk), chunk)
        for h in range(A_HEADS):
            kc = slice(h * A_KDIM, (h + 1) * A_KDIM)
            vc = slice(h * A_VDIM, (h + 1) * A_VDIM)
            o, s_new = _hgrn_chunk(q_ref[rows, kc], k_ref[rows, kc], lf_ref[rows, kc],
                                   v_ref[rows, vc], s_scr[h], emat, chunk)
            s_scr[h] = s_new
            ms = jnp.mean(o * o, axis=-1, keepdims=True)
            og_ref[rows, vc] = o * lax.rsqrt(ms + EPS) * gw * zs_ref[rows, vc]
        return carry

    if n_chunks == 1:
        body(0, 0)
    else:
        lax.fori_loop(0, n_chunks, body, 0)

    @pl.when(t == pl.num_programs(1) - 1)
    def _():
        so_ref[0] = s_scr[...]


def _hgrn_scan(q, k, lf, v, zs, gw, s0, batch, seq, tile, chunk):
    n_t = seq // tile
    emat = jnp.asarray(_exponent_matrix(chunk))
    row_block = lambda w: pl.BlockSpec((tile, w), lambda b, t: (b * n_t + t, 0))
    state_block = pl.BlockSpec((1, A_HEADS, A_KDIM, A_VDIM), lambda b, t: (b, 0, 0, 0))
    in_specs = [row_block(A_QK), row_block(A_QK), row_block(A_QK), row_block(A_V), row_block(A_V),
                pl.BlockSpec((1, A_VDIM), lambda b, t: (0, 0)),
                pl.BlockSpec(emat.shape, lambda b, t: (0, 0))]
    args = [q, k, lf, v, zs, gw, emat]
    if s0 is not None:
        in_specs.append(state_block)
        args.append(s0)
    kern = functools.partial(_hgrn_scan_kernel, chunk=chunk, n_chunks=tile // chunk,
                             has_state=s0 is not None)
    return pl.pallas_call(
        kern,
        grid=(batch, n_t),
        in_specs=in_specs,
        out_specs=[row_block(A_V), state_block],
        out_shape=[jax.ShapeDtypeStruct((batch * seq, A_V), F32),
                   jax.ShapeDtypeStruct((batch, A_HEADS, A_KDIM, A_VDIM), F32)],
        scratch_shapes=[pltpu.VMEM((A_HEADS, A_KDIM, A_VDIM), F32)],
        compiler_params=_params(("parallel", "arbitrary")),
        name="hgrn_scan",
    )(*args)


def _outproj_kernel(g_ref, w_ref, x_ref, gate_ref, o_ref):
    g = g_ref[...]
    y = _dot(g.reshape(-1, g.shape[-1]).astype(BF16), w_ref[...])
    x = x_ref[...]
    gate = gate_ref[...]
    if x.ndim == 2:
        gate = gate.reshape(1, gate.shape[-1])
    o_ref[...] = x + gate * y.reshape(x.shape)


def _outproj(g, w_bf, x, ada, batch, seq, tm):
    n_tiles, x_block, _, col_block, gate_block, shape, _ = _token_specs(batch, seq, tm)
    kdim = w_bf.shape[0]
    n_col = D_MODEL // COL_TILE
    gate0 = 2 * D_MODEL // COL_TILE
    return pl.pallas_call(
        _outproj_kernel,
        grid=(n_tiles, n_col),
        in_specs=[
            x_block(kdim),
            pl.BlockSpec((kdim, COL_TILE), lambda i, j: (0, j)),
            col_block(COL_TILE, lambda j: j),
            gate_block(COL_TILE, lambda j: gate0 + j),
        ],
        out_specs=col_block(COL_TILE, lambda j: j),
        out_shape=jax.ShapeDtypeStruct(shape(D_MODEL), F32),
        compiler_params=_params(("parallel", "arbitrary")),
        name="outproj",
    )(g, w_bf, x, ada)


def _rope_table_kernel(pos_ref, inv_ref, cos_ref, sin_ref):
    ang = pos_ref[...] * inv_ref[...]
    lane = lax.broadcasted_iota(jnp.int32, ang.shape, 1)
    cos_ref[...] = jnp.cos(ang)
    sin_ref[...] = jnp.where(lane < B_HD // 2, -jnp.sin(ang), jnp.sin(ang))


def _rope_tables(start, seq):
    half = B_HD // 2
    inv = ROPE_THETA ** (-jnp.arange(half, dtype=F32) / half)
    inv2 = jnp.concatenate([inv, inv]).reshape(1, B_HD)
    pos = (start + jnp.arange(seq, dtype=jnp.int32)).astype(F32).reshape(seq, 1)
    return pl.pallas_call(
        _rope_table_kernel,
        out_shape=[jax.ShapeDtypeStruct((seq, B_HD), F32)] * 2,
        name="rope_tables",
    )(pos, inv2)


_B_NG = B_WIDTH // COL_TILE


def _l1_inproj_kernel(x_ref, ada_ref, nw_ref, w_ref, qn_ref, kn_ref, cos_ref, sin_ref,
                      q_ref, kv_ref, zs_ref, h_scr):
    j = pl.program_id(1)

    @pl.when(j == 0)
    def _():
        h_scr[...] = _modulated_norm(x_ref, ada_ref, nw_ref)

    acc = _dot(h_scr[...], w_ref[...])
    tm = acc.shape[0]

    def norm_rope(w_ref_):
        cos = cos_ref[...]
        sin = sin_ref[...]
        reps = tm // cos.shape[0]
        if reps > 1:
            cos = jnp.broadcast_to(cos[None], (reps,) + cos.shape).reshape(tm, B_HD)
            sin = jnp.broadcast_to(sin[None], (reps,) + sin.shape).reshape(tm, B_HD)
        outs = []
        for h in range(COL_TILE // B_HD):
            xh = acc[:, h * B_HD:(h + 1) * B_HD]
            ms = jnp.mean(xh * xh, axis=-1, keepdims=True)
            y = xh * lax.rsqrt(ms + EPS) * w_ref_[...]
            outs.append(y * cos + pltpu.roll(y, B_HD // 2, 1) * sin)
        return jnp.concatenate(outs, axis=-1)

    @pl.when(j < _B_NG)
    def _():
        q_ref[...] = norm_rope(qn_ref).reshape(q_ref.shape)

    @pl.when((j >= _B_NG) & (j < 2 * _B_NG))
    def _():
        kv_ref[...] = norm_rope(kn_ref).reshape(kv_ref.shape)

    @pl.when((j >= 2 * _B_NG) & (j < 3 * _B_NG))
    def _():
        kv_ref[...] = acc.reshape(kv_ref.shape)

    @pl.when(j >= 3 * _B_NG)
    def _():
        zs_ref[...] = _silu(acc).reshape(zs_ref.shape)


def _kv_col(j):
    g_k = jnp.clip(j - _B_NG, 0, _B_NG - 1)
    g_v = jnp.clip(j - 2 * _B_NG, 0, _B_NG - 1)
    return jnp.where(j < 2 * _B_NG, 2 * g_k, 2 * g_v + 1)


def _l1_inproj(x, ada, nw, w_bf, qn, kn, cos, sin, batch, seq, tm):
    n_tiles, x_block, ada_block, col_block, _, shape, per = _token_specs(batch, seq, tm)
    n_col = w_bf.shape[1] // COL_TILE
    if per is None:
        tab_block = pl.BlockSpec((seq, B_HD), lambda i, j: (0, 0))
    else:
        tab_block = pl.BlockSpec((tm, B_HD), lambda i, j: (i % per, 0))
    vec_block = pl.BlockSpec((1, B_HD), lambda i, j: (0, 0))
    return pl.pallas_call(
        _l1_inproj_kernel,
        grid=(n_tiles, n_col),
        in_specs=[
            x_block(D_MODEL),
            ada_block,
            pl.BlockSpec((1, D_MODEL), lambda i, j: (0, 0)),
            pl.BlockSpec((D_MODEL, COL_TILE), lambda i, j: (0, j)),
            vec_block, vec_block, tab_block, tab_block,
        ],
        out_specs=[
            col_block(COL_TILE, _clip(0, _B_NG)),
            col_block(COL_TILE, _kv_col),
            col_block(COL_TILE, _clip(3 * _B_NG, _B_NG)),
        ],
        out_shape=[
            jax.ShapeDtypeStruct(shape(B_WIDTH), F32),
            jax.ShapeDtypeStruct(shape(2 * B_WIDTH), F32),
            jax.ShapeDtypeStruct(shape(B_WIDTH), F32),
        ],
        scratch_shapes=[pltpu.VMEM((tm, D_MODEL), BF16)],
        compiler_params=_params(("parallel", "arbitrary")),
        name="l1_inproj",
    )(x, ada, nw, w_bf, qn, kn, cos, sin)


Q_SUB = 128


def _softmax_parts(s, valid):
    s = jnp.where(valid, s, -jnp.inf)
    m = jnp.max(s, axis=-1, keepdims=True)
    p = jnp.exp(s - m)
    return m, p, jnp.sum(p, axis=-1, keepdims=True)


def _prompt_attn_kernel(q_ref, kc_ref, vc_ref, kp_ref, vp_ref, o_ref, lse_ref, *, tq):
    first_key = jnp.where(pl.program_id(2) == 0, Q_SUB, 0)
    row = lax.broadcasted_iota(jnp.int32, (Q_SUB, 2 * Q_SUB), 0)
    col = lax.broadcasted_iota(jnp.int32, (Q_SUB, 2 * Q_SUB), 1)
    band = (col >= row) & (col <= row + Q_SUB)
    scale = B_HD ** -0.5
    for h in range(B_HG):
        hc = slice(h * B_HD, (h + 1) * B_HD)
        for sb in range(tq // Q_SUB):
            rows = slice(sb * Q_SUB, (sb + 1) * Q_SUB)
            if sb == 0:
                k_prev, v_prev = kp_ref[0, :, hc], vp_ref[0, :, hc]
                valid = band & (col >= first_key)
            else:
                prev = slice((sb - 1) * Q_SUB, sb * Q_SUB)
                k_prev, v_prev = kc_ref[0, prev, hc], vc_ref[0, prev, hc]
                valid = band
            k2 = jnp.concatenate([k_prev, kc_ref[0, rows, hc]], axis=0).astype(BF16)
            v2 = jnp.concatenate([v_prev, vc_ref[0, rows, hc]], axis=0).astype(BF16)
            s = _dot(q_ref[0, rows, hc].astype(BF16), k2, _NT) * scale
            m, p, l = _softmax_parts(s, valid)
            o_ref[0, rows, hc] = _dot(p.astype(BF16), v2) / l
            lse_ref[0, rows, hc] = jnp.broadcast_to(m + jnp.log(l), (Q_SUB, B_HD))


def _prompt_attn_group(q, kv, g, dil, batch, seq):
    sub = seq // dil
    tq = min(512, sub)
    nq = sub // tq
    per_q = B_WIDTH // B_GW
    per_kv = 2 * B_WIDTH // B_GW
    q_v = q.reshape(batch, sub, dil * B_WIDTH)
    kv_v = kv.reshape(batch, sub, dil * 2 * B_WIDTH)
    cur = lambda off, per: pl.BlockSpec((1, tq, B_GW), lambda b, r, i: (b, i, r * per + off))
    prev = lambda off: pl.BlockSpec(
        (1, Q_SUB, B_GW), lambda b, r, i: (b, jnp.maximum(i * (tq // Q_SUB) - 1, 0), r * per_kv + off))
    o, lse = pl.pallas_call(
        functools.partial(_prompt_attn_kernel, tq=tq),
        grid=(batch, dil, nq),
        in_specs=[cur(g, per_q), cur(2 * g, per_kv), cur(2 * g + 1, per_kv), prev(2 * g), prev(2 * g + 1)],
        out_specs=[cur(0, 1), cur(0, 1)],
        out_shape=[jax.ShapeDtypeStruct((batch, sub, dil * B_GW), F32)] * 2,
        compiler_params=_params(("parallel", "parallel", "arbitrary")),
        name=f"prompt_attn_g{g}",
    )(q_v, kv_v, kv_v, kv_v, kv_v)
    return o.reshape(batch * seq, B_GW), lse.reshape(batch * seq, B_GW)


def _group_mix_kernel(o0, o1, o2, l0, l1, l2, zs_ref, out_ref):
    lses = [l0[...], l1[...], l2[...]]
    m = jnp.maximum(jnp.maximum(lses[0], lses[1]), lses[2])
    es = [jnp.exp(l - m) for l in lses]
    inv = 1.0 / (es[0] + es[1] + es[2])
    for g, o in enumerate((o0, o1, o2)):
        cols = slice(g * B_GW, (g + 1) * B_GW)
        out_ref[:, cols] = o[...] * (es[g] * inv) * zs_ref[:, cols]


def _group_mix(outs, lses, zs, tm):
    n = zs.shape[0]
    blk = pl.BlockSpec((tm, B_GW), lambda i: (i, 0))
    wide = pl.BlockSpec((tm, B_WIDTH), lambda i: (i, 0))
    return pl.pallas_call(
        _group_mix_kernel,
        grid=(n // tm,),
        in_specs=[blk] * 6 + [wide],
        out_specs=wide,
        out_shape=jax.ShapeDtypeStruct((n, B_WIDTH), F32),
        compiler_params=_params(("parallel",)),
        name="group_mix",
    )(*outs, *lses, zs)


def _sample_attn_kernel(q_ref, kv_ref, zs_ref, c0_ref, c1_ref, c2_ref, out_ref, *, seq):
    scale = B_HD ** -0.5
    caches = (c0_ref, c1_ref, c2_ref)
    for h in range(B_HG):
        outs, lses = [], []
        for g, (window, dil) in enumerate(B_GROUPS):
            hq = slice(g * B_GW + h * B_HD, g * B_GW + (h + 1) * B_HD)
            hk = slice(2 * g * B_GW + h * B_HD, 2 * g * B_GW + (h + 1) * B_HD)
            hv = slice((2 * g + 1) * B_GW + h * B_HD, (2 * g + 1) * B_GW + (h + 1) * B_HD)
            qh = q_ref[0, :, hq].astype(BF16)
            c_ref = caches[g]
            if dil >= seq:
                slabs = [(r, slice(r * 2 * B_GW + h * B_HD, r * 2 * B_GW + (h + 1) * B_HD),
                          slice(r * 2 * B_GW + B_GW + h * B_HD, r * 2 * B_GW + B_GW + (h + 1) * B_HD))
                         for r in range(seq)]
            else:
                slabs = [(None, slice(h * B_HD, (h + 1) * B_HD), slice(B_GW + h * B_HD, B_GW + (h + 1) * B_HD))]
            s_parts, valid_parts = [], []
            for r, kc, _ in slabs:
                kk = c_ref[0, :, kc].astype(BF16)
                s_parts.append(_dot(qh, kk, _NT) * scale)
                n_rows = kk.shape[0]
                qi = lax.broadcasted_iota(jnp.int32, (seq, n_rows), 0)
                rc = lax.broadcasted_iota(jnp.int32, (seq, n_rows), 1)
                if r is None:
                    valid_parts.append((rc >= qi) & (((rc - qi) & (dil - 1)) == 0))
                else:
                    valid_parts.append(qi == r)
            k_new = kv_ref[0, :, hk].astype(BF16)
            s_parts.append(_dot(qh, k_new, _NT) * scale)
            qi = lax.broadcasted_iota(jnp.int32, (seq, seq), 0)
            rn = lax.broadcasted_iota(jnp.int32, (seq, seq), 1)
            valid_parts.append((rn <= qi) & (((qi - rn) & (dil - 1)) == 0))

            s_parts = [jnp.where(v, s, -jnp.inf) for s, v in zip(s_parts, valid_parts)]
            m = functools.reduce(jnp.maximum, [jnp.max(s, axis=-1, keepdims=True) for s in s_parts])
            p_parts = [jnp.exp(s - m) for s in s_parts]
            l = functools.reduce(lambda a, b: a + b, [jnp.sum(p, axis=-1, keepdims=True) for p in p_parts])
            acc = _dot(p_parts[-1].astype(BF16), kv_ref[0, :, hv].astype(BF16))
            for (r, _, vc), p in zip(slabs, p_parts[:-1]):
                acc = acc + _dot(p.astype(BF16), c_ref[0, :, vc].astype(BF16))
            outs.append(acc / l)
            lses.append(m + jnp.log(l))
        m = functools.reduce(jnp.maximum, lses)
        es = [jnp.exp(l - m) for l in lses]
        inv = 1.0 / (es[0] + es[1] + es[2])
        for g in range(len(B_GROUPS)):
            hq = slice(g * B_GW + h * B_HD, g * B_GW + (h + 1) * B_HD)
            out_ref[0, :, hq] = outs[g] * (es[g] * inv) * zs_ref[0, :, hq]


def _sample_attn(q, kv, zs, caches, batch, seq):
    row = 2 * B_GW
    specs, views = [], []
    for (window, dil), c in zip(B_GROUPS, caches):
        assert window == dil * (N_KEYS - 1) and c.shape[1] == window
        if dil >= seq:
            views.append(c.reshape(batch, window // dil, dil * row))
            specs.append(pl.BlockSpec((1, window // dil, seq * row), lambda b: (b, 0, 0)))
        else:
            assert seq % dil == 0
            views.append(c)
            specs.append(pl.BlockSpec((1, window, row), lambda b: (b, 0, 0)))
    tok = lambda w: pl.BlockSpec((1, seq, w), lambda b: (b, 0, 0))
    return pl.pallas_call(
        functools.partial(_sample_attn_kernel, seq=seq),
        grid=(batch,),
        in_specs=[tok(B_WIDTH), tok(2 * B_WIDTH), tok(B_WIDTH)] + specs,
        out_specs=tok(B_WIDTH),
        out_shape=jax.ShapeDtypeStruct((batch, seq, B_WIDTH), F32),
        compiler_params=_params(("parallel",)),
        name="sample_attn",
    )(q, kv, zs, *views)


def _trunk(x, ada, start, state, caches, weights, tm, scan_tile, scan_chunk):
    (norm_w, alb, a_w_in, a_b_f, a_gw, a_w_out, b_w_in, b_qn, b_kn, b_w_out) = weights
    batch, seq, _ = x.shape
    flat = seq >= tm
    tok = (lambda a: a.reshape(batch * seq, a.shape[-1])) if flat else (lambda a: a)
    rows = lambda a: a.reshape(batch * seq, a.shape[-1])
    ada0 = ada[0].reshape(batch, 1, 3 * D_MODEL)
    ada1 = ada[1].reshape(batch, 1, 3 * D_MODEL)

    x0 = tok(x)
    q, k, lf, v, zs = _l0_inproj(x0, ada0, norm_w[0:1], a_w_in, alb, a_b_f, batch, seq, tm)
    og, s_new = _hgrn_scan(rows(q), rows(k), rows(lf), rows(v), rows(zs), a_gw, state,
                           batch, seq, scan_tile, scan_chunk)
    og = og if flat else og.reshape(batch, seq, A_V)
    x1 = _outproj(og, a_w_out, x0, ada0, batch, seq, tm)

    cos, sin = _rope_tables(start, seq)
    qr, kv, zs1 = _l1_inproj(x1, ada1, norm_w[1:2], b_w_in, b_qn, b_kn, cos, sin, batch, seq, tm)
    if caches is None:
        outs, lses = [], []
        for g, (_, dil) in enumerate(B_GROUPS):
            o, lse = _prompt_attn_group(qr, kv, g, dil, batch, seq)
            outs.append(o)
            lses.append(lse)
        og1 = _group_mix(outs, lses, zs1, tm)
    else:
        og1 = _sample_attn(qr, kv, zs1, caches, batch, seq)
    x2 = _outproj(og1, b_w_out, x1, ada1, batch, seq, tm)

    kv_rows = kv.reshape(batch, seq, len(B_GROUPS), 2, B_HG, B_HD)
    return x2.reshape(batch, seq, D_MODEL), s_new, kv_rows


def kernel(x_prompt, x_sample, state_hgrn, cache_kv_w128, cache_kv_w512, cache_kv_w2048, c_prompt, c_sample,
           norm_w, ada_w, ada_b, a_lower_bounds, a_w_in, a_b_f, a_g_norm_w, a_w_out, b_w_in, b_q_norm_w,
           b_k_norm_w, b_w_out):
    bp, lp, _ = x_prompt.shape
    bs, ls, _ = x_sample.shape

    pad = (-(bp + bs)) % 8
    c_all = jnp.concatenate([c_prompt, c_sample, jnp.zeros((pad, D_MODEL), F32)], axis=0)
    ada = _ada_vectors(c_all, ada_w, ada_b)
    ada_p, ada_s = ada[:, :bp], ada[:, bp:bp + bs]

    weights = (norm_w, a_lower_bounds, a_w_in[0].astype(BF16), a_b_f, a_g_norm_w, a_w_out[0].astype(BF16),
               b_w_in[0].astype(BF16), b_q_norm_w, b_k_norm_w, b_w_out[0].astype(BF16))

    y_p, s_p, kv_p = _trunk(x_prompt, ada_p, 0, None, None, weights,
                            tm=512, scan_tile=256, scan_chunk=64)
    caches = tuple(c[0].reshape(bs, c.shape[2], 2 * B_GW)
                   for c in (cache_kv_w128, cache_kv_w512, cache_kv_w2048))
    y_s, s_s, kv_s = _trunk(x_sample, ada_s, PAST_LEN, state_hgrn[0], caches, weights,
                            tm=bs * ls, scan_tile=ls, scan_chunk=ls)

    kv_out = []
    for g, (window, _) in enumerate(B_GROUPS):
        kv_out.append(kv_p[:, lp - min(window, lp):, g][None])
        kv_out.append(kv_s[:, :, g][None])
    return (y_p, y_s, s_p[None], s_s[None], *kv_out)
```

```python
import functools
import math

import numpy as np
import jax
import jax.numpy as jnp
from jax import lax
from jax.experimental import pallas as pl
from jax.experimental.pallas import tpu as pltpu

F32 = jnp.float32
BF16 = jnp.bfloat16

D_MODEL = 1024
EPS = 1e-6
A_HEADS = 8
A_KDIM = 128
A_VDIM = 256
A_QK = A_HEADS * A_KDIM
A_V = A_HEADS * A_VDIM
B_GROUPS = ((128, 1), (512, 4), (2048, 16))
B_HG = 4
B_HD = 128
B_GW = B_HG * B_HD
B_WIDTH = len(B_GROUPS) * B_GW
ROPE_THETA = 10000.0
PAST_LEN = 2048
N_KEYS = 129

COL_TILE = 512
VMEM_LIMIT = 56 * 1024 * 1024

_NT = (((1,), (1,)), ((), ()))
_TN = (((0,), (0,)), ((), ()))


def _sigmoid(x):
    return 1.0 / (1.0 + jnp.exp(-x))


def _silu(x):
    return x * _sigmoid(x)


def _dot(a, b, dims=None):
    if dims is None:
        return jnp.dot(a, b, preferred_element_type=F32)
    return lax.dot_general(a, b, dims, preferred_element_type=F32)


def _params(sem):
    return pltpu.CompilerParams(dimension_semantics=sem, vmem_limit_bytes=VMEM_LIMIT)


def _ada_kernel(c_ref, w_ref, b_ref, o_ref):
    a = _silu(c_ref[...]).astype(BF16)
    o_ref[0] = _dot(a, w_ref[0].astype(BF16)) + b_ref[0]


def _ada_vectors(c_all, ada_w, ada_b):
    n_layers, _, width = ada_w.shape
    rows = c_all.shape[0]
    return pl.pallas_call(
        _ada_kernel,
        grid=(n_layers, width // COL_TILE),
        in_specs=[
            pl.BlockSpec((rows, D_MODEL), lambda l, j: (0, 0)),
            pl.BlockSpec((1, D_MODEL, COL_TILE), lambda l, j: (l, 0, j)),
            pl.BlockSpec((1, 1, COL_TILE), lambda l, j: (l, 0, j)),
        ],
        out_specs=pl.BlockSpec((1, rows, COL_TILE), lambda l, j: (l, 0, j)),
        out_shape=jax.ShapeDtypeStruct((n_layers, rows, width), F32),
        compiler_params=_params(("parallel", "parallel")),
        name="ada_vectors",
    )(c_all, ada_w, ada_b.reshape(n_layers, 1, width))


def _modulated_norm(x_ref, ada_ref, nw_ref):
    x = x_ref[...]
    ada = ada_ref[...]
    if x.ndim == 2:
        ada = ada.reshape(1, ada.shape[-1])
    shift = ada[..., :D_MODEL]
    scale = ada[..., D_MODEL:2 * D_MODEL]
    ms = jnp.mean(x * x, axis=-1, keepdims=True)
    y = x * lax.rsqrt(ms + EPS) * nw_ref[...].reshape((1,) * (x.ndim - 1) + (D_MODEL,))
    h = y * (1.0 + scale) + shift
    return h.reshape(-1, D_MODEL).astype(BF16)


def _token_specs(batch, seq, tm):
    if seq >= tm:
        assert seq % tm == 0
        per = seq // tm
        n_tiles = batch * per
        x_block = lambda w: pl.BlockSpec((tm, w), lambda i, j: (i, 0))
        ada_block = pl.BlockSpec((1, 1, 3 * D_MODEL), lambda i, j: (i // per, 0, 0))
        col_block = lambda w, f: pl.BlockSpec((tm, w), lambda i, j: (i, f(j)))
        gate_block = lambda w, f: pl.BlockSpec((1, 1, w), lambda i, j: (i // per, 0, f(j)))
        shape = lambda w: (batch * seq, w)
        return n_tiles, x_block, ada_block, col_block, gate_block, shape, per
    tb = tm // seq
    assert batch % tb == 0
    n_tiles = batch // tb
    x_block = lambda w: pl.BlockSpec((tb, seq, w), lambda i, j: (i, 0, 0))
    ada_block = pl.BlockSpec((tb, 1, 3 * D_MODEL), lambda i, j: (i, 0, 0))
    col_block = lambda w, f: pl.BlockSpec((tb, seq, w), lambda i, j: (i, 0, f(j)))
    gate_block = lambda w, f: pl.BlockSpec((tb, 1, w), lambda i, j: (i, 0, f(j)))
    shape = lambda w: (batch, seq, w)
    return n_tiles, x_block, ada_block, col_block, gate_block, shape, None


def _clip(lo, n):
    return lambda j: jnp.clip(j - lo, 0, n - 1)


_A_NQ = A_QK // COL_TILE
_A_NF = A_QK // COL_TILE
_A_NV = A_V // COL_TILE
_A_NZ = A_V // COL_TILE


def _l0_inproj_kernel(x_ref, ada_ref, nw_ref, w_ref, alb_ref, bf_ref,
                      q_ref, k_ref, lf_ref, v_ref, zs_ref, h_scr):
    j = pl.program_id(1)

    @pl.when(j == 0)
    def _():
        h_scr[...] = _modulated_norm(x_ref, ada_ref, nw_ref)

    acc = _dot(h_scr[...], w_ref[...])

    @pl.when(j < _A_NQ)
    def _():
        q_ref[...] = _silu(acc).reshape(q_ref.shape)

    @pl.when((j >= _A_NQ) & (j < _A_NQ + _A_NF))
    def _():
        a = alb_ref[...]
        m = jnp.max(a, axis=0, keepdims=True)
        e = jnp.exp(a - m)
        lb = e[0:1] / jnp.sum(e, axis=0, keepdims=True)
        f = lb + (1.0 - lb) * _sigmoid(acc + bf_ref[...])
        lf_ref[...] = jnp.log(f).reshape(lf_ref.shape)
        k_ref[...] = (1.0 - f).reshape(k_ref.shape)

    @pl.when((j >= _A_NQ + _A_NF) & (j < _A_NQ + _A_NF + _A_NV))
    def _():
        v_ref[...] = acc.reshape(v_ref.shape)

    @pl.when(j >= _A_NQ + _A_NF + _A_NV)
    def _():
        zs_ref[...] = _silu(acc).reshape(zs_ref.shape)


def _l0_inproj(x, ada, nw, w_bf, alb, b_f, batch, seq, tm):
    n_tiles, x_block, ada_block, col_block, _, shape, _ = _token_specs(batch, seq, tm)
    n_col = w_bf.shape[1] // COL_TILE
    f0, v0, z0 = _A_NQ, _A_NQ + _A_NF, _A_NQ + _A_NF + _A_NV
    return pl.pallas_call(
        _l0_inproj_kernel,
        grid=(n_tiles, n_col),
        in_specs=[
            x_block(D_MODEL),
            ada_block,
            pl.BlockSpec((1, D_MODEL), lambda i, j: (0, 0)),
            pl.BlockSpec((D_MODEL, COL_TILE), lambda i, j: (0, j)),
            pl.BlockSpec((alb.shape[0], COL_TILE), lambda i, j: (0, _clip(f0, _A_NF)(j))),
            pl.BlockSpec((1, COL_TILE), lambda i, j: (0, _clip(f0, _A_NF)(j))),
        ],
        out_specs=[
            col_block(COL_TILE, _clip(0, _A_NQ)),
            col_block(COL_TILE, _clip(f0, _A_NF)),
            col_block(COL_TILE, _clip(f0, _A_NF)),
            col_block(COL_TILE, _clip(v0, _A_NV)),
            col_block(COL_TILE, _clip(z0, _A_NZ)),
        ],
        out_shape=[
            jax.ShapeDtypeStruct(shape(A_QK), F32),
            jax.ShapeDtypeStruct(shape(A_QK), F32),
            jax.ShapeDtypeStruct(shape(A_QK), F32),
            jax.ShapeDtypeStruct(shape(A_V), F32),
            jax.ShapeDtypeStruct(shape(A_V), F32),
        ],
        scratch_shapes=[pltpu.VMEM((tm, D_MODEL), BF16)],
        compiler_params=_params(("parallel", "arbitrary")),
        name="l0_inproj",
    )(x, ada, nw, w_bf, alb, b_f)


def _scan_levels(chunk):
    return [1 << l for l in range(int(math.log2(chunk)))]


def _exponent_matrix(chunk):
    r = np.arange(chunk)[:, None]
    c = np.arange(chunk)[None, :]
    blocks = [c <= r, c > r]
    for s in _scan_levels(chunk):
        bound = (r & ~(2 * s - 1)) | s
        upper = (r & s) != 0
        blocks.append(np.where(upper, (c > bound) & (c <= r), (c > r) & (c <= bound)))
    return np.concatenate(blocks, axis=0).astype(np.float32)


def _hgrn_chunk(q, k, lf, v, state, emat, chunk):
    dk = q.shape[-1]
    ex = jnp.exp(jnp.dot(emat, lf, preferred_element_type=F32, precision=lax.Precision.HIGHEST))
    e_b = ex[0:chunk]
    e_u = ex[chunk:2 * chunk]
    row = lax.broadcasted_iota(jnp.int32, (chunk, 1), 0)
    col = lax.broadcasted_iota(jnp.int32, (1, chunk), 1)

    att = jnp.where(row == col, jnp.sum(q * k, axis=-1, keepdims=True), 0.0)
    for l, s in enumerate(_scan_levels(chunk)):
        g = ex[(2 + l) * chunk:(3 + l) * chunk]
        upper = (row & s) != 0
        xb = (jnp.where(upper, q, k) * g).astype(BF16)
        part = _dot(xb, xb, _NT)
        keep = upper & ((col & s) == 0) & ((row >> (l + 1)) == (col >> (l + 1)))
        att = att + jnp.where(keep, part, 0.0)

    o = _dot(att.astype(BF16), v.astype(BF16)) + _dot((q * e_b).astype(BF16), state.astype(BF16))

    e_last = e_b[chunk - 1:chunk]
    eye = lax.broadcasted_iota(jnp.int32, (dk, dk), 0) == lax.broadcasted_iota(jnp.int32, (dk, dk), 1)
    e_col = jnp.sum(jnp.where(eye, e_last, 0.0), axis=-1, keepdims=True)
    new_state = state * e_col + _dot((k * e_u).astype(BF16), v.astype(BF16), _TN)
    return o, new_state


def _hgrn_scan_kernel(*refs, chunk, n_chunks, has_state):
    if has_state:
        q_ref, k_ref, lf_ref, v_ref, zs_ref, gw_ref, e_ref, s0_ref, og_ref, so_ref, s_scr = refs
    else:
        q_ref, k_ref, lf_ref, v_ref, zs_ref, gw_ref, e_ref, og_ref, so_ref, s_scr = refs
    t = pl.program_id(1)

    @pl.when(t == 0)
    def _():
        if has_state:
            s_scr[...] = s0_ref[0]
        else:
            s_scr[...] = jnp.zeros_like(s_scr)

    emat = e_ref[...]
    gw = gw_ref[...]

    def body(c, carry):
        rows = slice(0, chunk) if n_chunks == 1 else pl.ds(pl.multiple_of(c * chunk, chunk), chunk)
        for h in range(A_HEADS):
            kc = slice(h * A_KDIM, (h + 1) * A_KDIM)
            vc = slice(h * A_VDIM, (h + 1) * A_VDIM)
            o, s_new = _hgrn_chunk(q_ref[rows, kc], k_ref[rows, kc], lf_ref[rows, kc],
                                   v_ref[rows, vc], s_scr[h], emat, chunk)
            s_scr[h] = s_new
            ms = jnp.mean(o * o, axis=-1, keepdims=True)
            og_ref[rows, vc] = o * lax.rsqrt(ms + EPS) * gw * zs_ref[rows, vc]
        return carry

    if n_chunks == 1:
        body(0, 0)
    else:
        lax.fori_loop(0, n_chunks, body, 0)

    @pl.when(t == pl.num_programs(1) - 1)
    def _():
        so_ref[0] = s_scr[...]


def _hgrn_scan(q, k, lf, v, zs, gw, s0, batch, seq, tile, chunk):
    n_t = seq // tile
    emat = jnp.asarray(_exponent_matrix(chunk))
    row_block = lambda w: pl.BlockSpec((tile, w), lambda b, t: (b * n_t + t, 0))
    state_block = pl.BlockSpec((1, A_HEADS, A_KDIM, A_VDIM), lambda b, t: (b, 0, 0, 0))
    in_specs = [row_block(A_QK), row_block(A_QK), row_block(A_QK), row_block(A_V), row_block(A_V),
                pl.BlockSpec((1, A_VDIM), lambda b, t: (0, 0)),
                pl.BlockSpec(emat.shape, lambda b, t: (0, 0))]
    args = [q, k, lf, v, zs, gw, emat]
    if s0 is not None:
        in_specs.append(state_block)
        args.append(s0)
    kern = functools.partial(_hgrn_scan_kernel, chunk=chunk, n_chunks=tile // chunk,
                             has_state=s0 is not None)
    return pl.pallas_call(
        kern,
        grid=(batch, n_t),
        in_specs=in_specs,
        out_specs=[row_block(A_V), state_block],
        out_shape=[jax.ShapeDtypeStruct((batch * seq, A_V), F32),
                   jax.ShapeDtypeStruct((batch, A_HEADS, A_KDIM, A_VDIM), F32)],
        scratch_shapes=[pltpu.VMEM((A_HEADS, A_KDIM, A_VDIM), F32)],
        compiler_params=_params(("parallel", "arbitrary")),
        name="hgrn_scan",
    )(*args)


def _outproj_kernel(g_ref, w_ref, x_ref, gate_ref, o_ref):
    g = g_ref[...]
    y = _dot(g.reshape(-1, g.shape[-1]).astype(BF16), w_ref[...])
    x = x_ref[...]
    gate = gate_ref[...]
    if x.ndim == 2:
        gate = gate.reshape(1, gate.shape[-1])
    o_ref[...] = x + gate * y.reshape(x.shape)


def _outproj(g, w_bf, x, ada, batch, seq, tm):
    n_tiles, x_block, _, col_block, gate_block, shape, _ = _token_specs(batch, seq, tm)
    kdim = w_bf.shape[0]
    n_col = D_MODEL // COL_TILE
    gate0 = 2 * D_MODEL // COL_TILE
    return pl.pallas_call(
        _outproj_kernel,
        grid=(n_tiles, n_col),
        in_specs=[
            x_block(kdim),
            pl.BlockSpec((kdim, COL_TILE), lambda i, j: (0, j)),
            col_block(COL_TILE, lambda j: j),
            gate_block(COL_TILE, lambda j: gate0 + j),
        ],
        out_specs=col_block(COL_TILE, lambda j: j),
        out_shape=jax.ShapeDtypeStruct(shape(D_MODEL), F32),
        compiler_params=_params(("parallel", "arbitrary")),
        name="outproj",
    )(g, w_bf, x, ada)


def _rope_table_kernel(pos_ref, inv_ref, cos_ref, sin_ref):
    ang = pos_ref[...] * inv_ref[...]
    lane = lax.broadcasted_iota(jnp.int32, ang.shape, 1)
    cos_ref[...] = jnp.cos(ang)
    sin_ref[...] = jnp.where(lane < B_HD // 2, -jnp.sin(ang), jnp.sin(ang))


def _rope_tables(start, seq):
    half = B_HD // 2
    inv = ROPE_THETA ** (-jnp.arange(half, dtype=F32) / half)
    inv2 = jnp.concatenate([inv, inv]).reshape(1, B_HD)
    pos = (start + jnp.arange(seq, dtype=jnp.int32)).astype(F32).reshape(seq, 1)
    return pl.pallas_call(
        _rope_table_kernel,
        out_shape=[jax.ShapeDtypeStruct((seq, B_HD), F32)] * 2,
        name="rope_tables",
    )(pos, inv2)


_B_NG = B_WIDTH // COL_TILE


def _l1_inproj_kernel(x_ref, ada_ref, nw_ref, w_ref, qn_ref, kn_ref, cos_ref, sin_ref,
                      q_ref, kv_ref, zs_ref, h_scr):
    j = pl.program_id(1)

    @pl.when(j == 0)
    def _():
        h_scr[...] = _modulated_norm(x_ref, ada_ref, nw_ref)

    acc = _dot(h_scr[...], w_ref[...])
    tm = acc.shape[0]

    def norm_rope(w_ref_):
        cos = cos_ref[...]
        sin = sin_ref[...]
        reps = tm // cos.shape[0]
        if reps > 1:
            cos = jnp.broadcast_to(cos[None], (reps,) + cos.shape).reshape(tm, B_HD)
            sin = jnp.broadcast_to(sin[None], (reps,) + sin.shape).reshape(tm, B_HD)
        outs = []
        for h in range(COL_TILE // B_HD):
            xh = acc[:, h * B_HD:(h + 1) * B_HD]
            ms = jnp.mean(xh * xh, axis=-1, keepdims=True)
            y = xh * lax.rsqrt(ms + EPS) * w_ref_[...]
            outs.append(y * cos + pltpu.roll(y, B_HD // 2, 1) * sin)
        return jnp.concatenate(outs, axis=-1)

    @pl.when(j < _B_NG)
    def _():
        q_ref[...] = norm_rope(qn_ref).reshape(q_ref.shape)

    @pl.when((j >= _B_NG) & (j < 2 * _B_NG))
    def _():
        kv_ref[...] = norm_rope(kn_ref).reshape(kv_ref.shape)

    @pl.when((j >= 2 * _B_NG) & (j < 3 * _B_NG))
    def _():
        kv_ref[...] = acc.reshape(kv_ref.shape)

    @pl.when(j >= 3 * _B_NG)
    def _():
        zs_ref[...] = _silu(acc).reshape(zs_ref.shape)


def _kv_col(j):
    g_k = jnp.clip(j - _B_NG, 0, _B_NG - 1)
    g_v = jnp.clip(j - 2 * _B_NG, 0, _B_NG - 1)
    return jnp.where(j < 2 * _B_NG, 2 * g_k, 2 * g_v + 1)


def _l1_inproj(x, ada, nw, w_bf, qn, kn, cos, sin, batch, seq, tm):
    n_tiles, x_block, ada_block, col_block, _, shape, per = _token_specs(batch, seq, tm)
    n_col = w_bf.shape[1] // COL_TILE
    if per is None:
        tab_block = pl.BlockSpec((seq, B_HD), lambda i, j: (0, 0))
    else:
        tab_block = pl.BlockSpec((tm, B_HD), lambda i, j: (i % per, 0))
    vec_block = pl.BlockSpec((1, B_HD), lambda i, j: (0, 0))
    return pl.pallas_call(
        _l1_inproj_kernel,
        grid=(n_tiles, n_col),
        in_specs=[
            x_block(D_MODEL),
            ada_block,
            pl.BlockSpec((1, D_MODEL), lambda i, j: (0, 0)),
            pl.BlockSpec((D_MODEL, COL_TILE), lambda i, j: (0, j)),
            vec_block, vec_block, tab_block, tab_block,
        ],
        out_specs=[
            col_block(COL_TILE, _clip(0, _B_NG)),
            col_block(COL_TILE, _kv_col),
            col_block(COL_TILE, _clip(3 * _B_NG, _B_NG)),
        ],
        out_shape=[
            jax.ShapeDtypeStruct(shape(B_WIDTH), F32),
            jax.ShapeDtypeStruct(shape(2 * B_WIDTH), F32),
            jax.ShapeDtypeStruct(shape(B_WIDTH), F32),
        ],
        scratch_shapes=[pltpu.VMEM((tm, D_MODEL), BF16)],
        compiler_params=_params(("parallel", "arbitrary")),
        name="l1_inproj",
    )(x, ada, nw, w_bf, qn, kn, cos, sin)


Q_SUB = 128


def _softmax_parts(s, valid):
    s = jnp.where(valid, s, -jnp.inf)
    m = jnp.max(s, axis=-1, keepdims=True)
    p = jnp.exp(s - m)
    return m, p, jnp.sum(p, axis=-1, keepdims=True)


def _prompt_attn_kernel(q_ref, kc_ref, vc_ref, kp_ref, vp_ref, o_ref, lse_ref, *, tq, dil):
    first_key = jnp.where(pl.program_id(2) == 0, Q_SUB, 0)
    row = lax.broadcasted_iota(jnp.int32, (Q_SUB, 2 * Q_SUB), 0)
    col = lax.broadcasted_iota(jnp.int32, (Q_SUB, 2 * Q_SUB), 1)
    band = (col >= row) & (col <= row + Q_SUB)
    scale = B_HD ** -0.5

    def rows_of(r, first, n):
        start = r + first * dil
        return pl.ds(start, n) if dil == 1 else pl.ds(start, n, stride=dil)

    for r in range(dil):
        for sb in range(tq // Q_SUB):
            rows = rows_of(r, sb * Q_SUB, Q_SUB)
            if sb == 0:
                prev = rows_of(r, 0, Q_SUB)
                k_prev, v_prev = kp_ref[0, prev, :], vp_ref[0, prev, :]
                valid = band & (col >= first_key)
            else:
                prev = rows_of(r, (sb - 1) * Q_SUB, Q_SUB)
                k_prev, v_prev = kc_ref[0, prev, :], vc_ref[0, prev, :]
                valid = band
            k2 = jnp.concatenate([k_prev, kc_ref[0, rows, :]], axis=0).astype(BF16)
            v2 = jnp.concatenate([v_prev, vc_ref[0, rows, :]], axis=0).astype(BF16)
            s = _dot(q_ref[0, rows, :].astype(BF16), k2, _NT) * scale
            m, p, l = _softmax_parts(s, valid)
            o_ref[0, rows, :] = _dot(p.astype(BF16), v2) / l
            lse_ref[0, rows, :] = jnp.broadcast_to(m + jnp.log(l), (Q_SUB, B_HD))


def _prompt_attn_group(q, kv, g, dil, batch, seq):
    tq = min(512, seq // dil)
    tile = tq * dil
    back = Q_SUB * dil
    n_tiles = seq // tile
    q0, k0, v0 = g * B_HG, 2 * g * B_HG, (2 * g + 1) * B_HG
    cur = lambda c0: pl.BlockSpec((1, tile, B_HD), lambda b, h, i: (b, i, c0 + h))
    prev = lambda c0: pl.BlockSpec(
        (1, back, B_HD), lambda b, h, i: (b, jnp.maximum(i * (tile // back) - 1, 0), c0 + h))
    out = pl.BlockSpec((1, tile, B_HD), lambda b, h, i: (b, i, h))
    return pl.pallas_call(
        functools.partial(_prompt_attn_kernel, tq=tq, dil=dil),
        grid=(batch, B_HG, n_tiles),
        in_specs=[cur(q0), cur(k0), cur(v0), prev(k0), prev(v0)],
        out_specs=[out, out],
        out_shape=[jax.ShapeDtypeStruct((batch, seq, B_GW), F32)] * 2,
        compiler_params=_params(("parallel", "parallel", "arbitrary")),
        name=f"prompt_attn_g{g}",
    )(q, kv, kv, kv, kv)


def _group_mix_kernel(o0, o1, o2, l0, l1, l2, zs_ref, out_ref):
    lses = [l0[...], l1[...], l2[...]]
    m = jnp.maximum(jnp.maximum(lses[0], lses[1]), lses[2])
    es = [jnp.exp(l - m) for l in lses]
    inv = 1.0 / (es[0] + es[1] + es[2])
    for g, o in enumerate((o0, o1, o2)):
        cols = slice(g * B_GW, (g + 1) * B_GW)
        out_ref[:, cols] = o[...] * (es[g] * inv) * zs_ref[:, cols]


def _group_mix(outs, lses, zs, tm):
    n = zs.shape[0]
    blk = pl.BlockSpec((tm, B_GW), lambda i: (i, 0))
    wide = pl.BlockSpec((tm, B_WIDTH), lambda i: (i, 0))
    return pl.pallas_call(
        _group_mix_kernel,
        grid=(n // tm,),
        in_specs=[blk] * 6 + [wide],
        out_specs=wide,
        out_shape=jax.ShapeDtypeStruct((n, B_WIDTH), F32),
        compiler_params=_params(("parallel",)),
        name="group_mix",
    )(*outs, *lses, zs)


KV_ROWS = 2 * B_HG


def _sample_masks(seq):
    hq = np.arange(B_HG * seq)[:, None] // seq
    iq = np.arange(B_HG * seq)[:, None] % seq
    masks = []
    for window, dil in B_GROUPS:
        n_tok = window if dil < seq else (window // dil) * seq
        col = np.arange(n_tok * KV_ROWS)[None, :]
        tok, is_v, head = col // KV_ROWS, (col // B_HG) % 2, col % B_HG
        if dil < seq:
            ok = (tok >= iq) & ((tok - iq) % dil == 0)
        else:
            ok = (tok % seq) == iq
        masks.append(np.where(ok & (is_v == 0) & (head == hq), 0.0, -np.inf).astype(np.float32))
    col = np.arange(B_HG * seq)[None, :]
    new = []
    for _, dil in B_GROUPS:
        ok = (col // seq == hq) & (col % seq <= iq) & ((iq - col % seq) % dil == 0)
        new.append(np.where(ok, 0.0, -np.inf).astype(np.float32))
    return masks, np.stack(new)


def _sample_attn_kernel(q_ref, kv_ref, zs_ref, c0_ref, c1_ref, c2_ref, m0_ref, m1_ref, m2_ref, mn_ref,
                        out_ref, *, seq):
    scale = B_HD ** -0.5
    caches = (c0_ref, c1_ref, c2_ref)
    masks = (m0_ref, m1_ref, m2_ref)
    heads = lambda ref, c0: jnp.concatenate(
        [ref[0, :, c0 + h * B_HD:c0 + (h + 1) * B_HD] for h in range(B_HG)], axis=0).astype(BF16)
    outs, lses = [], []
    for g in range(len(B_GROUPS)):
        qa = heads(q_ref, g * B_GW)
        k_new = heads(kv_ref, 2 * g * B_GW)
        v_new = heads(kv_ref, (2 * g + 1) * B_GW)
        rows = caches[g][0].reshape(-1, B_HD).astype(BF16)
        s_c = _dot(qa, rows, _NT) * scale + masks[g][...]
        s_n = _dot(qa, k_new, _NT) * scale + mn_ref[g]
        m = jnp.maximum(jnp.max(s_c, axis=-1, keepdims=True), jnp.max(s_n, axis=-1, keepdims=True))
        p_c = jnp.exp(s_c - m)
        p_n = jnp.exp(s_n - m)
        l = jnp.sum(p_c, axis=-1, keepdims=True) + jnp.sum(p_n, axis=-1, keepdims=True)
        p_v = pltpu.roll(p_c, B_HG, 1)
        acc = _dot(p_v.astype(BF16), rows) + _dot(p_n.astype(BF16), v_new)
        outs.append(acc / l)
        lses.append(m + jnp.log(l))
    m = functools.reduce(jnp.maximum, lses)
    es = [jnp.exp(l - m) for l in lses]
    inv = 1.0 / (es[0] + es[1] + es[2])
    for g in range(len(B_GROUPS)):
        mixed = outs[g] * (es[g] * inv)
        for h in range(B_HG):
            cols = slice(g * B_GW + h * B_HD, g * B_GW + (h + 1) * B_HD)
            out_ref[0, :, cols] = mixed[h * seq:(h + 1) * seq] * zs_ref[0, :, cols]


def _sample_attn(q, kv, zs, caches, batch, seq):
    masks, mask_new = _sample_masks(seq)
    specs, views = [], []
    for (window, dil), c in zip(B_GROUPS, caches):
        assert window == dil * (N_KEYS - 1) and c.shape[2] == window
        if dil >= seq:
            views.append(c.reshape(batch, window // dil, dil * KV_ROWS, B_HD))
            specs.append(pl.BlockSpec((1, window // dil, seq * KV_ROWS, B_HD), lambda b: (b, 0, 0, 0)))
        else:
            assert seq % dil == 0
            views.append(c.reshape(batch, window * KV_ROWS, B_HD))
            specs.append(pl.BlockSpec((1, window * KV_ROWS, B_HD), lambda b: (b, 0, 0)))
    const = lambda a: pl.BlockSpec(a.shape, lambda b: (0,) * a.ndim)
    tok = lambda w: pl.BlockSpec((1, seq, w), lambda b: (b, 0, 0))
    return pl.pallas_call(
        functools.partial(_sample_attn_kernel, seq=seq),
        grid=(batch,),
        in_specs=[tok(B_WIDTH), tok(2 * B_WIDTH), tok(B_WIDTH)] + specs
                 + [const(a) for a in masks] + [const(mask_new)],
        out_specs=tok(B_WIDTH),
        out_shape=jax.ShapeDtypeStruct((batch, seq, B_WIDTH), F32),
        compiler_params=_params(("parallel",)),
        name="sample_attn",
    )(q, kv, zs, *views, *[jnp.asarray(a) for a in masks], jnp.asarray(mask_new))


def _trunk(x, ada, start, state, caches, weights, tm, scan_tile, scan_chunk):
    (norm_w, alb, a_w_in, a_b_f, a_gw, a_w_out, b_w_in, b_qn, b_kn, b_w_out) = weights
    batch, seq, _ = x.shape
    flat = seq >= tm
    tok = (lambda a: a.reshape(batch * seq, a.shape[-1])) if flat else (lambda a: a)
    rows = lambda a: a.reshape(batch * seq, a.shape[-1])
    ada0 = ada[0].reshape(batch, 1, 3 * D_MODEL)
    ada1 = ada[1].reshape(batch, 1, 3 * D_MODEL)

    x0 = tok(x)
    q, k, lf, v, zs = _l0_inproj(x0, ada0, norm_w[0:1], a_w_in, alb, a_b_f, batch, seq, tm)
    og, s_new = _hgrn_scan(rows(q), rows(k), rows(lf), rows(v), rows(zs), a_gw, state,
                           batch, seq, scan_tile, scan_chunk)
    og = og if flat else og.reshape(batch, seq, A_V)
    x1 = _outproj(og, a_w_out, x0, ada0, batch, seq, tm)

    cos, sin = _rope_tables(start, seq)
    qr, kv, zs1 = _l1_inproj(x1, ada1, norm_w[1:2], b_w_in, b_qn, b_kn, cos, sin, batch, seq, tm)
    kv3 = kv.reshape(batch, seq, 2 * B_WIDTH)
    if caches is None:
        outs, lses = [], []
        for g, (_, dil) in enumerate(B_GROUPS):
            o, lse = _prompt_attn_group(qr.reshape(batch, seq, B_WIDTH), kv3, g, dil, batch, seq)
            outs.append(o.reshape(batch * seq, B_GW))
            lses.append(lse.reshape(batch * seq, B_GW))
        og1 = _group_mix(outs, lses, zs1, tm)
    else:
        og1 = _sample_attn(qr, kv, zs1, caches, batch, seq)
    x2 = _outproj(og1, b_w_out, x1, ada1, batch, seq, tm)

    kv_rows = [kv3[:, seq - min(window, seq):, 2 * g * B_GW:2 * (g + 1) * B_GW]
               .reshape(batch, min(window, seq), 2, B_HG, B_HD) for g, (window, _) in enumerate(B_GROUPS)]
    return x2.reshape(batch, seq, D_MODEL), s_new, kv_rows


def kernel(x_prompt, x_sample, state_hgrn, cache_kv_w128, cache_kv_w512, cache_kv_w2048, c_prompt, c_sample,
           norm_w, ada_w, ada_b, a_lower_bounds, a_w_in, a_b_f, a_g_norm_w, a_w_out, b_w_in, b_q_norm_w,
           b_k_norm_w, b_w_out):
    bp, lp, _ = x_prompt.shape
    bs, ls, _ = x_sample.shape

    pad = (-(bp + bs)) % 8
    c_all = jnp.concatenate([c_prompt, c_sample, jnp.zeros((pad, D_MODEL), F32)], axis=0)
    ada = _ada_vectors(c_all, ada_w, ada_b)
    ada_p, ada_s = ada[:, :bp], ada[:, bp:bp + bs]

    weights = (norm_w, a_lower_bounds, a_w_in[0].astype(BF16), a_b_f, a_g_norm_w, a_w_out[0].astype(BF16),
               b_w_in[0].astype(BF16), b_q_norm_w, b_k_norm_w, b_w_out[0].astype(BF16))

    y_p, s_p, kv_p = _trunk(x_prompt, ada_p, 0, None, None, weights,
                            tm=512, scan_tile=256, scan_chunk=64)
    caches = (cache_kv_w128, cache_kv_w512, cache_kv_w2048)
    y_s, s_s, kv_s = _trunk(x_sample, ada_s, PAST_LEN, state_hgrn[0], caches, weights,
                            tm=bs * ls, scan_tile=ls, scan_chunk=ls)

    kv_out = []
    for g in range(len(B_GROUPS)):
        kv_out.append(kv_p[g][None])
        kv_out.append(kv_s[g][None])
    return (y_p, y_s, s_p[None], s_s[None], *kv_out)
```

```python
import functools
import math

import numpy as np
import jax
import jax.numpy as jnp
from jax import lax
from jax.experimental import pallas as pl
from jax.experimental.pallas import tpu as pltpu

F32 = jnp.float32
BF16 = jnp.bfloat16

D_MODEL = 1024
EPS = 1e-6
A_HEADS = 8
A_KDIM = 128
A_VDIM = 256
A_QK = A_HEADS * A_KDIM
A_V = A_HEADS * A_VDIM
B_GROUPS = ((128, 1), (512, 4), (2048, 16))
B_HG = 4
B_HD = 128
B_GW = B_HG * B_HD
B_WIDTH = len(B_GROUPS) * B_GW
ROPE_THETA = 10000.0
PAST_LEN = 2048
N_KEYS = 129

COL_TILE = 512
MIX_TILE = 512
VMEM_LIMIT = 56 * 1024 * 1024

_NT = (((1,), (1,)), ((), ()))
_TN = (((0,), (0,)), ((), ()))


def _sigmoid(x):
    return 1.0 / (1.0 + jnp.exp(-x))


def _silu(x):
    h = 0.5 * x
    return h * jnp.tanh(h) + h


def _dot(a, b, dims=None):
    if dims is None:
        return jnp.dot(a, b, preferred_element_type=F32)
    return lax.dot_general(a, b, dims, preferred_element_type=F32)


def _params(sem):
    return pltpu.CompilerParams(dimension_semantics=sem, vmem_limit_bytes=VMEM_LIMIT)


def _ada_kernel(c_ref, w_ref, b_ref, o_ref):
    a = _silu(c_ref[...]).astype(BF16)
    o_ref[0] = _dot(a, w_ref[0].astype(BF16)) + b_ref[0]


def _ada_vectors(c_all, ada_w, ada_b):
    n_layers, _, width = ada_w.shape
    rows = c_all.shape[0]
    return pl.pallas_call(
        _ada_kernel,
        grid=(n_layers, width // COL_TILE),
        in_specs=[
            pl.BlockSpec((rows, D_MODEL), lambda l, j: (0, 0)),
            pl.BlockSpec((1, D_MODEL, COL_TILE), lambda l, j: (l, 0, j)),
            pl.BlockSpec((1, 1, COL_TILE), lambda l, j: (l, 0, j)),
        ],
        out_specs=pl.BlockSpec((1, rows, COL_TILE), lambda l, j: (l, 0, j)),
        out_shape=jax.ShapeDtypeStruct((n_layers, rows, width), F32),
        compiler_params=_params(("parallel", "parallel")),
        name="ada_vectors",
    )(c_all, ada_w, ada_b.reshape(n_layers, 1, width))


def _modulated_norm(x_ref, ada_ref, nw_ref):
    x = x_ref[...]
    ada = ada_ref[...]
    if x.ndim == 2:
        ada = ada.reshape(1, ada.shape[-1])
    shift = ada[..., :D_MODEL]
    scale = ada[..., D_MODEL:2 * D_MODEL]
    ms = jnp.mean(x * x, axis=-1, keepdims=True)
    y = x * lax.rsqrt(ms + EPS) * nw_ref[...].reshape((1,) * (x.ndim - 1) + (D_MODEL,))
    h = y * (1.0 + scale) + shift
    return h.reshape(-1, D_MODEL).astype(BF16)


def _token_specs(batch, seq, tm):
    if seq >= tm:
        assert seq % tm == 0
        per = seq // tm
        n_tiles = batch * per
        x_block = lambda w: pl.BlockSpec((tm, w), lambda i, j: (i, 0))
        ada_block = pl.BlockSpec((1, 1, 3 * D_MODEL), lambda i, j: (i // per, 0, 0))
        col_block = lambda w, f: pl.BlockSpec((tm, w), lambda i, j: (i, f(j)))
        gate_block = lambda w, f: pl.BlockSpec((1, 1, w), lambda i, j: (i // per, 0, f(j)))
        shape = lambda w: (batch * seq, w)
        return n_tiles, x_block, ada_block, col_block, gate_block, shape, per
    tb = tm // seq
    assert batch % tb == 0
    n_tiles = batch // tb
    x_block = lambda w: pl.BlockSpec((tb, seq, w), lambda i, j: (i, 0, 0))
    ada_block = pl.BlockSpec((tb, 1, 3 * D_MODEL), lambda i, j: (i, 0, 0))
    col_block = lambda w, f: pl.BlockSpec((tb, seq, w), lambda i, j: (i, 0, f(j)))
    gate_block = lambda w, f: pl.BlockSpec((tb, 1, w), lambda i, j: (i, 0, f(j)))
    shape = lambda w: (batch, seq, w)
    return n_tiles, x_block, ada_block, col_block, gate_block, shape, None


def _clip(lo, n):
    return lambda j: jnp.clip(j - lo, 0, n - 1)


_A_NQ = A_QK // COL_TILE
_A_NF = A_QK // COL_TILE
_A_NV = A_V // COL_TILE
_A_NZ = A_V // COL_TILE


def _l0_inproj_kernel(x_ref, ada_ref, nw_ref, w_ref, alb_ref, bf_ref,
                      q_ref, k_ref, lf_ref, v_ref, zs_ref, h_scr):
    j = pl.program_id(1)

    @pl.when(j == 0)
    def _():
        h_scr[...] = _modulated_norm(x_ref, ada_ref, nw_ref)

    acc = _dot(h_scr[...], w_ref[...])

    @pl.when(j < _A_NQ)
    def _():
        q_ref[...] = _silu(acc).reshape(q_ref.shape)

    @pl.when((j >= _A_NQ) & (j < _A_NQ + _A_NF))
    def _():
        a = alb_ref[...]
        m = jnp.max(a, axis=0, keepdims=True)
        e = jnp.exp(a - m)
        lb = e[0:1] / jnp.sum(e, axis=0, keepdims=True)
        f = lb + (1.0 - lb) * _sigmoid(acc + bf_ref[...])
        lf_ref[...] = jnp.log(f).reshape(lf_ref.shape)
        k_ref[...] = (1.0 - f).reshape(k_ref.shape)

    @pl.when((j >= _A_NQ + _A_NF) & (j < _A_NQ + _A_NF + _A_NV))
    def _():
        v_ref[...] = acc.reshape(v_ref.shape)

    @pl.when(j >= _A_NQ + _A_NF + _A_NV)
    def _():
        zs_ref[...] = _silu(acc).reshape(zs_ref.shape)


def _l0_inproj(x, ada, nw, w_bf, alb, b_f, batch, seq, tm):
    n_tiles, x_block, ada_block, col_block, _, shape, _ = _token_specs(batch, seq, tm)
    n_col = w_bf.shape[1] // COL_TILE
    f0, v0, z0 = _A_NQ, _A_NQ + _A_NF, _A_NQ + _A_NF + _A_NV
    return pl.pallas_call(
        _l0_inproj_kernel,
        grid=(n_tiles, n_col),
        in_specs=[
            x_block(D_MODEL),
            ada_block,
            pl.BlockSpec((1, D_MODEL), lambda i, j: (0, 0)),
            pl.BlockSpec((D_MODEL, COL_TILE), lambda i, j: (0, j)),
            pl.BlockSpec((alb.shape[0], COL_TILE), lambda i, j: (0, _clip(f0, _A_NF)(j))),
            pl.BlockSpec((1, COL_TILE), lambda i, j: (0, _clip(f0, _A_NF)(j))),
        ],
        out_specs=[
            col_block(COL_TILE, _clip(0, _A_NQ)),
            col_block(COL_TILE, _clip(f0, _A_NF)),
            col_block(COL_TILE, _clip(f0, _A_NF)),
            col_block(COL_TILE, _clip(v0, _A_NV)),
            col_block(COL_TILE, _clip(z0, _A_NZ)),
        ],
        out_shape=[
            jax.ShapeDtypeStruct(shape(A_QK), F32),
            jax.ShapeDtypeStruct(shape(A_QK), F32),
            jax.ShapeDtypeStruct(shape(A_QK), F32),
            jax.ShapeDtypeStruct(shape(A_V), F32),
            jax.ShapeDtypeStruct(shape(A_V), F32),
        ],
        scratch_shapes=[pltpu.VMEM((tm, D_MODEL), BF16)],
        compiler_params=_params(("parallel", "arbitrary")),
        name="l0_inproj",
    )(x, ada, nw, w_bf, alb, b_f)


def _scan_levels(chunk):
    return [1 << l for l in range(int(math.log2(chunk)))]


def _exponent_matrix(chunk):
    r = np.arange(chunk)[:, None]
    c = np.arange(chunk)[None, :]
    blocks = [c <= r, c > r]
    for s in _scan_levels(chunk):
        bound = (r & ~(2 * s - 1)) | s
        upper = (r & s) != 0
        blocks.append(np.where(upper, (c > bound) & (c <= r), (c > r) & (c <= bound)))
    return np.concatenate(blocks, axis=0).astype(np.float32)


def _hgrn_chunk(q, k, lf, v, state, emat, chunk):
    dk = q.shape[-1]
    ex = jnp.exp(jnp.dot(emat, lf, preferred_element_type=F32, precision=lax.Precision.HIGHEST))
    e_b = ex[0:chunk]
    e_u = ex[chunk:2 * chunk]
    row = lax.broadcasted_iota(jnp.int32, (chunk, 1), 0)
    col = lax.broadcasted_iota(jnp.int32, (1, chunk), 1)

    att = jnp.where(row == col, jnp.sum(q * k, axis=-1, keepdims=True), 0.0)
    for l, s in enumerate(_scan_levels(chunk)):
        g = ex[(2 + l) * chunk:(3 + l) * chunk]
        upper = (row & s) != 0
        xb = (jnp.where(upper, q, k) * g).astype(BF16)
        part = _dot(xb, xb, _NT)
        keep = upper & ((col & s) == 0) & ((row >> (l + 1)) == (col >> (l + 1)))
        att = att + jnp.where(keep, part, 0.0)

    o = _dot(att.astype(BF16), v.astype(BF16)) + _dot((q * e_b).astype(BF16), state.astype(BF16))

    e_last = e_b[chunk - 1:chunk]
    eye = lax.broadcasted_iota(jnp.int32, (dk, dk), 0) == lax.broadcasted_iota(jnp.int32, (dk, dk), 1)
    e_col = jnp.sum(jnp.where(eye, e_last, 0.0), axis=-1, keepdims=True)
    new_state = state * e_col + _dot((k * e_u).astype(BF16), v.astype(BF16), _TN)
    return o, new_state


SAFE_DECAY = 60.0
SPLIT_PAD = 32


def _split_prefix_matrix(chunk):
    t = _exponent_matrix(chunk)[:2 * chunk]
    pad = (-3 * chunk) % SPLIT_PAD
    return np.concatenate([t, t, t, np.zeros((2 * chunk, pad), np.float32)], axis=1)


def _hgrn_chunk_bounded(q, k, lf, v, states, tmat, chunk):
    hi = lf.astype(BF16)
    r1 = lf - hi.astype(F32)
    mid = r1.astype(BF16)
    lo = r1 - mid.astype(F32)
    parts = [hi.astype(F32), mid.astype(F32), lo]
    pad = tmat.shape[1] - 3 * chunk
    if pad:
        parts.append(jnp.zeros((pad, lf.shape[1]), F32))
    sums = _dot(tmat, jnp.concatenate(parts, axis=0).astype(BF16))
    b, u = sums[:chunk], sums[chunk:]
    e_b = jnp.exp(b)
    qb = (q * e_b).astype(BF16)
    kn = (k * jnp.exp(-b)).astype(BF16)
    ku = (k * jnp.exp(u)).astype(BF16)
    vb = v.astype(BF16)
    row = lax.broadcasted_iota(jnp.int32, (chunk, chunk), 0)
    col = lax.broadcasted_iota(jnp.int32, (chunk, chunk), 1)
    eye = lax.broadcasted_iota(jnp.int32, (A_KDIM, A_KDIM), 0) == lax.broadcasted_iota(jnp.int32, (A_KDIM, A_KDIM), 1)
    ks = [slice(h * A_KDIM, (h + 1) * A_KDIM) for h in range(A_HEADS)]
    vs = [slice(h * A_VDIM, (h + 1) * A_VDIM) for h in range(A_HEADS)]
    atts = [jnp.where(row >= col, _dot(qb[:, ks[h]], kn[:, ks[h]], _NT), 0.0).astype(BF16)
            for h in range(A_HEADS)]
    outs = [_dot(atts[h], vb[:, vs[h]]) + _dot(qb[:, ks[h]], states[h].astype(BF16)) for h in range(A_HEADS)]
    new_states = []
    for h in range(A_HEADS):
        e_col = jnp.sum(jnp.where(eye, e_b[chunk - 1:chunk, ks[h]], 0.0), axis=-1, keepdims=True)
        new_states.append(states[h] * e_col + _dot(ku[:, ks[h]], vb[:, vs[h]], _TN))
    return outs, new_states


def _hgrn_scan_kernel(*refs, chunk, n_chunks, has_state):
    if has_state:
        q_ref, k_ref, lf_ref, v_ref, zs_ref, gw_ref, e_ref, t_ref, s0_ref, og_ref, so_ref, s_scr = refs
    else:
        q_ref, k_ref, lf_ref, v_ref, zs_ref, gw_ref, e_ref, t_ref, og_ref, so_ref, s_scr = refs
    t = pl.program_id(1)

    @pl.when(t == 0)
    def _():
        if has_state:
            s_scr[...] = s0_ref[0]
        else:
            s_scr[...] = jnp.zeros_like(s_scr)

    gw = gw_ref[...]

    def finish(rows, h, o):
        vc = slice(h * A_VDIM, (h + 1) * A_VDIM)
        ms = jnp.mean(o * o, axis=-1, keepdims=True)
        og_ref[rows, vc] = (o * lax.rsqrt(ms + EPS) * gw * zs_ref[rows, vc]).astype(og_ref.dtype)

    def chunk_rows(c):
        return slice(0, chunk) if n_chunks == 1 else pl.ds(pl.multiple_of(c * chunk, chunk), chunk)

    def bounded_body(c, carry):
        rows = chunk_rows(c)
        outs, new_states = _hgrn_chunk_bounded(q_ref[rows, :], k_ref[rows, :], lf_ref[rows, :], v_ref[rows, :],
                                               [s_scr[h] for h in range(A_HEADS)], t_ref[...], chunk)
        for h in range(A_HEADS):
            s_scr[h] = new_states[h]
            finish(rows, h, outs[h])
        return carry

    def general_body(c, carry):
        rows = chunk_rows(c)
        for h in range(A_HEADS):
            kc = slice(h * A_KDIM, (h + 1) * A_KDIM)
            vc = slice(h * A_VDIM, (h + 1) * A_VDIM)
            o, s_new = _hgrn_chunk(q_ref[rows, kc], k_ref[rows, kc], lf_ref[rows, kc],
                                   v_ref[rows, vc], s_scr[h], e_ref[...], chunk)
            s_scr[h] = s_new
            finish(rows, h, o)
        return carry

    def run(body):
        def go():
            if n_chunks == 1:
                body(0, 0)
            else:
                lax.fori_loop(0, n_chunks, body, 0)
        return go

    bounded = jnp.min(lf_ref[...]) * chunk >= -SAFE_DECAY
    lax.cond(bounded, run(bounded_body), run(general_body))

    @pl.when(t == pl.num_programs(1) - 1)
    def _():
        so_ref[0] = s_scr[...]


def _hgrn_scan(q, k, lf, v, zs, gw, s0, batch, seq, tile, chunk, out_dtype):
    n_t = seq // tile
    emat = jnp.asarray(_exponent_matrix(chunk))
    tmat = jnp.asarray(_split_prefix_matrix(chunk), dtype=BF16)
    row_block = lambda w: pl.BlockSpec((tile, w), lambda b, t: (b * n_t + t, 0))
    state_block = pl.BlockSpec((1, A_HEADS, A_KDIM, A_VDIM), lambda b, t: (b, 0, 0, 0))
    in_specs = [row_block(A_QK), row_block(A_QK), row_block(A_QK), row_block(A_V), row_block(A_V),
                pl.BlockSpec((1, A_VDIM), lambda b, t: (0, 0)),
                pl.BlockSpec(emat.shape, lambda b, t: (0, 0)),
                pl.BlockSpec(tmat.shape, lambda b, t: (0, 0))]
    args = [q, k, lf, v, zs, gw, emat, tmat]
    if s0 is not None:
        in_specs.append(state_block)
        args.append(s0)
    kern = functools.partial(_hgrn_scan_kernel, chunk=chunk, n_chunks=tile // chunk,
                             has_state=s0 is not None)
    return pl.pallas_call(
        kern,
        grid=(batch, n_t),
        in_specs=in_specs,
        out_specs=[row_block(A_V), state_block],
        out_shape=[jax.ShapeDtypeStruct((batch * seq, A_V), out_dtype),
                   jax.ShapeDtypeStruct((batch, A_HEADS, A_KDIM, A_VDIM), F32)],
        scratch_shapes=[pltpu.VMEM((A_HEADS, A_KDIM, A_VDIM), F32)],
        compiler_params=_params(("parallel", "arbitrary")),
        name="hgrn_scan",
    )(*args)


def _outproj_kernel(g_ref, w_ref, x_ref, gate_ref, o_ref):
    g = g_ref[...]
    y = _dot(g.reshape(-1, g.shape[-1]).astype(BF16), w_ref[...])
    x = x_ref[...]
    gate = gate_ref[...]
    if x.ndim == 2:
        gate = gate.reshape(1, gate.shape[-1])
    o_ref[...] = x + gate * y.reshape(x.shape)


def _outproj(g, w_bf, x, ada, batch, seq, tm):
    n_tiles, x_block, _, col_block, gate_block, shape, _ = _token_specs(batch, seq, tm)
    kdim = w_bf.shape[0]
    n_col = D_MODEL // COL_TILE
    gate0 = 2 * D_MODEL // COL_TILE
    return pl.pallas_call(
        _outproj_kernel,
        grid=(n_tiles, n_col),
        in_specs=[
            x_block(kdim),
            pl.BlockSpec((kdim, COL_TILE), lambda i, j: (0, j)),
            col_block(COL_TILE, lambda j: j),
            gate_block(COL_TILE, lambda j: gate0 + j),
        ],
        out_specs=col_block(COL_TILE, lambda j: j),
        out_shape=jax.ShapeDtypeStruct(shape(D_MODEL), F32),
        compiler_params=_params(("parallel", "arbitrary")),
        name="outproj",
    )(g, w_bf, x, ada)


def _rope_table_kernel(pos_ref, inv_ref, cos_ref, sin_ref):
    ang = pos_ref[...] * inv_ref[...]
    lane = lax.broadcasted_iota(jnp.int32, ang.shape, 1)
    cos_ref[...] = jnp.cos(ang)
    sin_ref[...] = jnp.where(lane < B_HD // 2, -jnp.sin(ang), jnp.sin(ang))


def _rope_tables(start, seq):
    half = B_HD // 2
    inv = ROPE_THETA ** (-jnp.arange(half, dtype=F32) / half)
    inv2 = jnp.concatenate([inv, inv]).reshape(1, B_HD)
    pos = (start + jnp.arange(seq, dtype=jnp.int32)).astype(F32).reshape(seq, 1)
    return pl.pallas_call(
        _rope_table_kernel,
        out_shape=[jax.ShapeDtypeStruct((seq, B_HD), F32)] * 2,
        name="rope_tables",
    )(pos, inv2)


_B_NG = B_WIDTH // COL_TILE


def _l1_inproj_kernel(x_ref, ada_ref, nw_ref, w_ref, qn_ref, kn_ref, cos_ref, sin_ref,
                      q_ref, kv_ref, zs_ref, h_scr):
    j = pl.program_id(1)

    @pl.when(j == 0)
    def _():
        h_scr[...] = _modulated_norm(x_ref, ada_ref, nw_ref)

    acc = _dot(h_scr[...], w_ref[...])
    tm = acc.shape[0]

    def norm_rope(w_ref_):
        cos = cos_ref[...]
        sin = sin_ref[...]
        reps = tm // cos.shape[0]
        if reps > 1:
            cos = jnp.broadcast_to(cos[None], (reps,) + cos.shape).reshape(tm, B_HD)
            sin = jnp.broadcast_to(sin[None], (reps,) + sin.shape).reshape(tm, B_HD)
        outs = []
        for h in range(COL_TILE // B_HD):
            xh = acc[:, h * B_HD:(h + 1) * B_HD]
            ms = jnp.mean(xh * xh, axis=-1, keepdims=True)
            y = xh * lax.rsqrt(ms + EPS) * w_ref_[...]
            outs.append(y * cos + pltpu.roll(y, B_HD // 2, 1) * sin)
        return jnp.concatenate(outs, axis=-1)

    @pl.when(j < _B_NG)
    def _():
        q_ref[...] = norm_rope(qn_ref).reshape(q_ref.shape)

    @pl.when((j >= _B_NG) & (j < 2 * _B_NG))
    def _():
        kv_ref[...] = norm_rope(kn_ref).reshape(kv_ref.shape)

    @pl.when((j >= 2 * _B_NG) & (j < 3 * _B_NG))
    def _():
        kv_ref[...] = acc.reshape(kv_ref.shape)

    @pl.when(j >= 3 * _B_NG)
    def _():
        zs_ref[...] = _silu(acc).reshape(zs_ref.shape)


def _kv_col(j):
    g_k = jnp.clip(j - _B_NG, 0, _B_NG - 1)
    g_v = jnp.clip(j - 2 * _B_NG, 0, _B_NG - 1)
    return jnp.where(j < 2 * _B_NG, 2 * g_k, 2 * g_v + 1)


def _l1_inproj(x, ada, nw, w_bf, qn, kn, cos, sin, batch, seq, tm):
    n_tiles, x_block, ada_block, col_block, _, shape, per = _token_specs(batch, seq, tm)
    n_col = w_bf.shape[1] // COL_TILE
    if per is None:
        tab_block = pl.BlockSpec((seq, B_HD), lambda i, j: (0, 0))
    else:
        tab_block = pl.BlockSpec((tm, B_HD), lambda i, j: (i % per, 0))
    vec_block = pl.BlockSpec((1, B_HD), lambda i, j: (0, 0))
    return pl.pallas_call(
        _l1_inproj_kernel,
        grid=(n_tiles, n_col),
        in_specs=[
            x_block(D_MODEL),
            ada_block,
            pl.BlockSpec((1, D_MODEL), lambda i, j: (0, 0)),
            pl.BlockSpec((D_MODEL, COL_TILE), lambda i, j: (0, j)),
            vec_block, vec_block, tab_block, tab_block,
        ],
        out_specs=[
            col_block(COL_TILE, _clip(0, _B_NG)),
            col_block(COL_TILE, _kv_col),
            col_block(COL_TILE, _clip(3 * _B_NG, _B_NG)),
        ],
        out_shape=[
            jax.ShapeDtypeStruct(shape(B_WIDTH), F32),
            jax.ShapeDtypeStruct(shape(2 * B_WIDTH), F32),
            jax.ShapeDtypeStruct(shape(B_WIDTH), F32),
        ],
        scratch_shapes=[pltpu.VMEM((tm, D_MODEL), BF16)],
        compiler_params=_params(("parallel", "arbitrary")),
        name="l1_inproj",
    )(x, ada, nw, w_bf, qn, kn, cos, sin)


Q_SUB = 128


def _softmax_parts(s, valid):
    s = jnp.where(valid, s, -jnp.inf)
    m = jnp.max(s, axis=-1, keepdims=True)
    p = jnp.exp(s - m)
    return m, p, jnp.sum(p, axis=-1, keepdims=True)


def _prompt_attn_kernel(q_ref, kc_ref, vc_ref, kp_ref, vp_ref, o_ref, lse_ref, *, tq, dil):
    first_key = jnp.where(pl.program_id(2) == 0, Q_SUB, 0)
    row = lax.broadcasted_iota(jnp.int32, (Q_SUB, 2 * Q_SUB), 0)
    col = lax.broadcasted_iota(jnp.int32, (Q_SUB, 2 * Q_SUB), 1)
    band = (col >= row) & (col <= row + Q_SUB)
    scale = B_HD ** -0.5

    def rows_of(r, first, n):
        start = r + first * dil
        return pl.ds(start, n) if dil == 1 else pl.ds(start, n, stride=dil)

    for r in range(dil):
        for sb in range(tq // Q_SUB):
            rows = rows_of(r, sb * Q_SUB, Q_SUB)
            if sb == 0:
                prev = rows_of(r, 0, Q_SUB)
                k_prev, v_prev = kp_ref[0, prev, :], vp_ref[0, prev, :]
                valid = band & (col >= first_key)
            else:
                prev = rows_of(r, (sb - 1) * Q_SUB, Q_SUB)
                k_prev, v_prev = kc_ref[0, prev, :], vc_ref[0, prev, :]
                valid = band
            k2 = jnp.concatenate([k_prev, kc_ref[0, rows, :]], axis=0).astype(BF16)
            v2 = jnp.concatenate([v_prev, vc_ref[0, rows, :]], axis=0).astype(BF16)
            s = _dot(q_ref[0, rows, :].astype(BF16), k2, _NT) * scale
            m, p, l = _softmax_parts(s, valid)
            o_ref[0, rows, :] = _dot(p.astype(BF16), v2) / l
            lse_ref[0, rows, :] = jnp.broadcast_to(m + jnp.log(l), (Q_SUB, B_HD))


def _prompt_attn_group(q, kv, g, dil, batch, seq):
    tq = min(512, seq // dil)
    tile = tq * dil
    back = Q_SUB * dil
    n_tiles = seq // tile
    q0, k0, v0 = g * B_HG, 2 * g * B_HG, (2 * g + 1) * B_HG
    cur = lambda c0: pl.BlockSpec((1, tile, B_HD), lambda b, h, i: (b, i, c0 + h))
    prev = lambda c0: pl.BlockSpec(
        (1, back, B_HD), lambda b, h, i: (b, jnp.maximum(i * (tile // back) - 1, 0), c0 + h))
    out = pl.BlockSpec((1, tile, B_HD), lambda b, h, i: (b, i, h))
    return pl.pallas_call(
        functools.partial(_prompt_attn_kernel, tq=tq, dil=dil),
        grid=(batch, B_HG, n_tiles),
        in_specs=[cur(q0), cur(k0), cur(v0), prev(k0), prev(v0)],
        out_specs=[out, out],
        out_shape=[jax.ShapeDtypeStruct((batch, seq, B_GW), F32)] * 2,
        compiler_params=_params(("parallel", "parallel", "arbitrary")),
        name=f"prompt_attn_g{g}",
    )(q, kv, kv, kv, kv)


def _group_mix_kernel(o0, o1, o2, l0, l1, l2, zs_ref, out_ref):
    lses = [l0[...], l1[...], l2[...]]
    m = jnp.maximum(jnp.maximum(lses[0], lses[1]), lses[2])
    es = [jnp.exp(l - m) for l in lses]
    inv = 1.0 / (es[0] + es[1] + es[2])
    for g, o in enumerate((o0, o1, o2)):
        cols = slice(g * B_GW, (g + 1) * B_GW)
        out_ref[:, cols] = (o[...] * (es[g] * inv) * zs_ref[:, cols]).astype(out_ref.dtype)


def _group_mix(outs, lses, zs, tm):
    n = zs.shape[0]
    blk = pl.BlockSpec((tm, B_GW), lambda i: (i, 0))
    wide = pl.BlockSpec((tm, B_WIDTH), lambda i: (i, 0))
    return pl.pallas_call(
        _group_mix_kernel,
        grid=(n // tm,),
        in_specs=[blk] * 6 + [wide],
        out_specs=wide,
        out_shape=jax.ShapeDtypeStruct((n, B_WIDTH), BF16),
        compiler_params=_params(("parallel",)),
        name="group_mix",
    )(*outs, *lses, zs)


KV_ROWS = 2 * B_HG


def _sample_masks(seq):
    hq = np.arange(B_HG * seq)[:, None] // seq
    iq = np.arange(B_HG * seq)[:, None] % seq
    masks = []
    for window, dil in B_GROUPS:
        n_tok = window if dil < seq else (window // dil) * seq
        col = np.arange(n_tok * KV_ROWS)[None, :]
        tok, is_v, head = col // KV_ROWS, (col // B_HG) % 2, col % B_HG
        if dil < seq:
            ok = (tok >= iq) & ((tok - iq) % dil == 0)
        else:
            ok = (tok % seq) == iq
        masks.append(np.where(ok & (is_v == 0) & (head == hq), 0.0, -np.inf).astype(np.float32))
    col = np.arange(B_HG * seq)[None, :]
    new = []
    for _, dil in B_GROUPS:
        ok = (col // seq == hq) & (col % seq <= iq) & ((iq - col % seq) % dil == 0)
        new.append(np.where(ok, 0.0, -np.inf).astype(np.float32))
    return masks, np.stack(new)


def _sample_attn_kernel(q_ref, kv_ref, zs_ref, c0_ref, c1_ref, c2_ref, m0_ref, m1_ref, m2_ref, mn_ref,
                        out_ref, *, seq):
    scale = B_HD ** -0.5
    caches = (c0_ref, c1_ref, c2_ref)
    masks = (m0_ref, m1_ref, m2_ref)
    heads = lambda ref, c0: jnp.concatenate(
        [ref[0, :, c0 + h * B_HD:c0 + (h + 1) * B_HD] for h in range(B_HG)], axis=0).astype(BF16)
    outs, lses = [], []
    for g in range(len(B_GROUPS)):
        qa = heads(q_ref, g * B_GW)
        k_new = heads(kv_ref, 2 * g * B_GW)
        v_new = heads(kv_ref, (2 * g + 1) * B_GW)
        rows = caches[g][0].reshape(-1, B_HD).astype(BF16)
        s_c = _dot(qa, rows, _NT) * scale + masks[g][...]
        s_n = _dot(qa, k_new, _NT) * scale + mn_ref[g]
        m = jnp.maximum(jnp.max(s_c, axis=-1, keepdims=True), jnp.max(s_n, axis=-1, keepdims=True))
        p_c = jnp.exp(s_c - m)
        p_n = jnp.exp(s_n - m)
        l = jnp.sum(p_c, axis=-1, keepdims=True) + jnp.sum(p_n, axis=-1, keepdims=True)
        p_v = pltpu.roll(p_c, B_HG, 1)
        acc = _dot(p_v.astype(BF16), rows) + _dot(p_n.astype(BF16), v_new)
        outs.append(acc / l)
        lses.append(m + jnp.log(l))
    m = functools.reduce(jnp.maximum, lses)
    es = [jnp.exp(l - m) for l in lses]
    inv = 1.0 / (es[0] + es[1] + es[2])
    for g in range(len(B_GROUPS)):
        mixed = outs[g] * (es[g] * inv)
        for h in range(B_HG):
            cols = slice(g * B_GW + h * B_HD, g * B_GW + (h + 1) * B_HD)
            out_ref[0, :, cols] = mixed[h * seq:(h + 1) * seq] * zs_ref[0, :, cols]


def _sample_attn(q, kv, zs, caches, batch, seq):
    masks, mask_new = _sample_masks(seq)
    specs, views = [], []
    for (window, dil), c in zip(B_GROUPS, caches):
        assert window == dil * (N_KEYS - 1) and c.shape[2] == window
        if dil >= seq:
            views.append(c.reshape(batch, window // dil, dil * KV_ROWS, B_HD))
            specs.append(pl.BlockSpec((1, window // dil, seq * KV_ROWS, B_HD), lambda b: (b, 0, 0, 0)))
        else:
            assert seq % dil == 0
            views.append(c.reshape(batch, window * KV_ROWS, B_HD))
            specs.append(pl.BlockSpec((1, window * KV_ROWS, B_HD), lambda b: (b, 0, 0)))
    const = lambda a: pl.BlockSpec(a.shape, lambda b: (0,) * a.ndim)
    tok = lambda w: pl.BlockSpec((1, seq, w), lambda b: (b, 0, 0))
    return pl.pallas_call(
        functools.partial(_sample_attn_kernel, seq=seq),
        grid=(batch,),
        in_specs=[tok(B_WIDTH), tok(2 * B_WIDTH), tok(B_WIDTH)] + specs
                 + [const(a) for a in masks] + [const(mask_new)],
        out_specs=tok(B_WIDTH),
        out_shape=jax.ShapeDtypeStruct((batch, seq, B_WIDTH), F32),
        compiler_params=_params(("parallel",)),
        name="sample_attn",
    )(q, kv, zs, *views, *[jnp.asarray(a) for a in masks], jnp.asarray(mask_new))


def _trunk(x, ada, start, state, caches, weights, tm, scan_tile, scan_chunk):
    (norm_w, alb, a_w_in, a_b_f, a_gw, a_w_out, b_w_in, b_qn, b_kn, b_w_out) = weights
    batch, seq, _ = x.shape
    flat = seq >= tm
    tok = (lambda a: a.reshape(batch * seq, a.shape[-1])) if flat else (lambda a: a)
    rows = lambda a: a.reshape(batch * seq, a.shape[-1])
    ada0 = ada[0].reshape(batch, 1, 3 * D_MODEL)
    ada1 = ada[1].reshape(batch, 1, 3 * D_MODEL)

    x0 = tok(x)
    q, k, lf, v, zs = _l0_inproj(x0, ada0, norm_w[0:1], a_w_in, alb, a_b_f, batch, seq, tm)
    og, s_new = _hgrn_scan(rows(q), rows(k), rows(lf), rows(v), rows(zs), a_gw, state,
                           batch, seq, scan_tile, scan_chunk, BF16 if flat else F32)
    og = og if flat else og.reshape(batch, seq, A_V)
    x1 = _outproj(og, a_w_out, x0, ada0, batch, seq, tm)

    cos, sin = _rope_tables(start, seq)
    qr, kv, zs1 = _l1_inproj(x1, ada1, norm_w[1:2], b_w_in, b_qn, b_kn, cos, sin, batch, seq, tm)
    kv3 = kv.reshape(batch, seq, 2 * B_WIDTH)
    if caches is None:
        outs, lses = [], []
        for g, (_, dil) in enumerate(B_GROUPS):
            o, lse = _prompt_attn_group(qr.reshape(batch, seq, B_WIDTH), kv3, g, dil, batch, seq)
            outs.append(o.reshape(batch * seq, B_GW))
            lses.append(lse.reshape(batch * seq, B_GW))
        og1 = _group_mix(outs, lses, zs1, MIX_TILE)
    else:
        og1 = _sample_attn(qr, kv, zs1, caches, batch, seq)
    x2 = _outproj(og1, b_w_out, x1, ada1, batch, seq, tm)

    kv_rows = [kv3[:, seq - min(window, seq):, 2 * g * B_GW:2 * (g + 1) * B_GW]
               .reshape(batch, min(window, seq), 2, B_HG, B_HD) for g, (window, _) in enumerate(B_GROUPS)]
    return x2.reshape(batch, seq, D_MODEL), s_new, kv_rows


def kernel(x_prompt, x_sample, state_hgrn, cache_kv_w128, cache_kv_w512, cache_kv_w2048, c_prompt, c_sample,
           norm_w, ada_w, ada_b, a_lower_bounds, a_w_in, a_b_f, a_g_norm_w, a_w_out, b_w_in, b_q_norm_w,
           b_k_norm_w, b_w_out):
    bp, lp, _ = x_prompt.shape
    bs, ls, _ = x_sample.shape

    pad = (-(bp + bs)) % 8
    c_all = jnp.concatenate([c_prompt, c_sample, jnp.zeros((pad, D_MODEL), F32)], axis=0)
    ada = _ada_vectors(c_all, ada_w, ada_b)
    ada_p, ada_s = ada[:, :bp], ada[:, bp:bp + bs]

    weights = (norm_w, a_lower_bounds, a_w_in[0].astype(BF16), a_b_f, a_g_norm_w, a_w_out[0].astype(BF16),
               b_w_in[0].astype(BF16), b_q_norm_w, b_k_norm_w, b_w_out[0].astype(BF16))

    y_p, s_p, kv_p = _trunk(x_prompt, ada_p, 0, None, None, weights,
                            tm=1024, scan_tile=256, scan_chunk=64)
    caches = (cache_kv_w128, cache_kv_w512, cache_kv_w2048)
    y_s, s_s, kv_s = _trunk(x_sample, ada_s, PAST_LEN, state_hgrn[0], caches, weights,
                            tm=bs * ls, scan_tile=ls, scan_chunk=ls)

    kv_out = []
    for g in range(len(B_GROUPS)):
        kv_out.append(kv_p[g][None])
        kv_out.append(kv_s[g][None])
    return (y_p, y_s, s_p[None], s_s[None], *kv_out)
```

```python
import functools
import math

import numpy as np
import jax
import jax.numpy as jnp
from jax import lax
from jax.experimental import pallas as pl
from jax.experimental.pallas import tpu as pltpu

F32 = jnp.float32
BF16 = jnp.bfloat16

D_MODEL = 1024
EPS = 1e-6
A_HEADS = 8
A_KDIM = 128
A_VDIM = 256
A_QK = A_HEADS * A_KDIM
A_V = A_HEADS * A_VDIM
B_GROUPS = ((128, 1), (512, 4), (2048, 16))
B_HG = 4
B_HD = 128
B_GW = B_HG * B_HD
B_WIDTH = len(B_GROUPS) * B_GW
ROPE_THETA = 10000.0
PAST_LEN = 2048
N_KEYS = 129

COL_TILE = 512
MIX_TILE = 512
VMEM_LIMIT = 56 * 1024 * 1024

_NT = (((1,), (1,)), ((), ()))
_TN = (((0,), (0,)), ((), ()))


def _sigmoid(x):
    return 1.0 / (1.0 + jnp.exp(-x))


def _silu(x):
    h = 0.5 * x
    return h * jnp.tanh(h) + h


def _dot(a, b, dims=None):
    if dims is None:
        return jnp.dot(a, b, preferred_element_type=F32)
    return lax.dot_general(a, b, dims, preferred_element_type=F32)


def _params(sem):
    return pltpu.CompilerParams(dimension_semantics=sem, vmem_limit_bytes=VMEM_LIMIT)


def _ada_kernel(c_ref, w_ref, b_ref, o_ref):
    a = _silu(c_ref[...]).astype(BF16)
    o_ref[0] = _dot(a, w_ref[0].astype(BF16)) + b_ref[0]


def _ada_vectors(c_all, ada_w, ada_b):
    n_layers, _, width = ada_w.shape
    rows = c_all.shape[0]
    return pl.pallas_call(
        _ada_kernel,
        grid=(n_layers, width // COL_TILE),
        in_specs=[
            pl.BlockSpec((rows, D_MODEL), lambda l, j: (0, 0)),
            pl.BlockSpec((1, D_MODEL, COL_TILE), lambda l, j: (l, 0, j)),
            pl.BlockSpec((1, 1, COL_TILE), lambda l, j: (l, 0, j)),
        ],
        out_specs=pl.BlockSpec((1, rows, COL_TILE), lambda l, j: (l, 0, j)),
        out_shape=jax.ShapeDtypeStruct((n_layers, rows, width), F32),
        compiler_params=_params(("parallel", "parallel")),
        name="ada_vectors",
    )(c_all, ada_w, ada_b.reshape(n_layers, 1, width))


def _modulated_norm(x_ref, ada_ref, nw_ref):
    x = x_ref[...]
    ada = ada_ref[...]
    if x.ndim == 2:
        ada = ada.reshape(1, ada.shape[-1])
    shift = ada[..., :D_MODEL]
    scale = ada[..., D_MODEL:2 * D_MODEL]
    ms = jnp.mean(x * x, axis=-1, keepdims=True)
    y = x * lax.rsqrt(ms + EPS) * nw_ref[...].reshape((1,) * (x.ndim - 1) + (D_MODEL,))
    h = y * (1.0 + scale) + shift
    return h.reshape(-1, D_MODEL).astype(BF16)


def _token_specs(batch, seq, tm):
    if seq >= tm:
        assert seq % tm == 0
        per = seq // tm
        n_tiles = batch * per
        tok_block = lambda w: pl.BlockSpec((tm, w), lambda i: (i, 0))
        ada_block = pl.BlockSpec((1, 1, 3 * D_MODEL), lambda i: (i // per, 0, 0))
        shape = lambda w: (batch * seq, w)
        return n_tiles, tok_block, ada_block, shape, per
    tb = tm // seq
    assert batch % tb == 0
    n_tiles = batch // tb
    tok_block = lambda w: pl.BlockSpec((tb, seq, w), lambda i: (i, 0, 0))
    ada_block = pl.BlockSpec((tb, 1, 3 * D_MODEL), lambda i: (i, 0, 0))
    shape = lambda w: (batch, seq, w)
    return n_tiles, tok_block, ada_block, shape, None


def _resident(shape):
    return pl.BlockSpec(shape, lambda i: (0,) * len(shape), pipeline_mode=pl.Buffered(1))


def _put(ref, c0, val):
    ref[(Ellipsis, slice(c0, c0 + val.shape[-1]))] = val.reshape(ref.shape[:-1] + val.shape[-1:]).astype(ref.dtype)


def _l0_inproj_kernel(x_ref, ada_ref, nw_ref, w_ref, alb_ref, bf_ref, q_ref, k_ref, lf_ref, v_ref, zs_ref):
    h = _modulated_norm(x_ref, ada_ref, nw_ref)
    a = alb_ref[...]
    m = jnp.max(a, axis=0, keepdims=True)
    e = jnp.exp(a - m)
    lb = e[0:1] / jnp.sum(e, axis=0, keepdims=True)
    for c0 in range(0, w_ref.shape[1], COL_TILE):
        acc = _dot(h, w_ref[:, c0:c0 + COL_TILE])
        if c0 < A_QK:
            _put(q_ref, c0, _silu(acc))
        elif c0 < 2 * A_QK:
            c = c0 - A_QK
            f = lb[:, c:c + COL_TILE] + (1.0 - lb[:, c:c + COL_TILE]) * _sigmoid(acc + bf_ref[:, c:c + COL_TILE])
            _put(lf_ref, c, jnp.log(f))
            _put(k_ref, c, 1.0 - f)
        elif c0 < 2 * A_QK + A_V:
            _put(v_ref, c0 - 2 * A_QK, acc)
        else:
            _put(zs_ref, c0 - 2 * A_QK - A_V, _silu(acc))


def _l0_inproj(x, ada, nw, w_bf, alb, b_f, batch, seq, tm, act_dtype):
    n_tiles, tok_block, ada_block, shape, _ = _token_specs(batch, seq, tm)
    return pl.pallas_call(
        _l0_inproj_kernel,
        grid=(n_tiles,),
        in_specs=[tok_block(D_MODEL), ada_block, _resident((1, D_MODEL)), _resident(w_bf.shape),
                  _resident(alb.shape), _resident(b_f.shape)],
        out_specs=[tok_block(A_QK), tok_block(A_QK), tok_block(A_QK), tok_block(A_V), tok_block(A_V)],
        out_shape=[
            jax.ShapeDtypeStruct(shape(A_QK), F32),
            jax.ShapeDtypeStruct(shape(A_QK), F32),
            jax.ShapeDtypeStruct(shape(A_QK), F32),
            jax.ShapeDtypeStruct(shape(A_V), act_dtype),
            jax.ShapeDtypeStruct(shape(A_V), act_dtype),
        ],
        compiler_params=_params(("parallel",)),
        name="l0_inproj",
    )(x, ada, nw, w_bf, alb, b_f)


def _scan_levels(chunk):
    return [1 << l for l in range(int(math.log2(chunk)))]


def _exponent_matrix(chunk):
    r = np.arange(chunk)[:, None]
    c = np.arange(chunk)[None, :]
    blocks = [c <= r, c > r]
    for s in _scan_levels(chunk):
        bound = (r & ~(2 * s - 1)) | s
        upper = (r & s) != 0
        blocks.append(np.where(upper, (c > bound) & (c <= r), (c > r) & (c <= bound)))
    return np.concatenate(blocks, axis=0).astype(np.float32)


def _hgrn_chunk(q, k, lf, v, state, emat, chunk):
    dk = q.shape[-1]
    ex = jnp.exp(jnp.dot(emat, lf, preferred_element_type=F32, precision=lax.Precision.HIGHEST))
    e_b = ex[0:chunk]
    e_u = ex[chunk:2 * chunk]
    row = lax.broadcasted_iota(jnp.int32, (chunk, 1), 0)
    col = lax.broadcasted_iota(jnp.int32, (1, chunk), 1)

    att = jnp.where(row == col, jnp.sum(q * k, axis=-1, keepdims=True), 0.0)
    for l, s in enumerate(_scan_levels(chunk)):
        g = ex[(2 + l) * chunk:(3 + l) * chunk]
        upper = (row & s) != 0
        xb = (jnp.where(upper, q, k) * g).astype(BF16)
        part = _dot(xb, xb, _NT)
        keep = upper & ((col & s) == 0) & ((row >> (l + 1)) == (col >> (l + 1)))
        att = att + jnp.where(keep, part, 0.0)

    o = _dot(att.astype(BF16), v.astype(BF16)) + _dot((q * e_b).astype(BF16), state.astype(BF16))

    e_last = e_b[chunk - 1:chunk]
    eye = lax.broadcasted_iota(jnp.int32, (dk, dk), 0) == lax.broadcasted_iota(jnp.int32, (dk, dk), 1)
    e_col = jnp.sum(jnp.where(eye, e_last, 0.0), axis=-1, keepdims=True)
    new_state = state * e_col + _dot((k * e_u).astype(BF16), v.astype(BF16), _TN)
    return o, new_state


SAFE_DECAY = 60.0
SPLIT_PAD = 32


def _split_prefix_matrix(chunk):
    t = _exponent_matrix(chunk)[:2 * chunk]
    pad = (-3 * chunk) % SPLIT_PAD
    return np.concatenate([t, t, t, np.zeros((2 * chunk, pad), np.float32)], axis=1)


def _hgrn_chunk_bounded(q, k, lf, v, states, tmat, chunk):
    hi = lf.astype(BF16)
    r1 = lf - hi.astype(F32)
    mid = r1.astype(BF16)
    lo = r1 - mid.astype(F32)
    parts = [hi.astype(F32), mid.astype(F32), lo]
    pad = tmat.shape[1] - 3 * chunk
    if pad:
        parts.append(jnp.zeros((pad, lf.shape[1]), F32))
    sums = _dot(tmat, jnp.concatenate(parts, axis=0).astype(BF16))
    b, u = sums[:chunk], sums[chunk:]
    e_b = jnp.exp(b)
    qb = (q * e_b).astype(BF16)
    kn = (k * jnp.exp(-b)).astype(BF16)
    ku = (k * jnp.exp(u)).astype(BF16)
    vb = v.astype(BF16)
    row = lax.broadcasted_iota(jnp.int32, (chunk, chunk), 0)
    col = lax.broadcasted_iota(jnp.int32, (chunk, chunk), 1)
    eye = lax.broadcasted_iota(jnp.int32, (A_KDIM, A_KDIM), 0) == lax.broadcasted_iota(jnp.int32, (A_KDIM, A_KDIM), 1)
    ks = [slice(h * A_KDIM, (h + 1) * A_KDIM) for h in range(A_HEADS)]
    vs = [slice(h * A_VDIM, (h + 1) * A_VDIM) for h in range(A_HEADS)]
    atts = [jnp.where(row >= col, _dot(qb[:, ks[h]], kn[:, ks[h]], _NT), 0.0).astype(BF16)
            for h in range(A_HEADS)]
    outs = [_dot(atts[h], vb[:, vs[h]]) + _dot(qb[:, ks[h]], states[h].astype(BF16)) for h in range(A_HEADS)]
    new_states = []
    for h in range(A_HEADS):
        e_col = jnp.sum(jnp.where(eye, e_b[chunk - 1:chunk, ks[h]], 0.0), axis=-1, keepdims=True)
        new_states.append(states[h] * e_col + _dot(ku[:, ks[h]], vb[:, vs[h]], _TN))
    return outs, new_states


def _hgrn_scan_kernel(*refs, chunk, n_chunks, has_state):
    if has_state:
        q_ref, k_ref, lf_ref, v_ref, zs_ref, gw_ref, e_ref, t_ref, s0_ref, og_ref, so_ref, s_scr = refs
    else:
        q_ref, k_ref, lf_ref, v_ref, zs_ref, gw_ref, e_ref, t_ref, og_ref, so_ref, s_scr = refs
    t = pl.program_id(1)

    @pl.when(t == 0)
    def _():
        if has_state:
            s_scr[...] = s0_ref[0]
        else:
            s_scr[...] = jnp.zeros_like(s_scr)

    gw = gw_ref[...]

    def finish(rows, h, o):
        vc = slice(h * A_VDIM, (h + 1) * A_VDIM)
        ms = jnp.mean(o * o, axis=-1, keepdims=True)
        og_ref[rows, vc] = (o * lax.rsqrt(ms + EPS) * gw * zs_ref[rows, vc]).astype(og_ref.dtype)

    def chunk_rows(c):
        return slice(0, chunk) if n_chunks == 1 else pl.ds(pl.multiple_of(c * chunk, chunk), chunk)

    def bounded_body(c, carry):
        rows = chunk_rows(c)
        outs, new_states = _hgrn_chunk_bounded(q_ref[rows, :], k_ref[rows, :], lf_ref[rows, :], v_ref[rows, :],
                                               [s_scr[h] for h in range(A_HEADS)], t_ref[...], chunk)
        for h in range(A_HEADS):
            s_scr[h] = new_states[h]
            finish(rows, h, outs[h])
        return carry

    def general_body(c, carry):
        rows = chunk_rows(c)
        for h in range(A_HEADS):
            kc = slice(h * A_KDIM, (h + 1) * A_KDIM)
            vc = slice(h * A_VDIM, (h + 1) * A_VDIM)
            o, s_new = _hgrn_chunk(q_ref[rows, kc], k_ref[rows, kc], lf_ref[rows, kc],
                                   v_ref[rows, vc], s_scr[h], e_ref[...], chunk)
            s_scr[h] = s_new
            finish(rows, h, o)
        return carry

    def run(body):
        def go():
            if n_chunks == 1:
                body(0, 0)
            else:
                lax.fori_loop(0, n_chunks, body, 0)
        return go

    bounded = jnp.min(lf_ref[...]) * chunk >= -SAFE_DECAY
    lax.cond(bounded, run(bounded_body), run(general_body))

    @pl.when(t == pl.num_programs(1) - 1)
    def _():
        so_ref[0] = s_scr[...]


def _hgrn_scan(q, k, lf, v, zs, gw, s0, batch, seq, tile, chunk, out_dtype):
    n_t = seq // tile
    emat = jnp.asarray(_exponent_matrix(chunk))
    tmat = jnp.asarray(_split_prefix_matrix(chunk), dtype=BF16)
    row_block = lambda w: pl.BlockSpec((tile, w), lambda b, t: (b * n_t + t, 0))
    state_block = pl.BlockSpec((1, A_HEADS, A_KDIM, A_VDIM), lambda b, t: (b, 0, 0, 0))
    in_specs = [row_block(A_QK), row_block(A_QK), row_block(A_QK), row_block(A_V), row_block(A_V),
                pl.BlockSpec((1, A_VDIM), lambda b, t: (0, 0)),
                pl.BlockSpec(emat.shape, lambda b, t: (0, 0)),
                pl.BlockSpec(tmat.shape, lambda b, t: (0, 0))]
    args = [q, k, lf, v, zs, gw, emat, tmat]
    if s0 is not None:
        in_specs.append(state_block)
        args.append(s0)
    kern = functools.partial(_hgrn_scan_kernel, chunk=chunk, n_chunks=tile // chunk,
                             has_state=s0 is not None)
    return pl.pallas_call(
        kern,
        grid=(batch, n_t),
        in_specs=in_specs,
        out_specs=[row_block(A_V), state_block],
        out_shape=[jax.ShapeDtypeStruct((batch * seq, A_V), out_dtype),
                   jax.ShapeDtypeStruct((batch, A_HEADS, A_KDIM, A_VDIM), F32)],
        scratch_shapes=[pltpu.VMEM((A_HEADS, A_KDIM, A_VDIM), F32)],
        compiler_params=_params(("parallel", "arbitrary")),
        name="hgrn_scan",
    )(*args)


def _outproj_kernel(g_ref, w_ref, x_ref, ada_ref, o_ref):
    g = g_ref[...]
    g = g.reshape(-1, g.shape[-1]).astype(BF16)
    ada = ada_ref[...]
    if len(x_ref.shape) == 2:
        ada = ada.reshape(1, ada.shape[-1])
    for c0 in range(0, D_MODEL, COL_TILE):
        cols = (Ellipsis, slice(c0, c0 + COL_TILE))
        y = _dot(g, w_ref[:, c0:c0 + COL_TILE])
        gate = ada[..., 2 * D_MODEL + c0:2 * D_MODEL + c0 + COL_TILE]
        o_ref[cols] = x_ref[cols] + gate * y.reshape(x_ref.shape[:-1] + (COL_TILE,))


def _outproj(g, w_bf, x, ada, batch, seq, tm):
    n_tiles, tok_block, ada_block, shape, _ = _token_specs(batch, seq, tm)
    return pl.pallas_call(
        _outproj_kernel,
        grid=(n_tiles,),
        in_specs=[tok_block(w_bf.shape[0]), _resident(w_bf.shape), tok_block(D_MODEL), ada_block],
        out_specs=tok_block(D_MODEL),
        out_shape=jax.ShapeDtypeStruct(shape(D_MODEL), F32),
        compiler_params=_params(("parallel",)),
        name="outproj",
    )(g, w_bf, x, ada)


def _rope_table_kernel(pos_ref, inv_ref, cos_ref, sin_ref):
    ang = pos_ref[...] * inv_ref[...]
    lane = lax.broadcasted_iota(jnp.int32, ang.shape, 1)
    cos_ref[...] = jnp.cos(ang)
    sin_ref[...] = jnp.where(lane < B_HD // 2, -jnp.sin(ang), jnp.sin(ang))


def _rope_tables(start, seq):
    half = B_HD // 2
    inv = ROPE_THETA ** (-jnp.arange(half, dtype=F32) / half)
    inv2 = jnp.concatenate([inv, inv]).reshape(1, B_HD)
    pos = (start + jnp.arange(seq, dtype=jnp.int32)).astype(F32).reshape(seq, 1)
    return pl.pallas_call(
        _rope_table_kernel,
        out_shape=[jax.ShapeDtypeStruct((seq, B_HD), F32)] * 2,
        name="rope_tables",
    )(pos, inv2)


def _l1_inproj_kernel(x_ref, ada_ref, nw_ref, w_ref, qn_ref, kn_ref, cos_ref, sin_ref, q_ref, kv_ref, zs_ref):
    h = _modulated_norm(x_ref, ada_ref, nw_ref)
    tm = h.shape[0]
    cos = cos_ref[...]
    sin = sin_ref[...]
    reps = tm // cos.shape[0]
    if reps > 1:
        cos = jnp.broadcast_to(cos[None], (reps,) + cos.shape).reshape(tm, B_HD)
        sin = jnp.broadcast_to(sin[None], (reps,) + sin.shape).reshape(tm, B_HD)

    def norm_rope(acc, nw):
        outs = []
        for hd in range(acc.shape[1] // B_HD):
            xh = acc[:, hd * B_HD:(hd + 1) * B_HD]
            ms = jnp.mean(xh * xh, axis=-1, keepdims=True)
            y = xh * lax.rsqrt(ms + EPS) * nw
            outs.append(y * cos + pltpu.roll(y, B_HD // 2, 1) * sin)
        return jnp.concatenate(outs, axis=-1)

    for c0 in range(0, w_ref.shape[1], B_GW):
        acc = _dot(h, w_ref[:, c0:c0 + B_GW])
        seg, g = divmod(c0 // B_GW, len(B_GROUPS))
        if seg == 0:
            _put(q_ref, g * B_GW, norm_rope(acc, qn_ref[...]))
        elif seg == 1:
            _put(kv_ref, 2 * g * B_GW, norm_rope(acc, kn_ref[...]))
        elif seg == 2:
            _put(kv_ref, (2 * g + 1) * B_GW, acc)
        else:
            _put(zs_ref, g * B_GW, _silu(acc))


def _l1_inproj(x, ada, nw, w_bf, qn, kn, cos, sin, batch, seq, tm, act_dtype):
    n_tiles, tok_block, ada_block, shape, per = _token_specs(batch, seq, tm)
    if per is None:
        tab_block = _resident((seq, B_HD))
    else:
        tab_block = pl.BlockSpec((tm, B_HD), lambda i: (i % per, 0))
    return pl.pallas_call(
        _l1_inproj_kernel,
        grid=(n_tiles,),
        in_specs=[tok_block(D_MODEL), ada_block, _resident((1, D_MODEL)), _resident(w_bf.shape),
                  _resident((1, B_HD)), _resident((1, B_HD)), tab_block, tab_block],
        out_specs=[tok_block(B_WIDTH), tok_block(2 * B_WIDTH), tok_block(B_WIDTH)],
        out_shape=[
            jax.ShapeDtypeStruct(shape(B_WIDTH), F32),
            jax.ShapeDtypeStruct(shape(2 * B_WIDTH), F32),
            jax.ShapeDtypeStruct(shape(B_WIDTH), act_dtype),
        ],
        compiler_params=_params(("parallel",)),
        name="l1_inproj",
    )(x, ada, nw, w_bf, qn, kn, cos, sin)


Q_SUB = 128


def _softmax_parts(s, valid):
    s = jnp.where(valid, s, -jnp.inf)
    m = jnp.max(s, axis=-1, keepdims=True)
    p = jnp.exp(s - m)
    return m, p, jnp.sum(p, axis=-1, keepdims=True)


def _prompt_attn_kernel(q_ref, kc_ref, vc_ref, kp_ref, vp_ref, o_ref, lse_ref, *, tq, dil):
    first_key = jnp.where(pl.program_id(2) == 0, Q_SUB, 0)
    row = lax.broadcasted_iota(jnp.int32, (Q_SUB, 2 * Q_SUB), 0)
    col = lax.broadcasted_iota(jnp.int32, (Q_SUB, 2 * Q_SUB), 1)
    band = (col >= row) & (col <= row + Q_SUB)
    scale = B_HD ** -0.5

    def rows_of(r, first, n):
        start = r + first * dil
        return pl.ds(start, n) if dil == 1 else pl.ds(start, n, stride=dil)

    for r in range(dil):
        for sb in range(tq // Q_SUB):
            rows = rows_of(r, sb * Q_SUB, Q_SUB)
            if sb == 0:
                prev = rows_of(r, 0, Q_SUB)
                k_prev, v_prev = kp_ref[0, prev, :], vp_ref[0, prev, :]
                valid = band & (col >= first_key)
            else:
                prev = rows_of(r, (sb - 1) * Q_SUB, Q_SUB)
                k_prev, v_prev = kc_ref[0, prev, :], vc_ref[0, prev, :]
                valid = band
            k2 = jnp.concatenate([k_prev, kc_ref[0, rows, :]], axis=0).astype(BF16)
            v2 = jnp.concatenate([v_prev, vc_ref[0, rows, :]], axis=0).astype(BF16)
            s = _dot(q_ref[0, rows, :].astype(BF16), k2, _NT) * scale
            m, p, l = _softmax_parts(s, valid)
            o_ref[0, rows, :] = _dot(p.astype(BF16), v2) / l
            lse_ref[0, rows, :] = jnp.broadcast_to(m + jnp.log(l), (Q_SUB, B_HD))


def _prompt_attn_group(q, kv, g, dil, batch, seq):
    tq = min(512, seq // dil)
    tile = tq * dil
    back = Q_SUB * dil
    n_tiles = seq // tile
    q0, k0, v0 = g * B_HG, 2 * g * B_HG, (2 * g + 1) * B_HG
    cur = lambda c0: pl.BlockSpec((1, tile, B_HD), lambda b, h, i: (b, i, c0 + h))
    prev = lambda c0: pl.BlockSpec(
        (1, back, B_HD), lambda b, h, i: (b, jnp.maximum(i * (tile // back) - 1, 0), c0 + h))
    out = pl.BlockSpec((1, tile, B_HD), lambda b, h, i: (b, i, h))
    return pl.pallas_call(
        functools.partial(_prompt_attn_kernel, tq=tq, dil=dil),
        grid=(batch, B_HG, n_tiles),
        in_specs=[cur(q0), cur(k0), cur(v0), prev(k0), prev(v0)],
        out_specs=[out, out],
        out_shape=[jax.ShapeDtypeStruct((batch, seq, B_GW), F32)] * 2,
        compiler_params=_params(("parallel", "parallel", "arbitrary")),
        name=f"prompt_attn_g{g}",
    )(q, kv, kv, kv, kv)


def _group_mix_kernel(o0, o1, o2, l0, l1, l2, zs_ref, out_ref):
    lses = [l0[...], l1[...], l2[...]]
    m = jnp.maximum(jnp.maximum(lses[0], lses[1]), lses[2])
    es = [jnp.exp(l - m) for l in lses]
    inv = 1.0 / (es[0] + es[1] + es[2])
    for g, o in enumerate((o0, o1, o2)):
        cols = slice(g * B_GW, (g + 1) * B_GW)
        out_ref[:, cols] = (o[...] * (es[g] * inv) * zs_ref[:, cols]).astype(out_ref.dtype)


def _group_mix(outs, lses, zs, tm):
    n = zs.shape[0]
    blk = pl.BlockSpec((tm, B_GW), lambda i: (i, 0))
    wide = pl.BlockSpec((tm, B_WIDTH), lambda i: (i, 0))
    return pl.pallas_call(
        _group_mix_kernel,
        grid=(n // tm,),
        in_specs=[blk] * 6 + [wide],
        out_specs=wide,
        out_shape=jax.ShapeDtypeStruct((n, B_WIDTH), BF16),
        compiler_params=_params(("parallel",)),
        name="group_mix",
    )(*outs, *lses, zs)


KV_ROWS = 2 * B_HG


def _sample_masks(seq):
    hq = np.arange(B_HG * seq)[:, None] // seq
    iq = np.arange(B_HG * seq)[:, None] % seq
    masks = []
    for window, dil in B_GROUPS:
        n_tok = window if dil < seq else (window // dil) * seq
        col = np.arange(n_tok * KV_ROWS)[None, :]
        tok, is_v, head = col // KV_ROWS, (col // B_HG) % 2, col % B_HG
        if dil < seq:
            ok = (tok >= iq) & ((tok - iq) % dil == 0)
        else:
            ok = (tok % seq) == iq
        masks.append(np.where(ok & (is_v == 0) & (head == hq), 0.0, -np.inf).astype(np.float32))
    col = np.arange(B_HG * seq)[None, :]
    new = []
    for _, dil in B_GROUPS:
        ok = (col // seq == hq) & (col % seq <= iq) & ((iq - col % seq) % dil == 0)
        new.append(np.where(ok, 0.0, -np.inf).astype(np.float32))
    return masks, np.stack(new)


def _sample_attn_kernel(q_ref, kv_ref, zs_ref, c0_ref, c1_ref, c2_ref, m0_ref, m1_ref, m2_ref, mn_ref,
                        out_ref, *, seq):
    scale = B_HD ** -0.5
    caches = (c0_ref, c1_ref, c2_ref)
    masks = (m0_ref, m1_ref, m2_ref)
    heads = lambda ref, c0: jnp.concatenate(
        [ref[0, :, c0 + h * B_HD:c0 + (h + 1) * B_HD] for h in range(B_HG)], axis=0).astype(BF16)
    outs, lses = [], []
    for g in range(len(B_GROUPS)):
        qa = heads(q_ref, g * B_GW)
        k_new = heads(kv_ref, 2 * g * B_GW)
        v_new = heads(kv_ref, (2 * g + 1) * B_GW)
        rows = caches[g][0].reshape(-1, B_HD).astype(BF16)
        s_c = _dot(qa, rows, _NT) * scale + masks[g][...]
        s_n = _dot(qa, k_new, _NT) * scale + mn_ref[g]
        m = jnp.maximum(jnp.max(s_c, axis=-1, keepdims=True), jnp.max(s_n, axis=-1, keepdims=True))
        p_c = jnp.exp(s_c - m)
        p_n = jnp.exp(s_n - m)
        l = jnp.sum(p_c, axis=-1, keepdims=True) + jnp.sum(p_n, axis=-1, keepdims=True)
        p_v = pltpu.roll(p_c, B_HG, 1)
        acc = _dot(p_v.astype(BF16), rows) + _dot(p_n.astype(BF16), v_new)
        outs.append(acc / l)
        lses.append(m + jnp.log(l))
    m = functools.reduce(jnp.maximum, lses)
    es = [jnp.exp(l - m) for l in lses]
    inv = 1.0 / (es[0] + es[1] + es[2])
    for g in range(len(B_GROUPS)):
        mixed = outs[g] * (es[g] * inv)
        for h in range(B_HG):
            cols = slice(g * B_GW + h * B_HD, g * B_GW + (h + 1) * B_HD)
            out_ref[0, :, cols] = mixed[h * seq:(h + 1) * seq] * zs_ref[0, :, cols]


def _sample_attn(q, kv, zs, caches, batch, seq):
    masks, mask_new = _sample_masks(seq)
    specs, views = [], []
    for (window, dil), c in zip(B_GROUPS, caches):
        assert window == dil * (N_KEYS - 1) and c.shape[2] == window
        if dil >= seq:
            views.append(c.reshape(batch, window // dil, dil * KV_ROWS, B_HD))
            specs.append(pl.BlockSpec((1, window // dil, seq * KV_ROWS, B_HD), lambda b: (b, 0, 0, 0)))
        else:
            assert seq % dil == 0
            views.append(c.reshape(batch, window * KV_ROWS, B_HD))
            specs.append(pl.BlockSpec((1, window * KV_ROWS, B_HD), lambda b: (b, 0, 0)))
    const = lambda a: pl.BlockSpec(a.shape, lambda b: (0,) * a.ndim)
    tok = lambda w: pl.BlockSpec((1, seq, w), lambda b: (b, 0, 0))
    return pl.pallas_call(
        functools.partial(_sample_attn_kernel, seq=seq),
        grid=(batch,),
        in_specs=[tok(B_WIDTH), tok(2 * B_WIDTH), tok(B_WIDTH)] + specs
                 + [const(a) for a in masks] + [const(mask_new)],
        out_specs=tok(B_WIDTH),
        out_shape=jax.ShapeDtypeStruct((batch, seq, B_WIDTH), F32),
        compiler_params=_params(("parallel",)),
        name="sample_attn",
    )(q, kv, zs, *views, *[jnp.asarray(a) for a in masks], jnp.asarray(mask_new))


def _trunk(x, ada, start, state, caches, weights, tm_in, tm_out, scan_tile, scan_chunk):
    (norm_w, alb, a_w_in, a_b_f, a_gw, a_w_out, b_w_in, b_qn, b_kn, b_w_out) = weights
    batch, seq, _ = x.shape
    flat = seq >= tm_in
    act = BF16 if flat else F32
    tok = (lambda a: a.reshape(batch * seq, a.shape[-1])) if flat else (lambda a: a)
    rows = lambda a: a.reshape(batch * seq, a.shape[-1])
    ada0 = ada[0].reshape(batch, 1, 3 * D_MODEL)
    ada1 = ada[1].reshape(batch, 1, 3 * D_MODEL)

    x0 = tok(x)
    q, k, lf, v, zs = _l0_inproj(x0, ada0, norm_w[0:1], a_w_in, alb, a_b_f, batch, seq, tm_in, act)
    og, s_new = _hgrn_scan(rows(q), rows(k), rows(lf), rows(v), rows(zs), a_gw, state,
                           batch, seq, scan_tile, scan_chunk, act)
    og = og if flat else og.reshape(batch, seq, A_V)
    x1 = _outproj(og, a_w_out, x0, ada0, batch, seq, tm_out)

    cos, sin = _rope_tables(start, seq)
    qr, kv, zs1 = _l1_inproj(x1, ada1, norm_w[1:2], b_w_in, b_qn, b_kn, cos, sin, batch, seq, tm_in, act)
    kv3 = kv.reshape(batch, seq, 2 * B_WIDTH)
    if caches is None:
        outs, lses = [], []
        for g, (_, dil) in enumerate(B_GROUPS):
            o, lse = _prompt_attn_group(qr.reshape(batch, seq, B_WIDTH), kv3, g, dil, batch, seq)
            outs.append(o.reshape(batch * seq, B_GW))
            lses.append(lse.reshape(batch * seq, B_GW))
        og1 = _group_mix(outs, lses, zs1, MIX_TILE)
    else:
        og1 = _sample_attn(qr, kv, zs1, caches, batch, seq)
    x2 = _outproj(og1, b_w_out, x1, ada1, batch, seq, tm_out)

    kv_rows = [kv3[:, seq - min(window, seq):, 2 * g * B_GW:2 * (g + 1) * B_GW]
               .reshape(batch, min(window, seq), 2, B_HG, B_HD) for g, (window, _) in enumerate(B_GROUPS)]
    return x2.reshape(batch, seq, D_MODEL), s_new, kv_rows


def kernel(x_prompt, x_sample, state_hgrn, cache_kv_w128, cache_kv_w512, cache_kv_w2048, c_prompt, c_sample,
           norm_w, ada_w, ada_b, a_lower_bounds, a_w_in, a_b_f, a_g_norm_w, a_w_out, b_w_in, b_q_norm_w,
           b_k_norm_w, b_w_out):
    bp, lp, _ = x_prompt.shape
    bs, ls, _ = x_sample.shape

    pad = (-(bp + bs)) % 8
    c_all = jnp.concatenate([c_prompt, c_sample, jnp.zeros((pad, D_MODEL), F32)], axis=0)
    ada = _ada_vectors(c_all, ada_w, ada_b)
    ada_p, ada_s = ada[:, :bp], ada[:, bp:bp + bs]

    weights = (norm_w, a_lower_bounds, a_w_in[0].astype(BF16), a_b_f, a_g_norm_w, a_w_out[0].astype(BF16),
               b_w_in[0].astype(BF16), b_q_norm_w, b_k_norm_w, b_w_out[0].astype(BF16))

    y_p, s_p, kv_p = _trunk(x_prompt, ada_p, 0, None, None, weights,
                            tm_in=256, tm_out=512, scan_tile=256, scan_chunk=64)
    caches = (cache_kv_w128, cache_kv_w512, cache_kv_w2048)
    y_s, s_s, kv_s = _trunk(x_sample, ada_s, PAST_LEN, state_hgrn[0], caches, weights,
                            tm_in=256, tm_out=256, scan_tile=ls, scan_chunk=ls)

    kv_out = []
    for g in range(len(B_GROUPS)):
        kv_out.append(kv_p[g][None])
        kv_out.append(kv_s[g][None])
    return (y_p, y_s, s_p[None], s_s[None], *kv_out)
```

```python
import functools
import math

import numpy as np
import jax
import jax.numpy as jnp
from jax import lax
from jax.experimental import pallas as pl
from jax.experimental.pallas import tpu as pltpu

F32 = jnp.float32
BF16 = jnp.bfloat16

D_MODEL = 1024
EPS = 1e-6
A_HEADS = 8
A_KDIM = 128
A_VDIM = 256
A_QK = A_HEADS * A_KDIM
A_V = A_HEADS * A_VDIM
B_GROUPS = ((128, 1), (512, 4), (2048, 16))
B_HG = 4
B_HD = 128
B_GW = B_HG * B_HD
B_WIDTH = len(B_GROUPS) * B_GW
ROPE_THETA = 10000.0
PAST_LEN = 2048
N_KEYS = 129

COL_TILE = 512
MIX_TILE = 512
VMEM_LIMIT = 56 * 1024 * 1024

_NT = (((1,), (1,)), ((), ()))
_TN = (((0,), (0,)), ((), ()))


def _sigmoid(x):
    return 1.0 / (1.0 + jnp.exp(-x))


def _silu(x):
    h = 0.5 * x
    return h * jnp.tanh(h) + h


def _dot(a, b, dims=None):
    if dims is None:
        return jnp.dot(a, b, preferred_element_type=F32)
    return lax.dot_general(a, b, dims, preferred_element_type=F32)


def _params(sem):
    return pltpu.CompilerParams(dimension_semantics=sem, vmem_limit_bytes=VMEM_LIMIT)


def _ada_kernel(c_ref, w_ref, b_ref, o_ref):
    a = _silu(c_ref[...]).astype(BF16)
    o_ref[0] = _dot(a, w_ref[0].astype(BF16)) + b_ref[0]


def _ada_vectors(c_all, ada_w, ada_b):
    n_layers, _, width = ada_w.shape
    rows = c_all.shape[0]
    return pl.pallas_call(
        _ada_kernel,
        grid=(n_layers, width // COL_TILE),
        in_specs=[
            pl.BlockSpec((rows, D_MODEL), lambda l, j: (0, 0)),
            pl.BlockSpec((1, D_MODEL, COL_TILE), lambda l, j: (l, 0, j)),
            pl.BlockSpec((1, 1, COL_TILE), lambda l, j: (l, 0, j)),
        ],
        out_specs=pl.BlockSpec((1, rows, COL_TILE), lambda l, j: (l, 0, j)),
        out_shape=jax.ShapeDtypeStruct((n_layers, rows, width), F32),
        compiler_params=_params(("parallel", "parallel")),
        name="ada_vectors",
    )(c_all, ada_w, ada_b.reshape(n_layers, 1, width))


def _modulated_norm(x_ref, ada_ref, nw_ref):
    x = x_ref[...]
    ada = ada_ref[...]
    if x.ndim == 2:
        ada = ada.reshape(1, ada.shape[-1])
    shift = ada[..., :D_MODEL]
    scale = ada[..., D_MODEL:2 * D_MODEL]
    ms = jnp.mean(x * x, axis=-1, keepdims=True)
    y = x * lax.rsqrt(ms + EPS) * nw_ref[...].reshape((1,) * (x.ndim - 1) + (D_MODEL,))
    h = y * (1.0 + scale) + shift
    return h.reshape(-1, D_MODEL).astype(BF16)


def _token_specs(batch, seq, tm):
    if seq >= tm:
        assert seq % tm == 0
        per = seq // tm
        n_tiles = batch * per
        tok_block = lambda w: pl.BlockSpec((tm, w), lambda i: (i, 0))
        ada_block = pl.BlockSpec((1, 1, 3 * D_MODEL), lambda i: (i // per, 0, 0))
        shape = lambda w: (batch * seq, w)
        return n_tiles, tok_block, ada_block, shape, per
    tb = tm // seq
    assert batch % tb == 0
    n_tiles = batch // tb
    tok_block = lambda w: pl.BlockSpec((tb, seq, w), lambda i: (i, 0, 0))
    ada_block = pl.BlockSpec((tb, 1, 3 * D_MODEL), lambda i: (i, 0, 0))
    shape = lambda w: (batch, seq, w)
    return n_tiles, tok_block, ada_block, shape, None


def _resident(shape):
    return pl.BlockSpec(shape, lambda i: (0,) * len(shape), pipeline_mode=pl.Buffered(1))


def _put(ref, c0, val):
    ref[(Ellipsis, slice(c0, c0 + val.shape[-1]))] = val.reshape(ref.shape[:-1] + val.shape[-1:]).astype(ref.dtype)


def _l0_inproj_kernel(x_ref, ada_ref, nw_ref, w_ref, alb_ref, bf_ref, q_ref, k_ref, lf_ref, v_ref, zs_ref):
    h = _modulated_norm(x_ref, ada_ref, nw_ref)
    a = alb_ref[...]
    m = jnp.max(a, axis=0, keepdims=True)
    e = jnp.exp(a - m)
    lb = e[0:1] / jnp.sum(e, axis=0, keepdims=True)
    for c0 in range(0, w_ref.shape[1], COL_TILE):
        acc = _dot(h, w_ref[:, c0:c0 + COL_TILE])
        if c0 < A_QK:
            _put(q_ref, c0, _silu(acc))
        elif c0 < 2 * A_QK:
            c = c0 - A_QK
            f = lb[:, c:c + COL_TILE] + (1.0 - lb[:, c:c + COL_TILE]) * _sigmoid(acc + bf_ref[:, c:c + COL_TILE])
            _put(lf_ref, c, jnp.log(f))
            _put(k_ref, c, 1.0 - f)
        elif c0 < 2 * A_QK + A_V:
            _put(v_ref, c0 - 2 * A_QK, acc)
        else:
            _put(zs_ref, c0 - 2 * A_QK - A_V, _silu(acc))


def _l0_inproj(x, ada, nw, w_bf, alb, b_f, batch, seq, tm, act_dtype):
    n_tiles, tok_block, ada_block, shape, _ = _token_specs(batch, seq, tm)
    return pl.pallas_call(
        _l0_inproj_kernel,
        grid=(n_tiles,),
        in_specs=[tok_block(D_MODEL), ada_block, _resident((1, D_MODEL)), _resident(w_bf.shape),
                  _resident(alb.shape), _resident(b_f.shape)],
        out_specs=[tok_block(A_QK), tok_block(A_QK), tok_block(A_QK), tok_block(A_V), tok_block(A_V)],
        out_shape=[
            jax.ShapeDtypeStruct(shape(A_QK), F32),
            jax.ShapeDtypeStruct(shape(A_QK), F32),
            jax.ShapeDtypeStruct(shape(A_QK), F32),
            jax.ShapeDtypeStruct(shape(A_V), act_dtype),
            jax.ShapeDtypeStruct(shape(A_V), act_dtype),
        ],
        compiler_params=_params(("parallel",)),
        name="l0_inproj",
    )(x, ada, nw, w_bf, alb, b_f)


def _scan_levels(chunk):
    return [1 << l for l in range(int(math.log2(chunk)))]


def _exponent_matrix(chunk):
    r = np.arange(chunk)[:, None]
    c = np.arange(chunk)[None, :]
    blocks = [c <= r, c > r]
    for s in _scan_levels(chunk):
        bound = (r & ~(2 * s - 1)) | s
        upper = (r & s) != 0
        blocks.append(np.where(upper, (c > bound) & (c <= r), (c > r) & (c <= bound)))
    return np.concatenate(blocks, axis=0).astype(np.float32)


def _hgrn_chunk(q, k, lf, v, state, emat, chunk):
    dk = q.shape[-1]
    ex = jnp.exp(jnp.dot(emat, lf, preferred_element_type=F32, precision=lax.Precision.HIGHEST))
    e_b = ex[0:chunk]
    e_u = ex[chunk:2 * chunk]
    row = lax.broadcasted_iota(jnp.int32, (chunk, 1), 0)
    col = lax.broadcasted_iota(jnp.int32, (1, chunk), 1)

    att = jnp.where(row == col, jnp.sum(q * k, axis=-1, keepdims=True), 0.0)
    for l, s in enumerate(_scan_levels(chunk)):
        g = ex[(2 + l) * chunk:(3 + l) * chunk]
        upper = (row & s) != 0
        xb = (jnp.where(upper, q, k) * g).astype(BF16)
        part = _dot(xb, xb, _NT)
        keep = upper & ((col & s) == 0) & ((row >> (l + 1)) == (col >> (l + 1)))
        att = att + jnp.where(keep, part, 0.0)

    o = _dot(att.astype(BF16), v.astype(BF16)) + _dot((q * e_b).astype(BF16), state.astype(BF16))

    e_last = e_b[chunk - 1:chunk]
    eye = lax.broadcasted_iota(jnp.int32, (dk, dk), 0) == lax.broadcasted_iota(jnp.int32, (dk, dk), 1)
    e_col = jnp.sum(jnp.where(eye, e_last, 0.0), axis=-1, keepdims=True)
    new_state = state * e_col + _dot((k * e_u).astype(BF16), v.astype(BF16), _TN)
    return o, new_state


SAFE_DECAY = 60.0
SPLIT_PAD = 32


def _split_prefix_matrix(chunk):
    t = _exponent_matrix(chunk)[:2 * chunk]
    pad = (-3 * chunk) % SPLIT_PAD
    return np.concatenate([t, t, t, np.zeros((2 * chunk, pad), np.float32)], axis=1)


def _hgrn_chunk_bounded(q, k, lf, v, states, tmat, chunk):
    hi = lf.astype(BF16)
    r1 = lf - hi.astype(F32)
    mid = r1.astype(BF16)
    lo = r1 - mid.astype(F32)
    parts = [hi.astype(F32), mid.astype(F32), lo]
    pad = tmat.shape[1] - 3 * chunk
    if pad:
        parts.append(jnp.zeros((pad, lf.shape[1]), F32))
    sums = _dot(tmat, jnp.concatenate(parts, axis=0).astype(BF16))
    b, u = sums[:chunk], sums[chunk:]
    e_b = jnp.exp(b)
    qb = (q * e_b).astype(BF16)
    kn = (k * jnp.exp(-b)).astype(BF16)
    ku = (k * jnp.exp(u)).astype(BF16)
    vb = v.astype(BF16)
    row = lax.broadcasted_iota(jnp.int32, (chunk, chunk), 0)
    col = lax.broadcasted_iota(jnp.int32, (chunk, chunk), 1)
    eye = lax.broadcasted_iota(jnp.int32, (A_KDIM, A_KDIM), 0) == lax.broadcasted_iota(jnp.int32, (A_KDIM, A_KDIM), 1)
    ks = [slice(h * A_KDIM, (h + 1) * A_KDIM) for h in range(A_HEADS)]
    vs = [slice(h * A_VDIM, (h + 1) * A_VDIM) for h in range(A_HEADS)]
    atts = [jnp.where(row >= col, _dot(qb[:, ks[h]], kn[:, ks[h]], _NT), 0.0).astype(BF16)
            for h in range(A_HEADS)]
    outs = [_dot(atts[h], vb[:, vs[h]]) + _dot(qb[:, ks[h]], states[h].astype(BF16)) for h in range(A_HEADS)]
    new_states = []
    for h in range(A_HEADS):
        e_col = jnp.sum(jnp.where(eye, e_b[chunk - 1:chunk, ks[h]], 0.0), axis=-1, keepdims=True)
        new_states.append(states[h] * e_col + _dot(ku[:, ks[h]], vb[:, vs[h]], _TN))
    return outs, new_states


def _hgrn_scan_kernel(*refs, chunk, n_chunks, has_state, one_shot):
    if has_state:
        q_ref, k_ref, lf_ref, v_ref, zs_ref, gw_ref, e_ref, t_ref, s0_ref, og_ref, so_ref, s_scr = refs
    else:
        q_ref, k_ref, lf_ref, v_ref, zs_ref, gw_ref, e_ref, t_ref, og_ref, so_ref, s_scr = refs
    t = pl.program_id(1)

    if not one_shot:
        @pl.when(t == 0)
        def _():
            if has_state:
                s_scr[...] = s0_ref[0]
            else:
                s_scr[...] = jnp.zeros_like(s_scr)

    def load_state(h):
        return s0_ref[0, h] if one_shot else s_scr[h]

    def store_state(h, val):
        if one_shot:
            so_ref[0, h] = val
        else:
            s_scr[h] = val

    gw = gw_ref[...]

    def finish(rows, h, o):
        vc = slice(h * A_VDIM, (h + 1) * A_VDIM)
        ms = jnp.mean(o * o, axis=-1, keepdims=True)
        og_ref[rows, vc] = (o * lax.rsqrt(ms + EPS) * gw * zs_ref[rows, vc]).astype(og_ref.dtype)

    def chunk_rows(c):
        return slice(0, chunk) if n_chunks == 1 else pl.ds(pl.multiple_of(c * chunk, chunk), chunk)

    def bounded_body(c, carry):
        rows = chunk_rows(c)
        outs, new_states = _hgrn_chunk_bounded(q_ref[rows, :], k_ref[rows, :], lf_ref[rows, :], v_ref[rows, :],
                                               [load_state(h) for h in range(A_HEADS)], t_ref[...], chunk)
        for h in range(A_HEADS):
            store_state(h, new_states[h])
            finish(rows, h, outs[h])
        return carry

    def general_body(c, carry):
        rows = chunk_rows(c)
        for h in range(A_HEADS):
            kc = slice(h * A_KDIM, (h + 1) * A_KDIM)
            vc = slice(h * A_VDIM, (h + 1) * A_VDIM)
            o, s_new = _hgrn_chunk(q_ref[rows, kc], k_ref[rows, kc], lf_ref[rows, kc],
                                   v_ref[rows, vc], load_state(h), e_ref[...], chunk)
            store_state(h, s_new)
            finish(rows, h, o)
        return carry

    def run(body):
        def go():
            if n_chunks == 1:
                body(0, 0)
            else:
                lax.fori_loop(0, n_chunks, body, 0)
        return go

    bounded = jnp.min(lf_ref[...]) * chunk >= -SAFE_DECAY
    lax.cond(bounded, run(bounded_body), run(general_body))

    if not one_shot:
        @pl.when(t == pl.num_programs(1) - 1)
        def _():
            so_ref[0] = s_scr[...]


def _hgrn_scan(q, k, lf, v, zs, gw, s0, batch, seq, tile, chunk, out_dtype):
    n_t = seq // tile
    emat = jnp.asarray(_exponent_matrix(chunk))
    tmat = jnp.asarray(_split_prefix_matrix(chunk), dtype=BF16)
    row_block = lambda w: pl.BlockSpec((tile, w), lambda b, t: (b * n_t + t, 0))
    state_block = pl.BlockSpec((1, A_HEADS, A_KDIM, A_VDIM), lambda b, t: (b, 0, 0, 0))
    in_specs = [row_block(A_QK), row_block(A_QK), row_block(A_QK), row_block(A_V), row_block(A_V),
                pl.BlockSpec((1, A_VDIM), lambda b, t: (0, 0)),
                pl.BlockSpec(emat.shape, lambda b, t: (0, 0)),
                pl.BlockSpec(tmat.shape, lambda b, t: (0, 0))]
    args = [q, k, lf, v, zs, gw, emat, tmat]
    if s0 is not None:
        in_specs.append(state_block)
        args.append(s0)
    kern = functools.partial(_hgrn_scan_kernel, chunk=chunk, n_chunks=tile // chunk,
                             has_state=s0 is not None, one_shot=s0 is not None and seq == chunk)
    return pl.pallas_call(
        kern,
        grid=(batch, n_t),
        in_specs=in_specs,
        out_specs=[row_block(A_V), state_block],
        out_shape=[jax.ShapeDtypeStruct((batch * seq, A_V), out_dtype),
                   jax.ShapeDtypeStruct((batch, A_HEADS, A_KDIM, A_VDIM), F32)],
        scratch_shapes=[pltpu.VMEM((A_HEADS, A_KDIM, A_VDIM), F32)],
        compiler_params=_params(("parallel", "arbitrary")),
        name="hgrn_scan",
    )(*args)


def _outproj_kernel(g_ref, w_ref, x_ref, ada_ref, o_ref):
    g = g_ref[...]
    g = g.reshape(-1, g.shape[-1]).astype(BF16)
    ada = ada_ref[...]
    if len(x_ref.shape) == 2:
        ada = ada.reshape(1, ada.shape[-1])
    for c0 in range(0, D_MODEL, COL_TILE):
        cols = (Ellipsis, slice(c0, c0 + COL_TILE))
        y = _dot(g, w_ref[:, c0:c0 + COL_TILE])
        gate = ada[..., 2 * D_MODEL + c0:2 * D_MODEL + c0 + COL_TILE]
        o_ref[cols] = x_ref[cols] + gate * y.reshape(x_ref.shape[:-1] + (COL_TILE,))


def _outproj(g, w_bf, x, ada, batch, seq, tm):
    n_tiles, tok_block, ada_block, shape, _ = _token_specs(batch, seq, tm)
    return pl.pallas_call(
        _outproj_kernel,
        grid=(n_tiles,),
        in_specs=[tok_block(w_bf.shape[0]), _resident(w_bf.shape), tok_block(D_MODEL), ada_block],
        out_specs=tok_block(D_MODEL),
        out_shape=jax.ShapeDtypeStruct(shape(D_MODEL), F32),
        compiler_params=_params(("parallel",)),
        name="outproj",
    )(g, w_bf, x, ada)


def _rope_table_kernel(pos_ref, inv_ref, cos_ref, sin_ref):
    ang = pos_ref[...] * inv_ref[...]
    lane = lax.broadcasted_iota(jnp.int32, ang.shape, 1)
    cos_ref[...] = jnp.cos(ang)
    sin_ref[...] = jnp.where(lane < B_HD // 2, -jnp.sin(ang), jnp.sin(ang))


def _rope_tables(start, seq):
    half = B_HD // 2
    inv = ROPE_THETA ** (-jnp.arange(half, dtype=F32) / half)
    inv2 = jnp.concatenate([inv, inv]).reshape(1, B_HD)
    pos = (start + jnp.arange(seq, dtype=jnp.int32)).astype(F32).reshape(seq, 1)
    return pl.pallas_call(
        _rope_table_kernel,
        out_shape=[jax.ShapeDtypeStruct((seq, B_HD), F32)] * 2,
        name="rope_tables",
    )(pos, inv2)


def _l1_inproj_kernel(x_ref, ada_ref, nw_ref, w_ref, qn_ref, kn_ref, cos_ref, sin_ref, q_ref, kv_ref, zs_ref,
                      *row_refs, row_tokens):
    h = _modulated_norm(x_ref, ada_ref, nw_ref)

    def put_rows(g, is_v, tile):
        n = row_tokens[g]
        for hd in range(B_HG):
            row_refs[g][pl.ds(is_v * B_HG + hd, n, stride=KV_ROWS), :] = (
                tile[tile.shape[0] - n:, hd * B_HD:(hd + 1) * B_HD])

    tm = h.shape[0]
    cos = cos_ref[...]
    sin = sin_ref[...]
    reps = tm // cos.shape[0]
    if reps > 1:
        cos = jnp.broadcast_to(cos[None], (reps,) + cos.shape).reshape(tm, B_HD)
        sin = jnp.broadcast_to(sin[None], (reps,) + sin.shape).reshape(tm, B_HD)

    def norm_rope(acc, nw):
        outs = []
        for hd in range(acc.shape[1] // B_HD):
            xh = acc[:, hd * B_HD:(hd + 1) * B_HD]
            ms = jnp.mean(xh * xh, axis=-1, keepdims=True)
            y = xh * lax.rsqrt(ms + EPS) * nw
            outs.append(y * cos + pltpu.roll(y, B_HD // 2, 1) * sin)
        return jnp.concatenate(outs, axis=-1)

    for c0 in range(0, w_ref.shape[1], B_GW):
        acc = _dot(h, w_ref[:, c0:c0 + B_GW])
        seg, g = divmod(c0 // B_GW, len(B_GROUPS))
        if seg == 0:
            _put(q_ref, g * B_GW, norm_rope(acc, qn_ref[...]))
        elif seg == 1:
            kt = norm_rope(acc, kn_ref[...])
            _put(kv_ref, 2 * g * B_GW, kt)
            if row_refs:
                put_rows(g, 0, kt)
        elif seg == 2:
            _put(kv_ref, (2 * g + 1) * B_GW, acc)
            if row_refs:
                put_rows(g, 1, acc)
        else:
            _put(zs_ref, g * B_GW, _silu(acc))


def _l1_inproj(x, ada, nw, w_bf, qn, kn, cos, sin, batch, seq, tm, act_dtype):
    n_tiles, tok_block, ada_block, shape, per = _token_specs(batch, seq, tm)
    out_specs = [tok_block(B_WIDTH), tok_block(2 * B_WIDTH), tok_block(B_WIDTH)]
    out_shape = [jax.ShapeDtypeStruct(shape(B_WIDTH), F32),
                 jax.ShapeDtypeStruct(shape(2 * B_WIDTH), F32),
                 jax.ShapeDtypeStruct(shape(B_WIDTH), act_dtype)]
    row_tokens = ()
    if per is None:
        tab_block = _resident((seq, B_HD))
    else:
        tab_block = pl.BlockSpec((tm, B_HD), lambda i: (i % per, 0))
        for window, _ in B_GROUPS:
            keep = min(window, seq)
            blk = min(keep, tm)
            n_blk = keep // blk
            first = per - n_blk
            out_specs.append(pl.BlockSpec(
                (blk * KV_ROWS, B_HD),
                lambda i, n_blk=n_blk, first=first: ((i // per) * n_blk + jnp.clip(i % per - first, 0, n_blk - 1), 0)))
            out_shape.append(jax.ShapeDtypeStruct((batch * keep * KV_ROWS, B_HD), F32))
            row_tokens += (blk,)
    return pl.pallas_call(
        functools.partial(_l1_inproj_kernel, row_tokens=row_tokens),
        grid=(n_tiles,),
        in_specs=[tok_block(D_MODEL), ada_block, _resident((1, D_MODEL)), _resident(w_bf.shape),
                  _resident((1, B_HD)), _resident((1, B_HD)), tab_block, tab_block],
        out_specs=out_specs,
        out_shape=out_shape,
        compiler_params=_params(("arbitrary",)),
        name="l1_inproj",
    )(x, ada, nw, w_bf, qn, kn, cos, sin)


Q_SUB = 128


def _softmax_parts(s, valid):
    s = jnp.where(valid, s, -jnp.inf)
    m = jnp.max(s, axis=-1, keepdims=True)
    p = jnp.exp(s - m)
    return m, p, jnp.sum(p, axis=-1, keepdims=True)


def _prompt_attn_kernel(q_ref, kc_ref, vc_ref, kp_ref, vp_ref, o_ref, lse_ref, *, tq, dil):
    first_key = jnp.where(pl.program_id(2) == 0, Q_SUB, 0)
    row = lax.broadcasted_iota(jnp.int32, (Q_SUB, 2 * Q_SUB), 0)
    col = lax.broadcasted_iota(jnp.int32, (Q_SUB, 2 * Q_SUB), 1)
    band = (col >= row) & (col <= row + Q_SUB)
    scale = B_HD ** -0.5

    def rows_of(r, first, n):
        start = r + first * dil
        return pl.ds(start, n) if dil == 1 else pl.ds(start, n, stride=dil)

    for r in range(dil):
        for sb in range(tq // Q_SUB):
            rows = rows_of(r, sb * Q_SUB, Q_SUB)
            if sb == 0:
                prev = rows_of(r, 0, Q_SUB)
                k_prev, v_prev = kp_ref[0, prev, :], vp_ref[0, prev, :]
                valid = band & (col >= first_key)
            else:
                prev = rows_of(r, (sb - 1) * Q_SUB, Q_SUB)
                k_prev, v_prev = kc_ref[0, prev, :], vc_ref[0, prev, :]
                valid = band
            k2 = jnp.concatenate([k_prev, kc_ref[0, rows, :]], axis=0).astype(BF16)
            v2 = jnp.concatenate([v_prev, vc_ref[0, rows, :]], axis=0).astype(BF16)
            s = _dot(q_ref[0, rows, :].astype(BF16), k2, _NT) * scale
            m, p, l = _softmax_parts(s, valid)
            o_ref[0, rows, :] = _dot(p.astype(BF16), v2) / l
            lse_ref[0, rows, :] = jnp.broadcast_to(m + jnp.log(l), (Q_SUB, B_HD))


def _prompt_attn_group(q, kv, g, dil, batch, seq):
    tq = min(512, seq // dil)
    tile = tq * dil
    back = Q_SUB * dil
    n_tiles = seq // tile
    q0, k0, v0 = g * B_HG, 2 * g * B_HG, (2 * g + 1) * B_HG
    cur = lambda c0: pl.BlockSpec((1, tile, B_HD), lambda b, h, i: (b, i, c0 + h))
    prev = lambda c0: pl.BlockSpec(
        (1, back, B_HD), lambda b, h, i: (b, jnp.maximum(i * (tile // back) - 1, 0), c0 + h))
    out = pl.BlockSpec((1, tile, B_HD), lambda b, h, i: (b, i, h))
    return pl.pallas_call(
        functools.partial(_prompt_attn_kernel, tq=tq, dil=dil),
        grid=(batch, B_HG, n_tiles),
        in_specs=[cur(q0), cur(k0), cur(v0), prev(k0), prev(v0)],
        out_specs=[out, out],
        out_shape=[jax.ShapeDtypeStruct((batch, seq, B_GW), F32)] * 2,
        compiler_params=_params(("parallel", "parallel", "arbitrary")),
        name=f"prompt_attn_g{g}",
    )(q, kv, kv, kv, kv)


def _group_mix_kernel(o0, o1, o2, l0, l1, l2, zs_ref, out_ref):
    lses = [l0[...], l1[...], l2[...]]
    m = jnp.maximum(jnp.maximum(lses[0], lses[1]), lses[2])
    es = [jnp.exp(l - m) for l in lses]
    inv = 1.0 / (es[0] + es[1] + es[2])
    for g, o in enumerate((o0, o1, o2)):
        cols = slice(g * B_GW, (g + 1) * B_GW)
        out_ref[:, cols] = (o[...] * (es[g] * inv) * zs_ref[:, cols]).astype(out_ref.dtype)


def _group_mix(outs, lses, zs, tm):
    n = zs.shape[0]
    blk = pl.BlockSpec((tm, B_GW), lambda i: (i, 0))
    wide = pl.BlockSpec((tm, B_WIDTH), lambda i: (i, 0))
    return pl.pallas_call(
        _group_mix_kernel,
        grid=(n // tm,),
        in_specs=[blk] * 6 + [wide],
        out_specs=wide,
        out_shape=jax.ShapeDtypeStruct((n, B_WIDTH), BF16),
        compiler_params=_params(("parallel",)),
        name="group_mix",
    )(*outs, *lses, zs)


KV_ROWS = 2 * B_HG


def _sample_masks(seq):
    hq = np.arange(B_HG * seq)[:, None] // seq
    iq = np.arange(B_HG * seq)[:, None] % seq
    masks = []
    for window, dil in B_GROUPS:
        n_tok = window if dil < seq else (window // dil) * seq
        col = np.arange(n_tok * KV_ROWS)[None, :]
        tok, is_v, head = col // KV_ROWS, (col // B_HG) % 2, col % B_HG
        if dil < seq:
            ok = (tok >= iq) & ((tok - iq) % dil == 0)
        else:
            ok = (tok % seq) == iq
        masks.append(np.where(ok & (is_v == 0) & (head == hq), 0.0, -np.inf).astype(np.float32))
    col = np.arange(B_HG * seq)[None, :]
    new = []
    for _, dil in B_GROUPS:
        ok = (col // seq == hq) & (col % seq <= iq) & ((iq - col % seq) % dil == 0)
        new.append(np.where(ok, 0.0, -np.inf).astype(np.float32))
    return masks, np.stack(new)


def _sample_attn_kernel(q_ref, kv_ref, zs_ref, c0_ref, c1_ref, c2_ref, m0_ref, m1_ref, m2_ref, mn_ref,
                        out_ref, *, seq):
    scale = B_HD ** -0.5
    caches = (c0_ref, c1_ref, c2_ref)
    masks = (m0_ref, m1_ref, m2_ref)
    heads = lambda ref, c0: jnp.concatenate(
        [ref[0, :, c0 + h * B_HD:c0 + (h + 1) * B_HD] for h in range(B_HG)], axis=0).astype(BF16)
    groups = range(len(B_GROUPS))
    qa = [heads(q_ref, g * B_GW) for g in groups]
    k_new = [heads(kv_ref, 2 * g * B_GW) for g in groups]
    v_new = [heads(kv_ref, (2 * g + 1) * B_GW) for g in groups]
    rows = [caches[g][0].reshape(-1, B_HD).astype(BF16) for g in groups]
    s_c = [_dot(qa[g], rows[g], _NT) * scale + masks[g][...] for g in groups]
    s_n = [_dot(qa[g], k_new[g], _NT) * scale + mn_ref[g] for g in groups]
    ms = [jnp.maximum(jnp.max(s_c[g], axis=-1, keepdims=True), jnp.max(s_n[g], axis=-1, keepdims=True))
          for g in groups]
    p_c = [jnp.exp(s_c[g] - ms[g]) for g in groups]
    p_n = [jnp.exp(s_n[g] - ms[g]) for g in groups]
    ls = [jnp.sum(p_c[g], axis=-1, keepdims=True) + jnp.sum(p_n[g], axis=-1, keepdims=True) for g in groups]
    p_v = [pltpu.roll(p_c[g], B_HG, 1).astype(BF16) for g in groups]
    accs = [_dot(p_v[g], rows[g]) + _dot(p_n[g].astype(BF16), v_new[g]) for g in groups]
    outs = [accs[g] / ls[g] for g in groups]
    lses = [ms[g] + jnp.log(ls[g]) for g in groups]
    m = functools.reduce(jnp.maximum, lses)
    es = [jnp.exp(l - m) for l in lses]
    inv = 1.0 / (es[0] + es[1] + es[2])
    for g in range(len(B_GROUPS)):
        mixed = outs[g] * (es[g] * inv)
        for h in range(B_HG):
            cols = slice(g * B_GW + h * B_HD, g * B_GW + (h + 1) * B_HD)
            out_ref[0, :, cols] = mixed[h * seq:(h + 1) * seq] * zs_ref[0, :, cols]


def _sample_attn(q, kv, zs, caches, batch, seq):
    masks, mask_new = _sample_masks(seq)
    specs, views = [], []
    for (window, dil), c in zip(B_GROUPS, caches):
        assert window == dil * (N_KEYS - 1) and c.shape[2] == window
        if dil >= seq:
            views.append(c.reshape(batch, window // dil, dil * KV_ROWS, B_HD))
            specs.append(pl.BlockSpec((1, window // dil, seq * KV_ROWS, B_HD), lambda b: (b, 0, 0, 0)))
        else:
            assert seq % dil == 0
            views.append(c.reshape(batch, window * KV_ROWS, B_HD))
            specs.append(pl.BlockSpec((1, window * KV_ROWS, B_HD), lambda b: (b, 0, 0)))
    const = lambda a: pl.BlockSpec(a.shape, lambda b: (0,) * a.ndim)
    tok = lambda w: pl.BlockSpec((1, seq, w), lambda b: (b, 0, 0))
    return pl.pallas_call(
        functools.partial(_sample_attn_kernel, seq=seq),
        grid=(batch,),
        in_specs=[tok(B_WIDTH), tok(2 * B_WIDTH), tok(B_WIDTH)] + specs
                 + [const(a) for a in masks] + [const(mask_new)],
        out_specs=tok(B_WIDTH),
        out_shape=jax.ShapeDtypeStruct((batch, seq, B_WIDTH), F32),
        compiler_params=_params(("parallel",)),
        name="sample_attn",
    )(q, kv, zs, *views, *[jnp.asarray(a) for a in masks], jnp.asarray(mask_new))


def _trunk(x, ada, start, state, caches, weights, tm_in, tm_out, scan_tile, scan_chunk):
    (norm_w, alb, a_w_in, a_b_f, a_gw, a_w_out, b_w_in, b_qn, b_kn, b_w_out) = weights
    batch, seq, _ = x.shape
    flat = seq >= tm_in
    act = BF16 if flat else F32
    tok = (lambda a: a.reshape(batch * seq, a.shape[-1])) if flat else (lambda a: a)
    rows = lambda a: a.reshape(batch * seq, a.shape[-1])
    ada0 = ada[0].reshape(batch, 1, 3 * D_MODEL)
    ada1 = ada[1].reshape(batch, 1, 3 * D_MODEL)

    x0 = tok(x)
    q, k, lf, v, zs = _l0_inproj(x0, ada0, norm_w[0:1], a_w_in, alb, a_b_f, batch, seq, tm_in, act)
    og, s_new = _hgrn_scan(rows(q), rows(k), rows(lf), rows(v), rows(zs), a_gw, state,
                           batch, seq, scan_tile, scan_chunk, act)
    og = og if flat else og.reshape(batch, seq, A_V)
    x1 = _outproj(og, a_w_out, x0, ada0, batch, seq, tm_out)

    cos, sin = _rope_tables(start, seq)
    qr, kv, zs1, *new_rows = _l1_inproj(x1, ada1, norm_w[1:2], b_w_in, b_qn, b_kn, cos, sin, batch, seq, tm_in, act)
    kv3 = kv.reshape(batch, seq, 2 * B_WIDTH)
    if caches is None:
        outs, lses = [], []
        for g, (_, dil) in enumerate(B_GROUPS):
            o, lse = _prompt_attn_group(qr.reshape(batch, seq, B_WIDTH), kv3, g, dil, batch, seq)
            outs.append(o.reshape(batch * seq, B_GW))
            lses.append(lse.reshape(batch * seq, B_GW))
        og1 = _group_mix(outs, lses, zs1, MIX_TILE)
    else:
        og1 = _sample_attn(qr, kv, zs1, caches, batch, seq)
    x2 = _outproj(og1, b_w_out, x1, ada1, batch, seq, tm_out)

    if new_rows:
        kv_rows = [r.reshape(batch, min(window, seq), 2, B_HG, B_HD) for r, (window, _) in zip(new_rows, B_GROUPS)]
    else:
        kv_rows = [kv3[:, seq - min(window, seq):, 2 * g * B_GW:2 * (g + 1) * B_GW]
                   .reshape(batch, min(window, seq), 2, B_HG, B_HD) for g, (window, _) in enumerate(B_GROUPS)]
    return x2.reshape(batch, seq, D_MODEL), s_new, kv_rows


def kernel(x_prompt, x_sample, state_hgrn, cache_kv_w128, cache_kv_w512, cache_kv_w2048, c_prompt, c_sample,
           norm_w, ada_w, ada_b, a_lower_bounds, a_w_in, a_b_f, a_g_norm_w, a_w_out, b_w_in, b_q_norm_w,
           b_k_norm_w, b_w_out):
    bp, lp, _ = x_prompt.shape
    bs, ls, _ = x_sample.shape

    pad = (-(bp + bs)) % 8
    c_all = jnp.concatenate([c_prompt, c_sample, jnp.zeros((pad, D_MODEL), F32)], axis=0)
    ada = _ada_vectors(c_all, ada_w, ada_b)
    ada_p, ada_s = ada[:, :bp], ada[:, bp:bp + bs]

    weights = (norm_w, a_lower_bounds, a_w_in[0].astype(BF16), a_b_f, a_g_norm_w, a_w_out[0].astype(BF16),
               b_w_in[0].astype(BF16), b_q_norm_w, b_k_norm_w, b_w_out[0].astype(BF16))

    y_p, s_p, kv_p = _trunk(x_prompt, ada_p, 0, None, None, weights,
                            tm_in=256, tm_out=512, scan_tile=256, scan_chunk=64)
    caches = (cache_kv_w128, cache_kv_w512, cache_kv_w2048)
    y_s, s_s, kv_s = _trunk(x_sample, ada_s, PAST_LEN, state_hgrn[0], caches, weights,
                            tm_in=256, tm_out=256, scan_tile=ls, scan_chunk=ls)

    kv_out = []
    for g in range(len(B_GROUPS)):
        kv_out.append(kv_p[g][None])
        kv_out.append(kv_s[g][None])
    return (y_p, y_s, s_p[None], s_s[None], *kv_out)
```

```python
import functools
import math

import numpy as np
import jax
import jax.numpy as jnp
from jax import lax
from jax.experimental import pallas as pl
from jax.experimental.pallas import tpu as pltpu

F32 = jnp.float32
BF16 = jnp.bfloat16

D_MODEL = 1024
EPS = 1e-6
A_HEADS = 8
A_KDIM = 128
A_VDIM = 256
A_QK = A_HEADS * A_KDIM
A_V = A_HEADS * A_VDIM
B_GROUPS = ((128, 1), (512, 4), (2048, 16))
B_HG = 4
B_HD = 128
B_GW = B_HG * B_HD
B_WIDTH = len(B_GROUPS) * B_GW
ROPE_THETA = 10000.0
PAST_LEN = 2048
N_KEYS = 129

COL_TILE = 512
MIX_TILE = 512
VMEM_LIMIT = 56 * 1024 * 1024

_NT = (((1,), (1,)), ((), ()))
_TN = (((0,), (0,)), ((), ()))


def _sigmoid(x):
    return 1.0 / (1.0 + jnp.exp(-x))


def _silu(x):
    h = 0.5 * x
    return h * jnp.tanh(h) + h


def _dot(a, b, dims=None):
    if dims is None:
        return jnp.dot(a, b, preferred_element_type=F32)
    return lax.dot_general(a, b, dims, preferred_element_type=F32)


def _params(sem):
    return pltpu.CompilerParams(dimension_semantics=sem, vmem_limit_bytes=VMEM_LIMIT)


def _ada_kernel(c_ref, w_ref, b_ref, o_ref):
    a = _silu(c_ref[...]).astype(BF16)
    o_ref[0] = _dot(a, w_ref[0].astype(BF16)) + b_ref[0]


def _ada_vectors(c_all, ada_w, ada_b):
    n_layers, _, width = ada_w.shape
    rows = c_all.shape[0]
    return pl.pallas_call(
        _ada_kernel,
        grid=(n_layers, width // COL_TILE),
        in_specs=[
            pl.BlockSpec((rows, D_MODEL), lambda l, j: (0, 0)),
            pl.BlockSpec((1, D_MODEL, COL_TILE), lambda l, j: (l, 0, j)),
            pl.BlockSpec((1, 1, COL_TILE), lambda l, j: (l, 0, j)),
        ],
        out_specs=pl.BlockSpec((1, rows, COL_TILE), lambda l, j: (l, 0, j)),
        out_shape=jax.ShapeDtypeStruct((n_layers, rows, width), F32),
        compiler_params=_params(("parallel", "parallel")),
        name="ada_vectors",
    )(c_all, ada_w, ada_b.reshape(n_layers, 1, width))


def _modulated_norm(x_ref, ada_ref, nw_ref):
    x = x_ref[...]
    ada = ada_ref[...]
    if x.ndim == 2:
        ada = ada.reshape(1, ada.shape[-1])
    shift = ada[..., :D_MODEL]
    scale = ada[..., D_MODEL:2 * D_MODEL]
    ms = jnp.mean(x * x, axis=-1, keepdims=True)
    y = x * lax.rsqrt(ms + EPS) * nw_ref[...].reshape((1,) * (x.ndim - 1) + (D_MODEL,))
    h = y * (1.0 + scale) + shift
    return h.reshape(-1, D_MODEL).astype(BF16)


def _token_specs(batch, seq, tm):
    if seq >= tm:
        assert seq % tm == 0
        per = seq // tm
        n_tiles = batch * per
        tok_block = lambda w: pl.BlockSpec((tm, w), lambda i: (i, 0))
        ada_block = pl.BlockSpec((1, 1, 3 * D_MODEL), lambda i: (i // per, 0, 0))
        shape = lambda w: (batch * seq, w)
        return n_tiles, tok_block, ada_block, shape, per
    tb = tm // seq
    assert batch % tb == 0
    n_tiles = batch // tb
    tok_block = lambda w: pl.BlockSpec((tb, seq, w), lambda i: (i, 0, 0))
    ada_block = pl.BlockSpec((tb, 1, 3 * D_MODEL), lambda i: (i, 0, 0))
    shape = lambda w: (batch, seq, w)
    return n_tiles, tok_block, ada_block, shape, None


def _resident(shape):
    return pl.BlockSpec(shape, lambda i: (0,) * len(shape), pipeline_mode=pl.Buffered(1))


def _put(ref, c0, val):
    ref[(Ellipsis, slice(c0, c0 + val.shape[-1]))] = val.reshape(ref.shape[:-1] + val.shape[-1:]).astype(ref.dtype)


def _l0_inproj_kernel(x_ref, ada_ref, nw_ref, w_ref, alb_ref, bf_ref, q_ref, k_ref, lf_ref, v_ref, zs_ref):
    h = _modulated_norm(x_ref, ada_ref, nw_ref)
    a = alb_ref[...]
    m = jnp.max(a, axis=0, keepdims=True)
    e = jnp.exp(a - m)
    lb = e[0:1] / jnp.sum(e, axis=0, keepdims=True)
    for c0 in range(0, w_ref.shape[1], COL_TILE):
        acc = _dot(h, w_ref[:, c0:c0 + COL_TILE])
        if c0 < A_QK:
            _put(q_ref, c0, _silu(acc))
        elif c0 < 2 * A_QK:
            c = c0 - A_QK
            f = lb[:, c:c + COL_TILE] + (1.0 - lb[:, c:c + COL_TILE]) * _sigmoid(acc + bf_ref[:, c:c + COL_TILE])
            _put(lf_ref, c, jnp.log(f))
            _put(k_ref, c, 1.0 - f)
        elif c0 < 2 * A_QK + A_V:
            _put(v_ref, c0 - 2 * A_QK, acc)
        else:
            _put(zs_ref, c0 - 2 * A_QK - A_V, _silu(acc))


def _l0_inproj(x, ada, nw, w_bf, alb, b_f, batch, seq, tm, act_dtype):
    n_tiles, tok_block, ada_block, shape, _ = _token_specs(batch, seq, tm)
    return pl.pallas_call(
        _l0_inproj_kernel,
        grid=(n_tiles,),
        in_specs=[tok_block(D_MODEL), ada_block, _resident((1, D_MODEL)), _resident(w_bf.shape),
                  _resident(alb.shape), _resident(b_f.shape)],
        out_specs=[tok_block(A_QK), tok_block(A_QK), tok_block(A_QK), tok_block(A_V), tok_block(A_V)],
        out_shape=[
            jax.ShapeDtypeStruct(shape(A_QK), F32),
            jax.ShapeDtypeStruct(shape(A_QK), F32),
            jax.ShapeDtypeStruct(shape(A_QK), F32),
            jax.ShapeDtypeStruct(shape(A_V), act_dtype),
            jax.ShapeDtypeStruct(shape(A_V), act_dtype),
        ],
        compiler_params=_params(("parallel",)),
        name="l0_inproj",
    )(x, ada, nw, w_bf, alb, b_f)


def _scan_levels(chunk):
    return [1 << l for l in range(int(math.log2(chunk)))]


def _exponent_matrix(chunk):
    r = np.arange(chunk)[:, None]
    c = np.arange(chunk)[None, :]
    blocks = [c <= r, c > r]
    for s in _scan_levels(chunk):
        bound = (r & ~(2 * s - 1)) | s
        upper = (r & s) != 0
        blocks.append(np.where(upper, (c > bound) & (c <= r), (c > r) & (c <= bound)))
    return np.concatenate(blocks, axis=0).astype(np.float32)


def _hgrn_chunk(q, k, lf, v, state, emat, chunk):
    dk = q.shape[-1]
    ex = jnp.exp(jnp.dot(emat, lf, preferred_element_type=F32, precision=lax.Precision.HIGHEST))
    e_b = ex[0:chunk]
    e_u = ex[chunk:2 * chunk]
    row = lax.broadcasted_iota(jnp.int32, (chunk, 1), 0)
    col = lax.broadcasted_iota(jnp.int32, (1, chunk), 1)

    att = jnp.where(row == col, jnp.sum(q * k, axis=-1, keepdims=True), 0.0)
    for l, s in enumerate(_scan_levels(chunk)):
        g = ex[(2 + l) * chunk:(3 + l) * chunk]
        upper = (row & s) != 0
        xb = (jnp.where(upper, q, k) * g).astype(BF16)
        part = _dot(xb, xb, _NT)
        keep = upper & ((col & s) == 0) & ((row >> (l + 1)) == (col >> (l + 1)))
        att = att + jnp.where(keep, part, 0.0)

    o = _dot(att.astype(BF16), v.astype(BF16)) + _dot((q * e_b).astype(BF16), state.astype(BF16))

    e_last = e_b[chunk - 1:chunk]
    eye = lax.broadcasted_iota(jnp.int32, (dk, dk), 0) == lax.broadcasted_iota(jnp.int32, (dk, dk), 1)
    e_col = jnp.sum(jnp.where(eye, e_last, 0.0), axis=-1, keepdims=True)
    new_state = state * e_col + _dot((k * e_u).astype(BF16), v.astype(BF16), _TN)
    return o, new_state


SAFE_DECAY = 60.0
SPLIT_PAD = 32


def _split_prefix_matrix(chunk):
    t = _exponent_matrix(chunk)[:2 * chunk]
    pad = (-3 * chunk) % SPLIT_PAD
    return np.concatenate([t, t, t, np.zeros((2 * chunk, pad), np.float32)], axis=1)


def _hgrn_chunk_bounded(q, k, lf, v, states, tmat, chunk):
    hi = lf.astype(BF16)
    r1 = lf - hi.astype(F32)
    mid = r1.astype(BF16)
    lo = r1 - mid.astype(F32)
    parts = [hi.astype(F32), mid.astype(F32), lo]
    pad = tmat.shape[1] - 3 * chunk
    if pad:
        parts.append(jnp.zeros((pad, lf.shape[1]), F32))
    sums = _dot(tmat, jnp.concatenate(parts, axis=0).astype(BF16))
    b, u = sums[:chunk], sums[chunk:]
    e_b = jnp.exp(b)
    qb = (q * e_b).astype(BF16)
    kn = (k * jnp.exp(-b)).astype(BF16)
    ku = (k * jnp.exp(u)).astype(BF16)
    vb = v.astype(BF16)
    row = lax.broadcasted_iota(jnp.int32, (chunk, chunk), 0)
    col = lax.broadcasted_iota(jnp.int32, (chunk, chunk), 1)
    eye = lax.broadcasted_iota(jnp.int32, (A_KDIM, A_KDIM), 0) == lax.broadcasted_iota(jnp.int32, (A_KDIM, A_KDIM), 1)
    ks = [slice(h * A_KDIM, (h + 1) * A_KDIM) for h in range(A_HEADS)]
    vs = [slice(h * A_VDIM, (h + 1) * A_VDIM) for h in range(A_HEADS)]
    atts = [jnp.where(row >= col, _dot(qb[:, ks[h]], kn[:, ks[h]], _NT), 0.0).astype(BF16)
            for h in range(A_HEADS)]
    outs = [_dot(atts[h], vb[:, vs[h]]) + _dot(qb[:, ks[h]], states[h].astype(BF16)) for h in range(A_HEADS)]
    new_states = []
    for h in range(A_HEADS):
        e_col = jnp.sum(jnp.where(eye, e_b[chunk - 1:chunk, ks[h]], 0.0), axis=-1, keepdims=True)
        new_states.append(states[h] * e_col + _dot(ku[:, ks[h]], vb[:, vs[h]], _TN))
    return outs, new_states


def _hgrn_scan_kernel(*refs, chunk, n_chunks, has_state, one_shot):
    if has_state:
        q_ref, k_ref, lf_ref, v_ref, zs_ref, gw_ref, e_ref, t_ref, s0_ref, og_ref, so_ref, s_scr = refs
    else:
        q_ref, k_ref, lf_ref, v_ref, zs_ref, gw_ref, e_ref, t_ref, og_ref, so_ref, s_scr = refs
    t = pl.program_id(1)

    if not one_shot:
        @pl.when(t == 0)
        def _():
            if has_state:
                s_scr[...] = s0_ref[0]
            else:
                s_scr[...] = jnp.zeros_like(s_scr)

    def load_state(h):
        return s0_ref[0, h] if one_shot else s_scr[h]

    def store_state(h, val):
        if one_shot:
            so_ref[0, h] = val
        else:
            s_scr[h] = val

    gw = gw_ref[...]

    def finish(rows, h, o):
        vc = slice(h * A_VDIM, (h + 1) * A_VDIM)
        ms = jnp.mean(o * o, axis=-1, keepdims=True)
        og_ref[rows, vc] = (o * lax.rsqrt(ms + EPS) * gw * zs_ref[rows, vc]).astype(og_ref.dtype)

    def chunk_rows(c):
        return slice(0, chunk) if n_chunks == 1 else pl.ds(pl.multiple_of(c * chunk, chunk), chunk)

    def bounded_body(c, carry):
        rows = chunk_rows(c)
        outs, new_states = _hgrn_chunk_bounded(q_ref[rows, :], k_ref[rows, :], lf_ref[rows, :], v_ref[rows, :],
                                               [load_state(h) for h in range(A_HEADS)], t_ref[...], chunk)
        for h in range(A_HEADS):
            store_state(h, new_states[h])
            finish(rows, h, outs[h])
        return carry

    def general_body(c, carry):
        rows = chunk_rows(c)
        for h in range(A_HEADS):
            kc = slice(h * A_KDIM, (h + 1) * A_KDIM)
            vc = slice(h * A_VDIM, (h + 1) * A_VDIM)
            o, s_new = _hgrn_chunk(q_ref[rows, kc], k_ref[rows, kc], lf_ref[rows, kc],
                                   v_ref[rows, vc], load_state(h), e_ref[...], chunk)
            store_state(h, s_new)
            finish(rows, h, o)
        return carry

    def run(body):
        def go():
            if n_chunks == 1:
                body(0, 0)
            else:
                lax.fori_loop(0, n_chunks, body, 0)
        return go

    bounded = jnp.min(lf_ref[...]) * chunk >= -SAFE_DECAY
    lax.cond(bounded, run(bounded_body), run(general_body))

    if not one_shot:
        @pl.when(t == pl.num_programs(1) - 1)
        def _():
            so_ref[0] = s_scr[...]


def _hgrn_scan(q, k, lf, v, zs, gw, s0, batch, seq, tile, chunk, out_dtype):
    n_t = seq // tile
    emat = jnp.asarray(_exponent_matrix(chunk))
    tmat = jnp.asarray(_split_prefix_matrix(chunk), dtype=BF16)
    row_block = lambda w: pl.BlockSpec((tile, w), lambda b, t: (b * n_t + t, 0))
    state_block = pl.BlockSpec((1, A_HEADS, A_KDIM, A_VDIM), lambda b, t: (b, 0, 0, 0))
    in_specs = [row_block(A_QK), row_block(A_QK), row_block(A_QK), row_block(A_V), row_block(A_V),
                pl.BlockSpec((1, A_VDIM), lambda b, t: (0, 0)),
                pl.BlockSpec(emat.shape, lambda b, t: (0, 0)),
                pl.BlockSpec(tmat.shape, lambda b, t: (0, 0))]
    args = [q, k, lf, v, zs, gw, emat, tmat]
    if s0 is not None:
        in_specs.append(state_block)
        args.append(s0)
    kern = functools.partial(_hgrn_scan_kernel, chunk=chunk, n_chunks=tile // chunk,
                             has_state=s0 is not None, one_shot=s0 is not None and seq == chunk)
    return pl.pallas_call(
        kern,
        grid=(batch, n_t),
        in_specs=in_specs,
        out_specs=[row_block(A_V), state_block],
        out_shape=[jax.ShapeDtypeStruct((batch * seq, A_V), out_dtype),
                   jax.ShapeDtypeStruct((batch, A_HEADS, A_KDIM, A_VDIM), F32)],
        scratch_shapes=[pltpu.VMEM((A_HEADS, A_KDIM, A_VDIM), F32)],
        compiler_params=_params(("parallel", "arbitrary")),
        name="hgrn_scan",
    )(*args)


def _outproj_kernel(*refs):
    *g_refs, w_ref, x_ref, ada_ref, o_ref = refs
    gs = [g_ref[...] for g_ref in g_refs]
    gs = [g.reshape(-1, g.shape[-1]).astype(BF16) for g in gs]
    ada = ada_ref[...]
    if len(x_ref.shape) == 2:
        ada = ada.reshape(1, ada.shape[-1])
    for c0 in range(0, D_MODEL, COL_TILE):
        cols = (Ellipsis, slice(c0, c0 + COL_TILE))
        y, k0 = None, 0
        for g in gs:
            part = _dot(g, w_ref[k0:k0 + g.shape[1], c0:c0 + COL_TILE])
            y = part if y is None else y + part
            k0 += g.shape[1]
        gate = ada[..., 2 * D_MODEL + c0:2 * D_MODEL + c0 + COL_TILE]
        o_ref[cols] = x_ref[cols] + gate * y.reshape(x_ref.shape[:-1] + (COL_TILE,))


def _outproj(gs, w_bf, x, ada, batch, seq, tm):
    n_tiles, tok_block, ada_block, shape, _ = _token_specs(batch, seq, tm)
    assert sum(g.shape[-1] for g in gs) == w_bf.shape[0]
    return pl.pallas_call(
        _outproj_kernel,
        grid=(n_tiles,),
        in_specs=[tok_block(g.shape[-1]) for g in gs] + [_resident(w_bf.shape), tok_block(D_MODEL), ada_block],
        out_specs=tok_block(D_MODEL),
        out_shape=jax.ShapeDtypeStruct(shape(D_MODEL), F32),
        compiler_params=_params(("parallel",)),
        name="outproj",
    )(*gs, w_bf, x, ada)


def _rope_table_kernel(pos_ref, inv_ref, cos_ref, sin_ref):
    ang = pos_ref[...] * inv_ref[...]
    lane = lax.broadcasted_iota(jnp.int32, ang.shape, 1)
    cos_ref[...] = jnp.cos(ang)
    sin_ref[...] = jnp.where(lane < B_HD // 2, -jnp.sin(ang), jnp.sin(ang))


def _rope_tables(start, seq):
    half = B_HD // 2
    inv = ROPE_THETA ** (-jnp.arange(half, dtype=F32) / half)
    inv2 = jnp.concatenate([inv, inv]).reshape(1, B_HD)
    pos = (start + jnp.arange(seq, dtype=jnp.int32)).astype(F32).reshape(seq, 1)
    return pl.pallas_call(
        _rope_table_kernel,
        out_shape=[jax.ShapeDtypeStruct((seq, B_HD), F32)] * 2,
        name="rope_tables",
    )(pos, inv2)


def _l1_inproj_kernel(x_ref, ada_ref, nw_ref, w_ref, qn_ref, kn_ref, cos_ref, sin_ref, q_ref, kv_ref, zs_ref,
                      *row_refs, row_tokens):
    h = _modulated_norm(x_ref, ada_ref, nw_ref)

    def put_rows(g, is_v, tile):
        n = row_tokens[g]
        for hd in range(B_HG):
            row_refs[g][pl.ds(is_v * B_HG + hd, n, stride=KV_ROWS), :] = (
                tile[tile.shape[0] - n:, hd * B_HD:(hd + 1) * B_HD])

    tm = h.shape[0]
    cos = cos_ref[...]
    sin = sin_ref[...]
    reps = tm // cos.shape[0]
    if reps > 1:
        cos = jnp.broadcast_to(cos[None], (reps,) + cos.shape).reshape(tm, B_HD)
        sin = jnp.broadcast_to(sin[None], (reps,) + sin.shape).reshape(tm, B_HD)

    def norm_rope(acc, nw):
        outs = []
        for hd in range(acc.shape[1] // B_HD):
            xh = acc[:, hd * B_HD:(hd + 1) * B_HD]
            ms = jnp.mean(xh * xh, axis=-1, keepdims=True)
            y = xh * lax.rsqrt(ms + EPS) * nw
            outs.append(y * cos + pltpu.roll(y, B_HD // 2, 1) * sin)
        return jnp.concatenate(outs, axis=-1)

    for c0 in range(0, w_ref.shape[1], B_GW):
        acc = _dot(h, w_ref[:, c0:c0 + B_GW])
        seg, g = divmod(c0 // B_GW, len(B_GROUPS))
        if seg == 0:
            _put(q_ref, g * B_GW, norm_rope(acc, qn_ref[...]))
        elif seg == 1:
            kt = norm_rope(acc, kn_ref[...])
            _put(kv_ref, 2 * g * B_GW, kt)
            if row_refs:
                put_rows(g, 0, kt)
        elif seg == 2:
            _put(kv_ref, (2 * g + 1) * B_GW, acc)
            if row_refs:
                put_rows(g, 1, acc)
        else:
            _put(zs_ref, g * B_GW, _silu(acc))


def _l1_inproj(x, ada, nw, w_bf, qn, kn, cos, sin, batch, seq, tm, act_dtype):
    n_tiles, tok_block, ada_block, shape, per = _token_specs(batch, seq, tm)
    out_specs = [tok_block(B_WIDTH), tok_block(2 * B_WIDTH), tok_block(B_WIDTH)]
    out_shape = [jax.ShapeDtypeStruct(shape(B_WIDTH), F32),
                 jax.ShapeDtypeStruct(shape(2 * B_WIDTH), F32),
                 jax.ShapeDtypeStruct(shape(B_WIDTH), act_dtype)]
    row_tokens = ()
    if per is None:
        tab_block = _resident((seq, B_HD))
    else:
        tab_block = pl.BlockSpec((tm, B_HD), lambda i: (i % per, 0))
        for window, _ in B_GROUPS:
            keep = min(window, seq)
            blk = min(keep, tm)
            n_blk = keep // blk
            first = per - n_blk
            out_specs.append(pl.BlockSpec(
                (blk * KV_ROWS, B_HD),
                lambda i, n_blk=n_blk, first=first: ((i // per) * n_blk + jnp.clip(i % per - first, 0, n_blk - 1), 0)))
            out_shape.append(jax.ShapeDtypeStruct((batch * keep * KV_ROWS, B_HD), F32))
            row_tokens += (blk,)
    return pl.pallas_call(
        functools.partial(_l1_inproj_kernel, row_tokens=row_tokens),
        grid=(n_tiles,),
        in_specs=[tok_block(D_MODEL), ada_block, _resident((1, D_MODEL)), _resident(w_bf.shape),
                  _resident((1, B_HD)), _resident((1, B_HD)), tab_block, tab_block],
        out_specs=out_specs,
        out_shape=out_shape,
        compiler_params=_params(("arbitrary",)),
        name="l1_inproj",
    )(x, ada, nw, w_bf, qn, kn, cos, sin)


Q_SUB = 128


def _softmax_parts(s, valid):
    s = jnp.where(valid, s, -jnp.inf)
    m = jnp.max(s, axis=-1, keepdims=True)
    p = jnp.exp(s - m)
    return m, p, jnp.sum(p, axis=-1, keepdims=True)


def _prompt_attn_kernel(q_ref, kc_ref, vc_ref, kp_ref, vp_ref, o_ref, lse_ref, *, tq, dil):
    first_key = jnp.where(pl.program_id(2) == 0, Q_SUB, 0)
    row = lax.broadcasted_iota(jnp.int32, (Q_SUB, 2 * Q_SUB), 0)
    col = lax.broadcasted_iota(jnp.int32, (Q_SUB, 2 * Q_SUB), 1)
    band = (col >= row) & (col <= row + Q_SUB)
    scale = B_HD ** -0.5

    def rows_of(r, first, n):
        start = r + first * dil
        return pl.ds(start, n) if dil == 1 else pl.ds(start, n, stride=dil)

    for r in range(dil):
        for sb in range(tq // Q_SUB):
            rows = rows_of(r, sb * Q_SUB, Q_SUB)
            if sb == 0:
                prev = rows_of(r, 0, Q_SUB)
                k_prev, v_prev = kp_ref[0, prev, :], vp_ref[0, prev, :]
                valid = band & (col >= first_key)
            else:
                prev = rows_of(r, (sb - 1) * Q_SUB, Q_SUB)
                k_prev, v_prev = kc_ref[0, prev, :], vc_ref[0, prev, :]
                valid = band
            k2 = jnp.concatenate([k_prev, kc_ref[0, rows, :]], axis=0).astype(BF16)
            v2 = jnp.concatenate([v_prev, vc_ref[0, rows, :]], axis=0).astype(BF16)
            s = _dot(q_ref[0, rows, :].astype(BF16), k2, _NT) * scale
            m, p, l = _softmax_parts(s, valid)
            o_ref[0, rows, :] = _dot(p.astype(BF16), v2) / l
            lse_ref[0, rows, :] = jnp.broadcast_to(m + jnp.log(l), (Q_SUB, B_HD))


def _prompt_attn_group(q, kv, g, dil, batch, seq):
    tq = min(512, seq // dil)
    tile = tq * dil
    back = Q_SUB * dil
    n_tiles = seq // tile
    q0, k0, v0 = g * B_HG, 2 * g * B_HG, (2 * g + 1) * B_HG
    cur = lambda c0: pl.BlockSpec((1, tile, B_HD), lambda b, h, i: (b, i, c0 + h))
    prev = lambda c0: pl.BlockSpec(
        (1, back, B_HD), lambda b, h, i: (b, jnp.maximum(i * (tile // back) - 1, 0), c0 + h))
    out = pl.BlockSpec((1, tile, B_HD), lambda b, h, i: (b, i, h))
    return pl.pallas_call(
        functools.partial(_prompt_attn_kernel, tq=tq, dil=dil),
        grid=(batch, B_HG, n_tiles),
        in_specs=[cur(q0), cur(k0), cur(v0), prev(k0), prev(v0)],
        out_specs=[out, out],
        out_shape=[jax.ShapeDtypeStruct((batch, seq, B_GW), F32)] * 2,
        compiler_params=_params(("parallel", "parallel", "arbitrary")),
        name=f"prompt_attn_g{g}",
    )(q, kv, kv, kv, kv)


def _group_mix_kernel(o0, o1, o2, l0, l1, l2, zs_ref, out_ref):
    lses = [l0[...], l1[...], l2[...]]
    m = jnp.maximum(jnp.maximum(lses[0], lses[1]), lses[2])
    es = [jnp.exp(l - m) for l in lses]
    inv = 1.0 / (es[0] + es[1] + es[2])
    for g, o in enumerate((o0, o1, o2)):
        cols = slice(g * B_GW, (g + 1) * B_GW)
        out_ref[:, cols] = (o[...] * (es[g] * inv) * zs_ref[:, cols]).astype(out_ref.dtype)


def _group_mix(outs, lses, zs, tm):
    n = zs.shape[0]
    blk = pl.BlockSpec((tm, B_GW), lambda i: (i, 0))
    wide = pl.BlockSpec((tm, B_WIDTH), lambda i: (i, 0))
    return pl.pallas_call(
        _group_mix_kernel,
        grid=(n // tm,),
        in_specs=[blk] * 6 + [wide],
        out_specs=wide,
        out_shape=jax.ShapeDtypeStruct((n, B_WIDTH), BF16),
        compiler_params=_params(("parallel",)),
        name="group_mix",
    )(*outs, *lses, zs)


ATTN_TILE = 2048
ATTN_BATCH = 8
MIX_ROWS = 256


def _attend(qs, k2s, v2s, valids):
    scale = B_HD ** -0.5
    ss = [jnp.where(ok, _dot(q, k2, _NT) * scale, -jnp.inf) for q, k2, ok in zip(qs, k2s, valids)]
    ms = [jnp.max(s, axis=-1, keepdims=True) for s in ss]
    ps = [jnp.exp(s - m) for s, m in zip(ss, ms)]
    ls = [jnp.sum(p, axis=-1, keepdims=True) for p in ps]
    os = [_dot(p.astype(BF16), v2) * (1.0 / l) for p, v2, l in zip(ps, v2s, ls)]
    lses = [m + jnp.log(l) for m, l in zip(ms, ls)]
    return os, lses


def _prompt_attn_kernel_fused(*refs):
    n_g = len(B_GROUPS)
    ins, zs_refs = refs[:5 * n_g], refs[5 * n_g:6 * n_g]
    out_refs, (o_scr, lse_scr) = refs[6 * n_g:7 * n_g], refs[7 * n_g:]
    first_key = jnp.where(pl.program_id(2) == 0, Q_SUB, 0)
    row = lax.broadcasted_iota(jnp.int32, (Q_SUB, 2 * Q_SUB), 0)
    col = lax.broadcasted_iota(jnp.int32, (Q_SUB, 2 * Q_SUB), 1)
    band = (col >= row) & (col <= row + Q_SUB)
    band_first = band & (col >= first_key)

    for g, (_, dil) in enumerate(B_GROUPS):
        q_ref, kc_ref, vc_ref, kp_ref, vp_ref = ins[5 * g:5 * g + 5]
        n_sb = ATTN_TILE // (Q_SUB * dil)

        def rows_of(r, first, n):
            start = r + first * dil
            return pl.ds(start, n) if dil == 1 else pl.ds(start, n, stride=dil)

        work = []
        for r in range(dil):
            cls = rows_of(r, 0, n_sb * Q_SUB)
            kr = jnp.concatenate([kp_ref[0, rows_of(r, 0, Q_SUB), :], kc_ref[0, cls, :]], axis=0).astype(BF16)
            vr = jnp.concatenate([vp_ref[0, rows_of(r, 0, Q_SUB), :], vc_ref[0, cls, :]], axis=0).astype(BF16)
            qr = q_ref[0, cls, :].astype(BF16)
            for sb in range(n_sb):
                work.append((rows_of(r, sb * Q_SUB, Q_SUB), qr[sb * Q_SUB:(sb + 1) * Q_SUB],
                             kr[sb * Q_SUB:(sb + 2) * Q_SUB], vr[sb * Q_SUB:(sb + 2) * Q_SUB],
                             band_first if sb == 0 else band))
        for i in range(0, len(work), ATTN_BATCH):
            part = work[i:i + ATTN_BATCH]
            os, lses = _attend([w[1] for w in part], [w[2] for w in part], [w[3] for w in part],
                               [w[4] for w in part])
            for w, o, lse in zip(part, os, lses):
                o_scr[g, w[0], :] = o
                lse_scr[g, w[0], :] = jnp.broadcast_to(lse, (Q_SUB, B_HD))

    for c in range(ATTN_TILE // MIX_ROWS):
        rows = slice(c * MIX_ROWS, (c + 1) * MIX_ROWS)
        lses = [lse_scr[g, rows, :] for g in range(n_g)]
        m = functools.reduce(jnp.maximum, lses)
        es = [jnp.exp(l - m) for l in lses]
        inv = 1.0 / functools.reduce(lambda a, b: a + b, es)
        for g in range(n_g):
            out_refs[g][0, rows, :] = (o_scr[g, rows, :] * (es[g] * inv) * zs_refs[g][0, rows, :]
                                       ).astype(out_refs[g].dtype)


def _prompt_attn(q, kv, zs, batch, seq):
    assert seq % ATTN_TILE == 0
    n_tiles = seq // ATTN_TILE
    tok = lambda c0: pl.BlockSpec((1, ATTN_TILE, B_HD), lambda b, h, i: (b, i, c0 + h))
    in_specs, args = [], []
    for g, (_, dil) in enumerate(B_GROUPS):
        back = Q_SUB * dil
        assert ATTN_TILE % back == 0
        k0, v0 = 2 * g * B_HG, (2 * g + 1) * B_HG
        prev = lambda c0, back=back: pl.BlockSpec(
            (1, back, B_HD), lambda b, h, i: (b, jnp.maximum(i * (ATTN_TILE // back) - 1, 0), c0 + h))
        in_specs += [tok(g * B_HG), tok(k0), tok(v0), prev(k0), prev(v0)]
        args += [q, kv, kv, kv, kv]
    in_specs += [tok(g * B_HG) for g in range(len(B_GROUPS))]
    args += [zs] * len(B_GROUPS)
    out = pl.BlockSpec((1, ATTN_TILE, B_HD), lambda b, h, i: (b, i, h))
    return pl.pallas_call(
        _prompt_attn_kernel_fused,
        grid=(batch, B_HG, n_tiles),
        in_specs=in_specs,
        out_specs=[out] * len(B_GROUPS),
        out_shape=[jax.ShapeDtypeStruct((batch, seq, B_GW), BF16)] * len(B_GROUPS),
        scratch_shapes=[pltpu.VMEM((len(B_GROUPS), ATTN_TILE, B_HD), F32)] * 2,
        compiler_params=_params(("parallel", "parallel", "arbitrary")),
        name="prompt_attn",
    )(*args)


KV_ROWS = 2 * B_HG


def _sample_masks(seq):
    hq = np.arange(B_HG * seq)[:, None] // seq
    iq = np.arange(B_HG * seq)[:, None] % seq
    masks = []
    for window, dil in B_GROUPS:
        n_tok = window if dil < seq else (window // dil) * seq
        col = np.arange(n_tok * KV_ROWS)[None, :]
        tok, is_v, head = col // KV_ROWS, (col // B_HG) % 2, col % B_HG
        if dil < seq:
            ok = (tok >= iq) & ((tok - iq) % dil == 0)
        else:
            ok = (tok % seq) == iq
        masks.append(np.where(ok & (is_v == 0) & (head == hq), 0.0, -np.inf).astype(np.float32))
    col = np.arange(B_HG * seq)[None, :]
    new = []
    for _, dil in B_GROUPS:
        ok = (col // seq == hq) & (col % seq <= iq) & ((iq - col % seq) % dil == 0)
        new.append(np.where(ok, 0.0, -np.inf).astype(np.float32))
    return masks, np.stack(new)


def _sample_attn_kernel(q_ref, kv_ref, zs_ref, c0_ref, c1_ref, c2_ref, m0_ref, m1_ref, m2_ref, mn_ref,
                        out_ref, *, seq):
    scale = B_HD ** -0.5
    caches = (c0_ref, c1_ref, c2_ref)
    masks = (m0_ref, m1_ref, m2_ref)
    heads = lambda ref, c0: jnp.concatenate(
        [ref[0, :, c0 + h * B_HD:c0 + (h + 1) * B_HD] for h in range(B_HG)], axis=0).astype(BF16)
    groups = range(len(B_GROUPS))
    qa = [heads(q_ref, g * B_GW) for g in groups]
    k_new = [heads(kv_ref, 2 * g * B_GW) for g in groups]
    v_new = [heads(kv_ref, (2 * g + 1) * B_GW) for g in groups]
    rows = [caches[g][0].reshape(-1, B_HD).astype(BF16) for g in groups]
    s_c = [_dot(qa[g], rows[g], _NT) * scale + masks[g][...] for g in groups]
    s_n = [_dot(qa[g], k_new[g], _NT) * scale + mn_ref[g] for g in groups]
    ms = [jnp.maximum(jnp.max(s_c[g], axis=-1, keepdims=True), jnp.max(s_n[g], axis=-1, keepdims=True))
          for g in groups]
    p_c = [jnp.exp(s_c[g] - ms[g]) for g in groups]
    p_n = [jnp.exp(s_n[g] - ms[g]) for g in groups]
    ls = [jnp.sum(p_c[g], axis=-1, keepdims=True) + jnp.sum(p_n[g], axis=-1, keepdims=True) for g in groups]
    p_v = [pltpu.roll(p_c[g], B_HG, 1).astype(BF16) for g in groups]
    accs = [_dot(p_v[g], rows[g]) + _dot(p_n[g].astype(BF16), v_new[g]) for g in groups]
    outs = [accs[g] / ls[g] for g in groups]
    lses = [ms[g] + jnp.log(ls[g]) for g in groups]
    m = functools.reduce(jnp.maximum, lses)
    es = [jnp.exp(l - m) for l in lses]
    inv = 1.0 / (es[0] + es[1] + es[2])
    for g in range(len(B_GROUPS)):
        mixed = outs[g] * (es[g] * inv)
        for h in range(B_HG):
            cols = slice(g * B_GW + h * B_HD, g * B_GW + (h + 1) * B_HD)
            out_ref[0, :, cols] = mixed[h * seq:(h + 1) * seq] * zs_ref[0, :, cols]


def _sample_attn(q, kv, zs, caches, batch, seq):
    masks, mask_new = _sample_masks(seq)
    specs, views = [], []
    for (window, dil), c in zip(B_GROUPS, caches):
        assert window == dil * (N_KEYS - 1) and c.shape[2] == window
        if dil >= seq:
            views.append(c.reshape(batch, window // dil, dil * KV_ROWS, B_HD))
            specs.append(pl.BlockSpec((1, window // dil, seq * KV_ROWS, B_HD), lambda b: (b, 0, 0, 0)))
        else:
            assert seq % dil == 0
            views.append(c.reshape(batch, window * KV_ROWS, B_HD))
            specs.append(pl.BlockSpec((1, window * KV_ROWS, B_HD), lambda b: (b, 0, 0)))
    const = lambda a: pl.BlockSpec(a.shape, lambda b: (0,) * a.ndim)
    tok = lambda w: pl.BlockSpec((1, seq, w), lambda b: (b, 0, 0))
    return pl.pallas_call(
        functools.partial(_sample_attn_kernel, seq=seq),
        grid=(batch,),
        in_specs=[tok(B_WIDTH), tok(2 * B_WIDTH), tok(B_WIDTH)] + specs
                 + [const(a) for a in masks] + [const(mask_new)],
        out_specs=tok(B_WIDTH),
        out_shape=jax.ShapeDtypeStruct((batch, seq, B_WIDTH), F32),
        compiler_params=_params(("parallel",)),
        name="sample_attn",
    )(q, kv, zs, *views, *[jnp.asarray(a) for a in masks], jnp.asarray(mask_new))


def _trunk(x, ada, start, state, caches, weights, tm_in, tm_out, scan_tile, scan_chunk):
    (norm_w, alb, a_w_in, a_b_f, a_gw, a_w_out, b_w_in, b_qn, b_kn, b_w_out) = weights
    batch, seq, _ = x.shape
    flat = seq >= tm_in
    act = BF16 if flat else F32
    tok = (lambda a: a.reshape(batch * seq, a.shape[-1])) if flat else (lambda a: a)
    rows = lambda a: a.reshape(batch * seq, a.shape[-1])
    ada0 = ada[0].reshape(batch, 1, 3 * D_MODEL)
    ada1 = ada[1].reshape(batch, 1, 3 * D_MODEL)

    x0 = tok(x)
    q, k, lf, v, zs = _l0_inproj(x0, ada0, norm_w[0:1], a_w_in, alb, a_b_f, batch, seq, tm_in, act)
    og, s_new = _hgrn_scan(rows(q), rows(k), rows(lf), rows(v), rows(zs), a_gw, state,
                           batch, seq, scan_tile, scan_chunk, act)
    og = og if flat else og.reshape(batch, seq, A_V)
    x1 = _outproj([og], a_w_out, x0, ada0, batch, seq, tm_out)

    cos, sin = _rope_tables(start, seq)
    qr, kv, zs1, *new_rows = _l1_inproj(x1, ada1, norm_w[1:2], b_w_in, b_qn, b_kn, cos, sin, batch, seq, tm_in, act)
    kv3 = kv.reshape(batch, seq, 2 * B_WIDTH)
    if caches is None:
        og1 = _prompt_attn(qr.reshape(batch, seq, B_WIDTH), kv3, zs1.reshape(batch, seq, B_WIDTH), batch, seq)
        og1 = [o.reshape(batch * seq, B_GW) for o in og1]
    else:
        og1 = [_sample_attn(qr, kv, zs1, caches, batch, seq)]
    x2 = _outproj(og1, b_w_out, x1, ada1, batch, seq, tm_out)

    if new_rows:
        kv_rows = [r.reshape(batch, min(window, seq), 2, B_HG, B_HD) for r, (window, _) in zip(new_rows, B_GROUPS)]
    else:
        kv_rows = [kv3[:, seq - min(window, seq):, 2 * g * B_GW:2 * (g + 1) * B_GW]
                   .reshape(batch, min(window, seq), 2, B_HG, B_HD) for g, (window, _) in enumerate(B_GROUPS)]
    return x2.reshape(batch, seq, D_MODEL), s_new, kv_rows


def kernel(x_prompt, x_sample, state_hgrn, cache_kv_w128, cache_kv_w512, cache_kv_w2048, c_prompt, c_sample,
           norm_w, ada_w, ada_b, a_lower_bounds, a_w_in, a_b_f, a_g_norm_w, a_w_out, b_w_in, b_q_norm_w,
           b_k_norm_w, b_w_out):
    bp, lp, _ = x_prompt.shape
    bs, ls, _ = x_sample.shape

    pad = (-(bp + bs)) % 8
    c_all = jnp.concatenate([c_prompt, c_sample, jnp.zeros((pad, D_MODEL), F32)], axis=0)
    ada = _ada_vectors(c_all, ada_w, ada_b)
    ada_p, ada_s = ada[:, :bp], ada[:, bp:bp + bs]

    weights = (norm_w, a_lower_bounds, a_w_in[0].astype(BF16), a_b_f, a_g_norm_w, a_w_out[0].astype(BF16),
               b_w_in[0].astype(BF16), b_q_norm_w, b_k_norm_w, b_w_out[0].astype(BF16))

    y_p, s_p, kv_p = _trunk(x_prompt, ada_p, 0, None, None, weights,
                            tm_in=256, tm_out=512, scan_tile=256, scan_chunk=64)
    caches = (cache_kv_w128, cache_kv_w512, cache_kv_w2048)
    y_s, s_s, kv_s = _trunk(x_sample, ada_s, PAST_LEN, state_hgrn[0], caches, weights,
                            tm_in=256, tm_out=256, scan_tile=ls, scan_chunk=ls)

    kv_out = []
    for g in range(len(B_GROUPS)):
        kv_out.append(kv_p[g][None])
        kv_out.append(kv_s[g][None])
    return (y_p, y_s, s_p[None], s_s[None], *kv_out)
```

```python
import functools
import math

import numpy as np
import jax
import jax.numpy as jnp
from jax import lax
from jax.experimental import pallas as pl
from jax.experimental.pallas import tpu as pltpu

F32 = jnp.float32
BF16 = jnp.bfloat16

D_MODEL = 1024
EPS = 1e-6
A_HEADS = 8
A_KDIM = 128
A_VDIM = 256
A_QK = A_HEADS * A_KDIM
A_V = A_HEADS * A_VDIM
B_GROUPS = ((128, 1), (512, 4), (2048, 16))
B_HG = 4
B_HD = 128
B_GW = B_HG * B_HD
B_WIDTH = len(B_GROUPS) * B_GW
ROPE_THETA = 10000.0
PAST_LEN = 2048
N_KEYS = 129

COL_TILE = 512
VMEM_LIMIT = 56 * 1024 * 1024

_NT = (((1,), (1,)), ((), ()))
_TN = (((0,), (0,)), ((), ()))


def _sigmoid(x):
    return 1.0 / (1.0 + jnp.exp(-x))


def _silu(x):
    h = 0.5 * x
    return h * jnp.tanh(h) + h


def _dot(a, b, dims=None):
    if dims is None:
        return jnp.dot(a, b, preferred_element_type=F32)
    return lax.dot_general(a, b, dims, preferred_element_type=F32)


def _params(sem):
    return pltpu.CompilerParams(dimension_semantics=sem, vmem_limit_bytes=VMEM_LIMIT)


def _ada_kernel(c_ref, w_ref, b_ref, o_ref):
    a = _silu(c_ref[...]).astype(BF16)
    o_ref[0] = _dot(a, w_ref[0].astype(BF16)) + b_ref[0]


def _ada_vectors(c_all, ada_w, ada_b):
    n_layers, _, width = ada_w.shape
    rows = c_all.shape[0]
    return pl.pallas_call(
        _ada_kernel,
        grid=(n_layers, width // COL_TILE),
        in_specs=[
            pl.BlockSpec((rows, D_MODEL), lambda l, j: (0, 0)),
            pl.BlockSpec((1, D_MODEL, COL_TILE), lambda l, j: (l, 0, j)),
            pl.BlockSpec((1, 1, COL_TILE), lambda l, j: (l, 0, j)),
        ],
        out_specs=pl.BlockSpec((1, rows, COL_TILE), lambda l, j: (l, 0, j)),
        out_shape=jax.ShapeDtypeStruct((n_layers, rows, width), F32),
        compiler_params=_params(("parallel", "parallel")),
        name="ada_vectors",
    )(c_all, ada_w, ada_b.reshape(n_layers, 1, width))


def _modulated_norm(x_ref, ada_ref, nw_ref):
    x = x_ref[...]
    ada = ada_ref[...]
    if x.ndim == 2:
        ada = ada.reshape(1, ada.shape[-1])
    shift = ada[..., :D_MODEL]
    scale = ada[..., D_MODEL:2 * D_MODEL]
    ms = jnp.mean(x * x, axis=-1, keepdims=True)
    y = x * lax.rsqrt(ms + EPS) * nw_ref[...].reshape((1,) * (x.ndim - 1) + (D_MODEL,))
    h = y * (1.0 + scale) + shift
    return h.reshape(-1, D_MODEL).astype(BF16)


def _token_specs(batch, seq, tm):
    if seq >= tm:
        assert seq % tm == 0
        per = seq // tm
        n_tiles = batch * per
        tok_block = lambda w: pl.BlockSpec((tm, w), lambda i: (i, 0))
        ada_block = pl.BlockSpec((1, 1, 3 * D_MODEL), lambda i: (i // per, 0, 0))
        shape = lambda w: (batch * seq, w)
        return n_tiles, tok_block, ada_block, shape, per
    tb = tm // seq
    assert batch % tb == 0
    n_tiles = batch // tb
    tok_block = lambda w: pl.BlockSpec((tb, seq, w), lambda i: (i, 0, 0))
    ada_block = pl.BlockSpec((tb, 1, 3 * D_MODEL), lambda i: (i, 0, 0))
    shape = lambda w: (batch, seq, w)
    return n_tiles, tok_block, ada_block, shape, None


def _resident(shape):
    return pl.BlockSpec(shape, lambda i: (0,) * len(shape), pipeline_mode=pl.Buffered(1))


def _put(ref, c0, val):
    ref[(Ellipsis, slice(c0, c0 + val.shape[-1]))] = val.reshape(ref.shape[:-1] + val.shape[-1:]).astype(ref.dtype)


def _l0_inproj_kernel(x_ref, ada_ref, nw_ref, w_ref, alb_ref, bf_ref, q_ref, k_ref, lf_ref, v_ref, zs_ref):
    h = _modulated_norm(x_ref, ada_ref, nw_ref)
    a = alb_ref[...]
    m = jnp.max(a, axis=0, keepdims=True)
    e = jnp.exp(a - m)
    lb = e[0:1] / jnp.sum(e, axis=0, keepdims=True)
    for c0 in range(0, w_ref.shape[1], COL_TILE):
        acc = _dot(h, w_ref[:, c0:c0 + COL_TILE])
        if c0 < A_QK:
            _put(q_ref, c0, _silu(acc))
        elif c0 < 2 * A_QK:
            c = c0 - A_QK
            f = lb[:, c:c + COL_TILE] + (1.0 - lb[:, c:c + COL_TILE]) * _sigmoid(acc + bf_ref[:, c:c + COL_TILE])
            _put(lf_ref, c, jnp.log(f))
            _put(k_ref, c, 1.0 - f)
        elif c0 < 2 * A_QK + A_V:
            _put(v_ref, c0 - 2 * A_QK, acc)
        else:
            _put(zs_ref, c0 - 2 * A_QK - A_V, _silu(acc))


def _l0_inproj(x, ada, nw, w_bf, alb, b_f, batch, seq, tm, act_dtype):
    n_tiles, tok_block, ada_block, shape, _ = _token_specs(batch, seq, tm)
    return pl.pallas_call(
        _l0_inproj_kernel,
        grid=(n_tiles,),
        in_specs=[tok_block(D_MODEL), ada_block, _resident((1, D_MODEL)), _resident(w_bf.shape),
                  _resident(alb.shape), _resident(b_f.shape)],
        out_specs=[tok_block(A_QK), tok_block(A_QK), tok_block(A_QK), tok_block(A_V), tok_block(A_V)],
        out_shape=[
            jax.ShapeDtypeStruct(shape(A_QK), F32),
            jax.ShapeDtypeStruct(shape(A_QK), F32),
            jax.ShapeDtypeStruct(shape(A_QK), F32),
            jax.ShapeDtypeStruct(shape(A_V), act_dtype),
            jax.ShapeDtypeStruct(shape(A_V), act_dtype),
        ],
        compiler_params=_params(("parallel",)),
        name="l0_inproj",
    )(x, ada, nw, w_bf, alb, b_f)


def _scan_levels(chunk):
    return [1 << l for l in range(int(math.log2(chunk)))]


def _exponent_matrix(chunk):
    r = np.arange(chunk)[:, None]
    c = np.arange(chunk)[None, :]
    blocks = [c <= r, c > r]
    for s in _scan_levels(chunk):
        bound = (r & ~(2 * s - 1)) | s
        upper = (r & s) != 0
        blocks.append(np.where(upper, (c > bound) & (c <= r), (c > r) & (c <= bound)))
    return np.concatenate(blocks, axis=0).astype(np.float32)


def _hgrn_chunk(q, k, lf, v, state, emat, chunk):
    dk = q.shape[-1]
    ex = jnp.exp(jnp.dot(emat, lf, preferred_element_type=F32, precision=lax.Precision.HIGHEST))
    e_b = ex[0:chunk]
    e_u = ex[chunk:2 * chunk]
    row = lax.broadcasted_iota(jnp.int32, (chunk, 1), 0)
    col = lax.broadcasted_iota(jnp.int32, (1, chunk), 1)

    att = jnp.where(row == col, jnp.sum(q * k, axis=-1, keepdims=True), 0.0)
    for l, s in enumerate(_scan_levels(chunk)):
        g = ex[(2 + l) * chunk:(3 + l) * chunk]
        upper = (row & s) != 0
        xb = (jnp.where(upper, q, k) * g).astype(BF16)
        part = _dot(xb, xb, _NT)
        keep = upper & ((col & s) == 0) & ((row >> (l + 1)) == (col >> (l + 1)))
        att = att + jnp.where(keep, part, 0.0)

    o = _dot(att.astype(BF16), v.astype(BF16)) + _dot((q * e_b).astype(BF16), state.astype(BF16))

    e_last = e_b[chunk - 1:chunk]
    eye = lax.broadcasted_iota(jnp.int32, (dk, dk), 0) == lax.broadcasted_iota(jnp.int32, (dk, dk), 1)
    e_col = jnp.sum(jnp.where(eye, e_last, 0.0), axis=-1, keepdims=True)
    new_state = state * e_col + _dot((k * e_u).astype(BF16), v.astype(BF16), _TN)
    return o, new_state


SAFE_DECAY = 60.0
SPLIT_PAD = 32


def _split_prefix_matrix(chunk, n_seq):
    e = _exponent_matrix(chunk)
    eye = np.eye(n_seq, dtype=np.float32)
    t = np.concatenate([np.kron(eye, e[:chunk]), np.kron(eye, e[chunk:2 * chunk])], axis=0)
    pad = (-3 * t.shape[1]) % SPLIT_PAD
    return np.concatenate([t, t, t, np.zeros((t.shape[0], pad), np.float32)], axis=1)


def _hgrn_chunk_bounded(q, k, lf, v, states, tmat, chunk):
    n_seq = len(states)
    n_rows = n_seq * chunk
    hi = lf.astype(BF16)
    r1 = lf - hi.astype(F32)
    mid = r1.astype(BF16)
    lo = r1 - mid.astype(F32)
    parts = [hi.astype(F32), mid.astype(F32), lo]
    pad = tmat.shape[1] - 3 * n_rows
    if pad:
        parts.append(jnp.zeros((pad, lf.shape[1]), F32))
    sums = _dot(tmat, jnp.concatenate(parts, axis=0).astype(BF16))
    b, u = sums[:n_rows], sums[n_rows:]
    e_b = jnp.exp(b)
    qb = q * e_b
    kn = (k * jnp.exp(-b)).astype(BF16)
    ku = k * jnp.exp(u)
    row = lax.broadcasted_iota(jnp.int32, (n_rows, n_rows), 0)
    col = lax.broadcasted_iota(jnp.int32, (n_rows, n_rows), 1)
    causal = row >= col
    if n_seq > 1:
        shift = chunk.bit_length() - 1
        causal = causal & ((row >> shift) == (col >> shift))
    eye = lax.broadcasted_iota(jnp.int32, (A_KDIM, A_KDIM), 0) == lax.broadcasted_iota(jnp.int32, (A_KDIM, A_KDIM), 1)
    ks = [slice(h * A_KDIM, (h + 1) * A_KDIM) for h in range(A_HEADS)]
    vs = [slice(h * A_VDIM, (h + 1) * A_VDIM) for h in range(A_HEADS)]
    sq = [slice(s * chunk, (s + 1) * chunk) for s in range(n_seq)]
    qbb, vb = qb.astype(BF16), v.astype(BF16)
    qb_s = [qbb] if n_seq == 1 else [qb[r].astype(BF16) for r in sq]
    ku_s = [ku.astype(BF16)] if n_seq == 1 else [ku[r].astype(BF16) for r in sq]
    v_s = [vb] if n_seq == 1 else [v[r].astype(BF16) for r in sq]
    atts = [jnp.where(causal, _dot(qbb[:, ks[h]], kn[:, ks[h]], _NT), 0.0).astype(BF16) for h in range(A_HEADS)]
    outs = []
    for h in range(A_HEADS):
        carried = [_dot(qb_s[s][:, ks[h]], states[s][h].astype(BF16)) for s in range(n_seq)]
        carried = carried[0] if n_seq == 1 else jnp.concatenate(carried, axis=0)
        outs.append(_dot(atts[h], vb[:, vs[h]]) + carried)
    new_states = []
    for s in range(n_seq):
        last = (s + 1) * chunk - 1
        new_states.append([])
        for h in range(A_HEADS):
            e_col = jnp.sum(jnp.where(eye, e_b[last:last + 1, ks[h]], 0.0), axis=-1, keepdims=True)
            new_states[s].append(states[s][h] * e_col + _dot(ku_s[s][:, ks[h]], v_s[s][:, vs[h]], _TN))
    return outs, new_states


def _hgrn_scan_kernel(*refs, chunk, n_chunks, has_state, n_seq):
    one_shot = n_seq > 0
    if has_state:
        q_ref, k_ref, lf_ref, v_ref, zs_ref, gw_ref, e_ref, t_ref, s0_ref, og_ref, so_ref, s_scr = refs
    else:
        q_ref, k_ref, lf_ref, v_ref, zs_ref, gw_ref, e_ref, t_ref, og_ref, so_ref, s_scr = refs
    t = pl.program_id(1)

    if not one_shot:
        @pl.when(t == 0)
        def _():
            if has_state:
                s_scr[...] = s0_ref[0]
            else:
                s_scr[...] = jnp.zeros_like(s_scr)

    def load_state(s, h):
        return s0_ref[s, h] if one_shot else s_scr[h]

    def store_state(s, h, val):
        if one_shot:
            so_ref[s, h] = val
        else:
            s_scr[h] = val

    gw = gw_ref[...]

    def finish(rows, h, o):
        vc = slice(h * A_VDIM, (h + 1) * A_VDIM)
        ms = jnp.mean(o * o, axis=-1, keepdims=True)
        og_ref[rows, vc] = (o * lax.rsqrt(ms + EPS) * gw * zs_ref[rows, vc]).astype(og_ref.dtype)

    seqs = range(max(n_seq, 1))

    def chunk_rows(c, s=None):
        if n_chunks > 1:
            return pl.ds(pl.multiple_of(c * chunk, chunk), chunk)
        return slice(0, len(seqs) * chunk) if s is None else slice(s * chunk, (s + 1) * chunk)

    def bounded_body(c, carry):
        rows = chunk_rows(c)
        states = [[load_state(s, h) for h in range(A_HEADS)] for s in seqs]
        outs, new_states = _hgrn_chunk_bounded(q_ref[rows, :], k_ref[rows, :], lf_ref[rows, :], v_ref[rows, :],
                                               states, t_ref[...], chunk)
        for h in range(A_HEADS):
            for s in seqs:
                store_state(s, h, new_states[s][h])
            finish(rows, h, outs[h])
        return carry

    def general_body(c, carry):
        for s in seqs:
            rows = chunk_rows(c, s)
            for h in range(A_HEADS):
                kc = slice(h * A_KDIM, (h + 1) * A_KDIM)
                vc = slice(h * A_VDIM, (h + 1) * A_VDIM)
                o, s_new = _hgrn_chunk(q_ref[rows, kc], k_ref[rows, kc], lf_ref[rows, kc],
                                       v_ref[rows, vc], load_state(s, h), e_ref[...], chunk)
                store_state(s, h, s_new)
                finish(rows, h, o)
        return carry

    def run(body):
        def go():
            if n_chunks == 1:
                body(0, 0)
            else:
                lax.fori_loop(0, n_chunks, body, 0)
        return go

    bounded = jnp.min(lf_ref[...]) * chunk >= -SAFE_DECAY
    lax.cond(bounded, run(bounded_body), run(general_body))

    if not one_shot:
        @pl.when(t == pl.num_programs(1) - 1)
        def _():
            so_ref[0] = s_scr[...]


def _hgrn_scan(q, k, lf, v, zs, gw, s0, batch, seq, tile, chunk, out_dtype, n_seq=0):
    n_t = seq // tile
    per_step = max(n_seq, 1)
    assert batch % per_step == 0 and (n_seq == 0 or (s0 is not None and seq == tile == chunk))
    emat = jnp.asarray(_exponent_matrix(chunk))
    tmat = jnp.asarray(_split_prefix_matrix(chunk, per_step), dtype=BF16)
    row_block = lambda w: pl.BlockSpec((tile * per_step, w), lambda b, t: (b * n_t + t, 0))
    state_block = pl.BlockSpec((per_step, A_HEADS, A_KDIM, A_VDIM), lambda b, t: (b, 0, 0, 0))
    in_specs = [row_block(A_QK), row_block(A_QK), row_block(A_QK), row_block(A_V), row_block(A_V),
                pl.BlockSpec((1, A_VDIM), lambda b, t: (0, 0)),
                pl.BlockSpec(emat.shape, lambda b, t: (0, 0)),
                pl.BlockSpec(tmat.shape, lambda b, t: (0, 0))]
    args = [q, k, lf, v, zs, gw, emat, tmat]
    if s0 is not None:
        in_specs.append(state_block)
        args.append(s0)
    kern = functools.partial(_hgrn_scan_kernel, chunk=chunk, n_chunks=tile // chunk,
                             has_state=s0 is not None, n_seq=n_seq)
    return pl.pallas_call(
        kern,
        grid=(batch // per_step, n_t),
        in_specs=in_specs,
        out_specs=[row_block(A_V), state_block],
        out_shape=[jax.ShapeDtypeStruct((batch * seq, A_V), out_dtype),
                   jax.ShapeDtypeStruct((batch, A_HEADS, A_KDIM, A_VDIM), F32)],
        scratch_shapes=[pltpu.VMEM((A_HEADS, A_KDIM, A_VDIM), F32)],
        compiler_params=_params(("parallel", "arbitrary")),
        name="hgrn_scan",
    )(*args)


def _outproj_kernel(*refs):
    *g_refs, w_ref, x_ref, ada_ref, o_ref = refs
    gs = [g_ref[...] for g_ref in g_refs]
    gs = [g.reshape(-1, g.shape[-1]).astype(BF16) for g in gs]
    ada = ada_ref[...]
    if len(x_ref.shape) == 2:
        ada = ada.reshape(1, ada.shape[-1])
    for c0 in range(0, D_MODEL, COL_TILE):
        cols = (Ellipsis, slice(c0, c0 + COL_TILE))
        y, k0 = None, 0
        for g in gs:
            part = _dot(g, w_ref[k0:k0 + g.shape[1], c0:c0 + COL_TILE])
            y = part if y is None else y + part
            k0 += g.shape[1]
        gate = ada[..., 2 * D_MODEL + c0:2 * D_MODEL + c0 + COL_TILE]
        o_ref[cols] = x_ref[cols] + gate * y.reshape(x_ref.shape[:-1] + (COL_TILE,))


def _outproj(gs, w_bf, x, ada, batch, seq, tm):
    n_tiles, tok_block, ada_block, shape, _ = _token_specs(batch, seq, tm)
    assert sum(g.shape[-1] for g in gs) == w_bf.shape[0]
    return pl.pallas_call(
        _outproj_kernel,
        grid=(n_tiles,),
        in_specs=[tok_block(g.shape[-1]) for g in gs] + [_resident(w_bf.shape), tok_block(D_MODEL), ada_block],
        out_specs=tok_block(D_MODEL),
        out_shape=jax.ShapeDtypeStruct(shape(D_MODEL), F32),
        compiler_params=_params(("parallel",)),
        name="outproj",
    )(*gs, w_bf, x, ada)


def _rope_table_kernel(pos_ref, inv_ref, cos_ref, sin_ref):
    ang = pos_ref[...] * inv_ref[...]
    lane = lax.broadcasted_iota(jnp.int32, ang.shape, 1)
    cos_ref[...] = jnp.cos(ang)
    sin_ref[...] = jnp.where(lane < B_HD // 2, -jnp.sin(ang), jnp.sin(ang))


def _rope_tables(start, seq):
    half = B_HD // 2
    inv = ROPE_THETA ** (-jnp.arange(half, dtype=F32) / half)
    inv2 = jnp.concatenate([inv, inv]).reshape(1, B_HD)
    pos = (start + jnp.arange(seq, dtype=jnp.int32)).astype(F32).reshape(seq, 1)
    return pl.pallas_call(
        _rope_table_kernel,
        out_shape=[jax.ShapeDtypeStruct((seq, B_HD), F32)] * 2,
        name="rope_tables",
    )(pos, inv2)


KV_ROWS = 2 * B_HG


def _l1_inproj_kernel(x_ref, ada_ref, nw_ref, w_ref, qn_ref, kn_ref, cos_ref, sin_ref, q_ref, kv_ref, zs_ref,
                      *row_refs, row_tokens):
    h = _modulated_norm(x_ref, ada_ref, nw_ref)

    def put_rows(g, is_v, tile):
        n = row_tokens[g]
        for hd in range(B_HG):
            row_refs[g][pl.ds(is_v * B_HG + hd, n, stride=KV_ROWS), :] = (
                tile[tile.shape[0] - n:, hd * B_HD:(hd + 1) * B_HD])

    tm = h.shape[0]
    cos = cos_ref[...]
    sin = sin_ref[...]
    reps = tm // cos.shape[0]
    if reps > 1:
        cos = jnp.broadcast_to(cos[None], (reps,) + cos.shape).reshape(tm, B_HD)
        sin = jnp.broadcast_to(sin[None], (reps,) + sin.shape).reshape(tm, B_HD)

    def norm_rope(acc, nw):
        outs = []
        for hd in range(acc.shape[1] // B_HD):
            xh = acc[:, hd * B_HD:(hd + 1) * B_HD]
            ms = jnp.mean(xh * xh, axis=-1, keepdims=True)
            y = xh * lax.rsqrt(ms + EPS) * nw
            outs.append(y * cos + pltpu.roll(y, B_HD // 2, 1) * sin)
        return jnp.concatenate(outs, axis=-1)

    for c0 in range(0, w_ref.shape[1], B_GW):
        acc = _dot(h, w_ref[:, c0:c0 + B_GW])
        seg, g = divmod(c0 // B_GW, len(B_GROUPS))
        if seg == 0:
            _put(q_ref, g * B_GW, norm_rope(acc, qn_ref[...]))
        elif seg == 1:
            kt = norm_rope(acc, kn_ref[...])
            _put(kv_ref, 2 * g * B_GW, kt)
            if row_refs:
                put_rows(g, 0, kt)
        elif seg == 2:
            _put(kv_ref, (2 * g + 1) * B_GW, acc)
            if row_refs:
                put_rows(g, 1, acc)
        else:
            _put(zs_ref, g * B_GW, _silu(acc))


def _l1_inproj(x, ada, nw, w_bf, qn, kn, cos, sin, batch, seq, tm, act_dtype):
    n_tiles, tok_block, ada_block, shape, per = _token_specs(batch, seq, tm)
    out_specs = [tok_block(B_WIDTH), tok_block(2 * B_WIDTH), tok_block(B_WIDTH)]
    out_shape = [jax.ShapeDtypeStruct(shape(B_WIDTH), F32),
                 jax.ShapeDtypeStruct(shape(2 * B_WIDTH), F32),
                 jax.ShapeDtypeStruct(shape(B_WIDTH), act_dtype)]
    row_tokens = ()
    if per is None:
        tab_block = _resident((seq, B_HD))
    else:
        tab_block = pl.BlockSpec((tm, B_HD), lambda i: (i % per, 0))
        for window, _ in B_GROUPS:
            keep = min(window, seq)
            blk = min(keep, tm)
            n_blk = keep // blk
            first = per - n_blk
            out_specs.append(pl.BlockSpec(
                (blk * KV_ROWS, B_HD),
                lambda i, n_blk=n_blk, first=first: ((i // per) * n_blk + jnp.clip(i % per - first, 0, n_blk - 1), 0)))
            out_shape.append(jax.ShapeDtypeStruct((batch * keep * KV_ROWS, B_HD), F32))
            row_tokens += (blk,)
    return pl.pallas_call(
        functools.partial(_l1_inproj_kernel, row_tokens=row_tokens),
        grid=(n_tiles,),
        in_specs=[tok_block(D_MODEL), ada_block, _resident((1, D_MODEL)), _resident(w_bf.shape),
                  _resident((1, B_HD)), _resident((1, B_HD)), tab_block, tab_block],
        out_specs=out_specs,
        out_shape=out_shape,
        compiler_params=_params(("arbitrary",)),
        name="l1_inproj",
    )(x, ada, nw, w_bf, qn, kn, cos, sin)


Q_SUB = 128
ATTN_TILE = 2048
ATTN_BATCH = 8
MIX_ROWS = 256


def _attend(qs, k2s, v2s, valids):
    scale = B_HD ** -0.5
    ss = [jnp.where(ok, _dot(q, k2, _NT) * scale, -jnp.inf) for q, k2, ok in zip(qs, k2s, valids)]
    ms = [jnp.max(s, axis=-1, keepdims=True) for s in ss]
    ps = [jnp.exp(s - m) for s, m in zip(ss, ms)]
    ls = [jnp.sum(p, axis=-1, keepdims=True) for p in ps]
    os = [_dot(p.astype(BF16), v2) * (1.0 / l) for p, v2, l in zip(ps, v2s, ls)]
    lses = [m + jnp.log(l) for m, l in zip(ms, ls)]
    return os, lses


def _prompt_attn_kernel(*refs):
    n_g = len(B_GROUPS)
    ins, zs_refs = refs[:5 * n_g], refs[5 * n_g:6 * n_g]
    out_refs, (o_scr, lse_scr) = refs[6 * n_g:7 * n_g], refs[7 * n_g:]
    first_key = jnp.where(pl.program_id(2) == 0, Q_SUB, 0)
    row = lax.broadcasted_iota(jnp.int32, (Q_SUB, 2 * Q_SUB), 0)
    col = lax.broadcasted_iota(jnp.int32, (Q_SUB, 2 * Q_SUB), 1)
    band = (col >= row) & (col <= row + Q_SUB)
    band_first = band & (col >= first_key)

    for g, (_, dil) in enumerate(B_GROUPS):
        q_ref, kc_ref, vc_ref, kp_ref, vp_ref = ins[5 * g:5 * g + 5]
        n_sb = ATTN_TILE // (Q_SUB * dil)

        def rows_of(r, first, n):
            start = r + first * dil
            return pl.ds(start, n) if dil == 1 else pl.ds(start, n, stride=dil)

        work = []
        for r in range(dil):
            cls = rows_of(r, 0, n_sb * Q_SUB)
            kr = jnp.concatenate([kp_ref[0, rows_of(r, 0, Q_SUB), :], kc_ref[0, cls, :]], axis=0).astype(BF16)
            vr = jnp.concatenate([vp_ref[0, rows_of(r, 0, Q_SUB), :], vc_ref[0, cls, :]], axis=0).astype(BF16)
            qr = q_ref[0, cls, :].astype(BF16)
            for sb in range(n_sb):
                work.append((rows_of(r, sb * Q_SUB, Q_SUB), qr[sb * Q_SUB:(sb + 1) * Q_SUB],
                             kr[sb * Q_SUB:(sb + 2) * Q_SUB], vr[sb * Q_SUB:(sb + 2) * Q_SUB],
                             band_first if sb == 0 else band))
        for i in range(0, len(work), ATTN_BATCH):
            part = work[i:i + ATTN_BATCH]
            os, lses = _attend([w[1] for w in part], [w[2] for w in part], [w[3] for w in part],
                               [w[4] for w in part])
            for w, o, lse in zip(part, os, lses):
                o_scr[g, w[0], :] = o
                lse_scr[g, w[0], :] = jnp.broadcast_to(lse, (Q_SUB, B_HD))

    for c in range(ATTN_TILE // MIX_ROWS):
        rows = slice(c * MIX_ROWS, (c + 1) * MIX_ROWS)
        lses = [lse_scr[g, rows, :] for g in range(n_g)]
        m = functools.reduce(jnp.maximum, lses)
        es = [jnp.exp(l - m) for l in lses]
        inv = 1.0 / functools.reduce(lambda a, b: a + b, es)
        for g in range(n_g):
            out_refs[g][0, rows, :] = (o_scr[g, rows, :] * (es[g] * inv) * zs_refs[g][0, rows, :]
                                       ).astype(out_refs[g].dtype)


def _prompt_attn(q, kv, zs, batch, seq):
    assert seq % ATTN_TILE == 0
    n_tiles = seq // ATTN_TILE
    tok = lambda c0: pl.BlockSpec((1, ATTN_TILE, B_HD), lambda b, h, i: (b, i, c0 + h))
    in_specs, args = [], []
    for g, (_, dil) in enumerate(B_GROUPS):
        back = Q_SUB * dil
        assert ATTN_TILE % back == 0
        k0, v0 = 2 * g * B_HG, (2 * g + 1) * B_HG
        prev = lambda c0, back=back: pl.BlockSpec(
            (1, back, B_HD), lambda b, h, i: (b, jnp.maximum(i * (ATTN_TILE // back) - 1, 0), c0 + h))
        in_specs += [tok(g * B_HG), tok(k0), tok(v0), prev(k0), prev(v0)]
        args += [q, kv, kv, kv, kv]
    in_specs += [tok(g * B_HG) for g in range(len(B_GROUPS))]
    args += [zs] * len(B_GROUPS)
    out = pl.BlockSpec((1, ATTN_TILE, B_HD), lambda b, h, i: (b, i, h))
    return pl.pallas_call(
        _prompt_attn_kernel,
        grid=(batch, B_HG, n_tiles),
        in_specs=in_specs,
        out_specs=[out] * len(B_GROUPS),
        out_shape=[jax.ShapeDtypeStruct((batch, seq, B_GW), BF16)] * len(B_GROUPS),
        scratch_shapes=[pltpu.VMEM((len(B_GROUPS), ATTN_TILE, B_HD), F32)] * 2,
        compiler_params=_params(("parallel", "parallel", "arbitrary")),
        name="prompt_attn",
    )(*args)


SAMPLE_NB = 2


def _sample_masks(seq):
    hq = np.arange(B_HG * seq)[:, None] // seq
    iq = np.arange(B_HG * seq)[:, None] % seq
    masks = []
    for window, dil in B_GROUPS:
        n_tok = window if dil < seq else (window // dil) * seq
        col = np.arange(n_tok * KV_ROWS)[None, :]
        tok, is_v, head = col // KV_ROWS, (col // B_HG) % 2, col % B_HG
        if dil < seq:
            ok = (tok >= iq) & ((tok - iq) % dil == 0)
        else:
            ok = (tok % seq) == iq
        masks.append(np.where(ok & (is_v == 0) & (head == hq), 0.0, -np.inf).astype(np.float32))
    col = np.arange(B_HG * seq)[None, :]
    new = []
    for _, dil in B_GROUPS:
        ok = (col // seq == hq) & (col % seq <= iq) & ((iq - col % seq) % dil == 0)
        new.append(np.where(ok, 0.0, -np.inf).astype(np.float32))
    return masks, np.stack(new)


def _sample_attn_kernel(q_ref, kv_ref, zs_ref, c0_ref, c1_ref, c2_ref, m0_ref, m1_ref, m2_ref, mn_ref,
                        out_ref, *, seq, nb):
    scale = B_HD ** -0.5
    caches = (c0_ref, c1_ref, c2_ref)
    masks = (m0_ref, m1_ref, m2_ref)
    groups = range(len(B_GROUPS))

    def heads(ref, e, c0):
        return jnp.concatenate([ref[e, :, c0 + h * B_HD:c0 + (h + 1) * B_HD] for h in range(B_HG)],
                               axis=0).astype(BF16)

    def scores(e):
        qa = [heads(q_ref, e, g * B_GW) for g in groups]
        k_new = [heads(kv_ref, e, 2 * g * B_GW) for g in groups]
        rows = [caches[g][e].reshape(-1, B_HD).astype(BF16) for g in groups]
        s_c = [_dot(qa[g], rows[g], _NT) * scale + masks[g][...] for g in groups]
        s_n = [_dot(qa[g], k_new[g], _NT) * scale + mn_ref[g] for g in groups]
        return rows, s_c, s_n

    def weights(s_c, s_n):
        ms = [jnp.maximum(jnp.max(s_c[g], axis=-1, keepdims=True), jnp.max(s_n[g], axis=-1, keepdims=True))
              for g in groups]
        p_c = [jnp.exp(s_c[g] - ms[g]) for g in groups]
        p_n = [jnp.exp(s_n[g] - ms[g]) for g in groups]
        ls = [jnp.sum(p_c[g], axis=-1, keepdims=True) + jnp.sum(p_n[g], axis=-1, keepdims=True) for g in groups]
        p_v = [pltpu.roll(p_c[g], B_HG, 1).astype(BF16) for g in groups]
        return ms, ls, p_v, [p.astype(BF16) for p in p_n]

    def values(e, rows, p_v, p_n):
        v_new = [heads(kv_ref, e, (2 * g + 1) * B_GW) for g in groups]
        return [_dot(p_v[g], rows[g]) + _dot(p_n[g], v_new[g]) for g in groups]

    scored = [scores(e) for e in range(nb)]
    weighted = [weights(s_c, s_n) for _, s_c, s_n in scored]
    accs = [values(e, scored[e][0], weighted[e][2], weighted[e][3]) for e in range(nb)]
    for e in range(nb):
        ms, ls = weighted[e][0], weighted[e][1]
        outs = [accs[e][g] / ls[g] for g in groups]
        lses = [ms[g] + jnp.log(ls[g]) for g in groups]
        m = functools.reduce(jnp.maximum, lses)
        es = [jnp.exp(l - m) for l in lses]
        inv = 1.0 / (es[0] + es[1] + es[2])
        for g in groups:
            mixed = outs[g] * (es[g] * inv)
            for h in range(B_HG):
                cols = slice(g * B_GW + h * B_HD, g * B_GW + (h + 1) * B_HD)
                out_ref[e, :, cols] = mixed[h * seq:(h + 1) * seq] * zs_ref[e, :, cols]


def _sample_attn(q, kv, zs, caches, batch, seq, nb):
    assert batch % nb == 0
    masks, mask_new = _sample_masks(seq)
    specs, views = [], []
    for (window, dil), c in zip(B_GROUPS, caches):
        assert window == dil * (N_KEYS - 1) and c.shape[2] == window
        if dil >= seq:
            views.append(c.reshape(batch, window // dil, dil * KV_ROWS, B_HD))
            specs.append(pl.BlockSpec((nb, window // dil, seq * KV_ROWS, B_HD), lambda b: (b, 0, 0, 0)))
        else:
            assert seq % dil == 0
            views.append(c.reshape(batch, window * KV_ROWS, B_HD))
            specs.append(pl.BlockSpec((nb, window * KV_ROWS, B_HD), lambda b: (b, 0, 0)))
    const = lambda a: pl.BlockSpec(a.shape, lambda b: (0,) * a.ndim)
    tok = lambda w: pl.BlockSpec((nb, seq, w), lambda b: (b, 0, 0))
    return pl.pallas_call(
        functools.partial(_sample_attn_kernel, seq=seq, nb=nb),
        grid=(batch // nb,),
        in_specs=[tok(B_WIDTH), tok(2 * B_WIDTH), tok(B_WIDTH)] + specs
                 + [const(a) for a in masks] + [const(mask_new)],
        out_specs=tok(B_WIDTH),
        out_shape=jax.ShapeDtypeStruct((batch, seq, B_WIDTH), F32),
        compiler_params=_params(("parallel",)),
        name="sample_attn",
    )(q, kv, zs, *views, *[jnp.asarray(a) for a in masks], jnp.asarray(mask_new))


def _trunk(x, ada, start, state, caches, weights, tm_in, tm_out, scan_tile, scan_chunk):
    (norm_w, alb, a_w_in, a_b_f, a_gw, a_w_out, b_w_in, b_qn, b_kn, b_w_out) = weights
    batch, seq, _ = x.shape
    flat = seq >= tm_in
    act = BF16 if flat else F32
    tok = (lambda a: a.reshape(batch * seq, a.shape[-1])) if flat else (lambda a: a)
    rows = lambda a: a.reshape(batch * seq, a.shape[-1])
    ada0 = ada[0].reshape(batch, 1, 3 * D_MODEL)
    ada1 = ada[1].reshape(batch, 1, 3 * D_MODEL)

    x0 = tok(x)
    q, k, lf, v, zs = _l0_inproj(x0, ada0, norm_w[0:1], a_w_in, alb, a_b_f, batch, seq, tm_in, act)
    og, s_new = _hgrn_scan(rows(q), rows(k), rows(lf), rows(v), rows(zs), a_gw, state,
                           batch, seq, scan_tile, scan_chunk, act, n_seq=0 if state is None else SAMPLE_NB)
    og = og if flat else og.reshape(batch, seq, A_V)
    x1 = _outproj([og], a_w_out, x0, ada0, batch, seq, tm_out)

    cos, sin = _rope_tables(start, seq)
    qr, kv, zs1, *new_rows = _l1_inproj(x1, ada1, norm_w[1:2], b_w_in, b_qn, b_kn, cos, sin, batch, seq, tm_in, act)
    kv3 = kv.reshape(batch, seq, 2 * B_WIDTH)
    if caches is None:
        og1 = _prompt_attn(qr.reshape(batch, seq, B_WIDTH), kv3, zs1.reshape(batch, seq, B_WIDTH), batch, seq)
        og1 = [o.reshape(batch * seq, B_GW) for o in og1]
    else:
        og1 = [_sample_attn(qr, kv, zs1, caches, batch, seq, SAMPLE_NB)]
    x2 = _outproj(og1, b_w_out, x1, ada1, batch, seq, tm_out)

    if new_rows:
        kv_rows = [r.reshape(batch, min(window, seq), 2, B_HG, B_HD) for r, (window, _) in zip(new_rows, B_GROUPS)]
    else:
        kv_rows = [kv3[:, seq - min(window, seq):, 2 * g * B_GW:2 * (g + 1) * B_GW]
                   .reshape(batch, min(window, seq), 2, B_HG, B_HD) for g, (window, _) in enumerate(B_GROUPS)]
    return x2.reshape(batch, seq, D_MODEL), s_new, kv_rows


def kernel(x_prompt, x_sample, state_hgrn, cache_kv_w128, cache_kv_w512, cache_kv_w2048, c_prompt, c_sample,
           norm_w, ada_w, ada_b, a_lower_bounds, a_w_in, a_b_f, a_g_norm_w, a_w_out, b_w_in, b_q_norm_w,
           b_k_norm_w, b_w_out):
    bp, lp, _ = x_prompt.shape
    bs, ls, _ = x_sample.shape

    pad = (-(bp + bs)) % 8
    c_all = jnp.concatenate([c_prompt, c_sample, jnp.zeros((pad, D_MODEL), F32)], axis=0)
    ada = _ada_vectors(c_all, ada_w, ada_b)
    ada_p, ada_s = ada[:, :bp], ada[:, bp:bp + bs]

    weights = (norm_w, a_lower_bounds, a_w_in[0].astype(BF16), a_b_f, a_g_norm_w, a_w_out[0].astype(BF16),
               b_w_in[0].astype(BF16), b_q_norm_w, b_k_norm_w, b_w_out[0].astype(BF16))

    y_p, s_p, kv_p = _trunk(x_prompt, ada_p, 0, None, None, weights,
                            tm_in=256, tm_out=512, scan_tile=256, scan_chunk=64)
    caches = (cache_kv_w128, cache_kv_w512, cache_kv_w2048)
    y_s, s_s, kv_s = _trunk(x_sample, ada_s, PAST_LEN, state_hgrn[0], caches, weights,
                            tm_in=256, tm_out=256, scan_tile=ls, scan_chunk=ls)

    kv_out = []
    for g in range(len(B_GROUPS)):
        kv_out.append(kv_p[g][None])
        kv_out.append(kv_s[g][None])
    return (y_p, y_s, s_p[None], s_s[None], *kv_out)
```

```python
import functools
import math

import numpy as np
import jax
import jax.numpy as jnp
from jax import lax
from jax.experimental import pallas as pl
from jax.experimental.pallas import tpu as pltpu

F32 = jnp.float32
BF16 = jnp.bfloat16

D_MODEL = 1024
EPS = 1e-6
A_HEADS = 8
A_KDIM = 128
A_VDIM = 256
A_QK = A_HEADS * A_KDIM
A_V = A_HEADS * A_VDIM
B_GROUPS = ((128, 1), (512, 4), (2048, 16))
B_HG = 4
B_HD = 128
B_GW = B_HG * B_HD
B_WIDTH = len(B_GROUPS) * B_GW
ROPE_THETA = 10000.0
PAST_LEN = 2048
N_KEYS = 129

COL_TILE = 512
VMEM_LIMIT = 56 * 1024 * 1024

_NT = (((1,), (1,)), ((), ()))
_TN = (((0,), (0,)), ((), ()))


def _sigmoid(x):
    return 1.0 / (1.0 + jnp.exp(-x))


def _silu(x):
    h = 0.5 * x
    return h * jnp.tanh(h) + h


def _dot(a, b, dims=None):
    if dims is None:
        return jnp.dot(a, b, preferred_element_type=F32)
    return lax.dot_general(a, b, dims, preferred_element_type=F32)


def _params(sem):
    return pltpu.CompilerParams(dimension_semantics=sem, vmem_limit_bytes=VMEM_LIMIT)


def _ada_kernel(c_ref, w_ref, b_ref, o_ref):
    a = _silu(c_ref[...]).astype(BF16)
    o_ref[0] = _dot(a, w_ref[0].astype(BF16)) + b_ref[0]


def _ada_vectors(c_all, ada_w, ada_b):
    n_layers, _, width = ada_w.shape
    rows = c_all.shape[0]
    return pl.pallas_call(
        _ada_kernel,
        grid=(n_layers, width // COL_TILE),
        in_specs=[
            pl.BlockSpec((rows, D_MODEL), lambda l, j: (0, 0)),
            pl.BlockSpec((1, D_MODEL, COL_TILE), lambda l, j: (l, 0, j)),
            pl.BlockSpec((1, 1, COL_TILE), lambda l, j: (l, 0, j)),
        ],
        out_specs=pl.BlockSpec((1, rows, COL_TILE), lambda l, j: (l, 0, j)),
        out_shape=jax.ShapeDtypeStruct((n_layers, rows, width), F32),
        compiler_params=_params(("parallel", "parallel")),
        name="ada_vectors",
    )(c_all, ada_w, ada_b.reshape(n_layers, 1, width))


def _modulated_norm(x_ref, ada_ref, nw_ref):
    x = x_ref[...]
    ada = ada_ref[...]
    if x.ndim == 2:
        ada = ada.reshape(1, ada.shape[-1])
    shift = ada[..., :D_MODEL]
    scale = ada[..., D_MODEL:2 * D_MODEL]
    ms = jnp.mean(x * x, axis=-1, keepdims=True)
    y = x * lax.rsqrt(ms + EPS) * nw_ref[...].reshape((1,) * (x.ndim - 1) + (D_MODEL,))
    h = y * (1.0 + scale) + shift
    return h.reshape(-1, D_MODEL).astype(BF16)


def _token_specs(batch, seq, tm):
    if seq >= tm:
        assert seq % tm == 0
        per = seq // tm
        n_tiles = batch * per
        tok_block = lambda w: pl.BlockSpec((tm, w), lambda i: (i, 0))
        ada_block = pl.BlockSpec((1, 1, 3 * D_MODEL), lambda i: (i // per, 0, 0))
        shape = lambda w: (batch * seq, w)
        return n_tiles, tok_block, ada_block, shape, per
    tb = tm // seq
    assert batch % tb == 0
    n_tiles = batch // tb
    tok_block = lambda w: pl.BlockSpec((tb, seq, w), lambda i: (i, 0, 0))
    ada_block = pl.BlockSpec((tb, 1, 3 * D_MODEL), lambda i: (i, 0, 0))
    shape = lambda w: (batch, seq, w)
    return n_tiles, tok_block, ada_block, shape, None


def _resident(shape):
    return pl.BlockSpec(shape, lambda i: (0,) * len(shape), pipeline_mode=pl.Buffered(1))


def _put(ref, c0, val):
    ref[(Ellipsis, slice(c0, c0 + val.shape[-1]))] = val.reshape(ref.shape[:-1] + val.shape[-1:]).astype(ref.dtype)


def _l0_inproj_kernel(x_ref, ada_ref, nw_ref, w_ref, alb_ref, bf_ref, q_ref, k_ref, lf_ref, v_ref, zs_ref):
    h = _modulated_norm(x_ref, ada_ref, nw_ref)
    a = alb_ref[...]
    m = jnp.max(a, axis=0, keepdims=True)
    e = jnp.exp(a - m)
    lb = e[0:1] / jnp.sum(e, axis=0, keepdims=True)
    for c0 in range(0, w_ref.shape[1], COL_TILE):
        acc = _dot(h, w_ref[:, c0:c0 + COL_TILE])
        if c0 < A_QK:
            _put(q_ref, c0, _silu(acc))
        elif c0 < 2 * A_QK:
            c = c0 - A_QK
            f = lb[:, c:c + COL_TILE] + (1.0 - lb[:, c:c + COL_TILE]) * _sigmoid(acc + bf_ref[:, c:c + COL_TILE])
            _put(lf_ref, c, jnp.log(f))
            _put(k_ref, c, 1.0 - f)
        elif c0 < 2 * A_QK + A_V:
            _put(v_ref, c0 - 2 * A_QK, acc)
        else:
            _put(zs_ref, c0 - 2 * A_QK - A_V, _silu(acc))


def _l0_inproj(x, ada, nw, w_bf, alb, b_f, batch, seq, tm, act_dtype):
    n_tiles, tok_block, ada_block, shape, _ = _token_specs(batch, seq, tm)
    return pl.pallas_call(
        _l0_inproj_kernel,
        grid=(n_tiles,),
        in_specs=[tok_block(D_MODEL), ada_block, _resident((1, D_MODEL)), _resident(w_bf.shape),
                  _resident(alb.shape), _resident(b_f.shape)],
        out_specs=[tok_block(A_QK), tok_block(A_QK), tok_block(A_QK), tok_block(A_V), tok_block(A_V)],
        out_shape=[
            jax.ShapeDtypeStruct(shape(A_QK), F32),
            jax.ShapeDtypeStruct(shape(A_QK), F32),
            jax.ShapeDtypeStruct(shape(A_QK), F32),
            jax.ShapeDtypeStruct(shape(A_V), act_dtype),
            jax.ShapeDtypeStruct(shape(A_V), act_dtype),
        ],
        compiler_params=_params(("parallel",)),
        name="l0_inproj",
    )(x, ada, nw, w_bf, alb, b_f)


def _scan_levels(chunk):
    return [1 << l for l in range(int(math.log2(chunk)))]


def _exponent_matrix(chunk):
    r = np.arange(chunk)[:, None]
    c = np.arange(chunk)[None, :]
    blocks = [c <= r, c > r]
    for s in _scan_levels(chunk):
        bound = (r & ~(2 * s - 1)) | s
        upper = (r & s) != 0
        blocks.append(np.where(upper, (c > bound) & (c <= r), (c > r) & (c <= bound)))
    return np.concatenate(blocks, axis=0).astype(np.float32)


def _hgrn_chunk(q, k, lf, v, state, emat, chunk):
    dk = q.shape[-1]
    ex = jnp.exp(jnp.dot(emat, lf, preferred_element_type=F32, precision=lax.Precision.HIGHEST))
    e_b = ex[0:chunk]
    e_u = ex[chunk:2 * chunk]
    row = lax.broadcasted_iota(jnp.int32, (chunk, 1), 0)
    col = lax.broadcasted_iota(jnp.int32, (1, chunk), 1)

    att = jnp.where(row == col, jnp.sum(q * k, axis=-1, keepdims=True), 0.0)
    for l, s in enumerate(_scan_levels(chunk)):
        g = ex[(2 + l) * chunk:(3 + l) * chunk]
        upper = (row & s) != 0
        xb = (jnp.where(upper, q, k) * g).astype(BF16)
        part = _dot(xb, xb, _NT)
        keep = upper & ((col & s) == 0) & ((row >> (l + 1)) == (col >> (l + 1)))
        att = att + jnp.where(keep, part, 0.0)

    o = _dot(att.astype(BF16), v.astype(BF16)) + _dot((q * e_b).astype(BF16), state.astype(BF16))

    e_last = e_b[chunk - 1:chunk]
    eye = lax.broadcasted_iota(jnp.int32, (dk, dk), 0) == lax.broadcasted_iota(jnp.int32, (dk, dk), 1)
    e_col = jnp.sum(jnp.where(eye, e_last, 0.0), axis=-1, keepdims=True)
    new_state = state * e_col + _dot((k * e_u).astype(BF16), v.astype(BF16), _TN)
    return o, new_state


SAFE_DECAY = 60.0
SPLIT_PAD = 32
SCAN_LINK = 2


def _split_prefix_matrix(chunk, n_seq):
    t = np.kron(np.eye(n_seq, dtype=np.float32), _exponent_matrix(chunk)[:chunk])
    pad = (-3 * t.shape[1]) % SPLIT_PAD
    return np.concatenate([t, t, t, np.zeros((t.shape[0], pad), np.float32)], axis=1)


def _hgrn_chunk_bounded(q, k, lf, v, states, tmat, chunk, n_seq, linked):
    n_rows = n_seq * chunk
    hi = lf.astype(BF16)
    r1 = lf - hi.astype(F32)
    mid = r1.astype(BF16)
    lo = r1 - mid.astype(F32)
    parts = [hi.astype(F32), mid.astype(F32), lo]
    pad = tmat.shape[1] - 3 * n_rows
    if pad:
        parts.append(jnp.zeros((pad, lf.shape[1]), F32))
    b = _dot(tmat, jnp.concatenate(parts, axis=0).astype(BF16))
    e_b = jnp.exp(b)
    qb = q * e_b
    kn_f = k * jnp.exp(-b)
    kn = kn_f.astype(BF16)
    row = lax.broadcasted_iota(jnp.int32, (n_rows, n_rows), 0)
    col = lax.broadcasted_iota(jnp.int32, (n_rows, n_rows), 1)
    causal = row >= col
    if n_seq > 1:
        shift = chunk.bit_length() - 1
        causal = causal & ((row >> shift) == (col >> shift))
    eye = lax.broadcasted_iota(jnp.int32, (A_KDIM, A_KDIM), 0) == lax.broadcasted_iota(jnp.int32, (A_KDIM, A_KDIM), 1)
    ks = [slice(h * A_KDIM, (h + 1) * A_KDIM) for h in range(A_HEADS)]
    vs = [slice(h * A_VDIM, (h + 1) * A_VDIM) for h in range(A_HEADS)]
    sq = [slice(s * chunk, (s + 1) * chunk) for s in range(n_seq)]
    qbb, vb = qb.astype(BF16), v.astype(BF16)
    qb_s = [qbb] if n_seq == 1 else [qb[r].astype(BF16) for r in sq]
    ku_s = [(kn_f[r] * e_b[r.stop - 1:r.stop]).astype(BF16) for r in sq]
    v_s = [vb] if n_seq == 1 else [v[r].astype(BF16) for r in sq]
    atts = [jnp.where(causal, _dot(qbb[:, ks[h]], kn[:, ks[h]], _NT), 0.0).astype(BF16) for h in range(A_HEADS)]
    e_cols = [[jnp.sum(jnp.where(eye, e_b[(s + 1) * chunk - 1:(s + 1) * chunk, ks[h]], 0.0), axis=-1, keepdims=True)
               for h in range(A_HEADS)] for s in range(n_seq)]
    new_states = [[None] * A_HEADS for _ in range(n_seq)]
    carried = [[None] * A_HEADS for _ in range(n_seq)]
    for s in range(n_seq):
        for h in range(A_HEADS):
            state = new_states[s - 1][h] if (linked and s > 0) else states[0 if linked else s][h]
            carried[s][h] = _dot(qb_s[s][:, ks[h]], state.astype(BF16))
            new_states[s][h] = state * e_cols[s][h] + _dot(ku_s[s][:, ks[h]], v_s[s][:, vs[h]], _TN)
    outs = []
    for h in range(A_HEADS):
        inter = carried[0][h] if n_seq == 1 else jnp.concatenate([carried[s][h] for s in range(n_seq)], axis=0)
        outs.append(_dot(atts[h], vb[:, vs[h]]) + inter)
    return outs, new_states


def _hgrn_scan_kernel(*refs, chunk, n_chunks, has_state, n_seq, n_link):
    one_shot = n_seq > 0
    if has_state:
        q_ref, k_ref, lf_ref, v_ref, zs_ref, gw_ref, e_ref, t_ref, s0_ref, og_ref, so_ref, s_scr = refs
    else:
        q_ref, k_ref, lf_ref, v_ref, zs_ref, gw_ref, e_ref, t_ref, og_ref, so_ref, s_scr = refs
    t = pl.program_id(1)

    if not one_shot:
        @pl.when(t == 0)
        def _():
            if has_state:
                s_scr[...] = s0_ref[0]
            else:
                s_scr[...] = jnp.zeros_like(s_scr)

    def load_state(s, h):
        return s0_ref[s, h] if one_shot else s_scr[h]

    def store_state(s, h, val):
        if one_shot:
            so_ref[s, h] = val
        else:
            s_scr[h] = val

    gw = gw_ref[...]

    def finish(rows, h, o):
        vc = slice(h * A_VDIM, (h + 1) * A_VDIM)
        ms = jnp.mean(o * o, axis=-1, keepdims=True)
        og_ref[rows, vc] = (o * lax.rsqrt(ms + EPS) * gw * zs_ref[rows, vc]).astype(og_ref.dtype)

    seqs = range(max(n_seq, 1))

    def chunk_rows(c, s=None, n=1):
        if n_chunks > 1:
            return pl.ds(pl.multiple_of(c * (n * chunk), n * chunk), n * chunk)
        return slice(0, len(seqs) * chunk) if s is None else slice(s * chunk, (s + 1) * chunk)

    def bounded_body(c, carry):
        rows = chunk_rows(c, n=n_link)
        states = [[load_state(s, h) for h in range(A_HEADS)] for s in seqs]
        n_at_once = len(seqs) if one_shot else n_link
        outs, new_states = _hgrn_chunk_bounded(q_ref[rows, :], k_ref[rows, :], lf_ref[rows, :], v_ref[rows, :],
                                               states, t_ref[...], chunk, n_at_once, linked=not one_shot)
        for h in range(A_HEADS):
            if one_shot:
                for s in seqs:
                    store_state(s, h, new_states[s][h])
            else:
                store_state(0, h, new_states[-1][h])
            finish(rows, h, outs[h])
        return carry

    def general_body(c, carry):
        for s in seqs:
            rows = chunk_rows(c, s)
            for h in range(A_HEADS):
                kc = slice(h * A_KDIM, (h + 1) * A_KDIM)
                vc = slice(h * A_VDIM, (h + 1) * A_VDIM)
                o, s_new = _hgrn_chunk(q_ref[rows, kc], k_ref[rows, kc], lf_ref[rows, kc],
                                       v_ref[rows, vc], load_state(s, h), e_ref[...], chunk)
                store_state(s, h, s_new)
                finish(rows, h, o)
        return carry

    def run(body, trips):
        def go():
            if trips == 1:
                body(0, 0)
            else:
                lax.fori_loop(0, trips, body, 0)
        return go

    bounded = jnp.min(lf_ref[...]) * chunk >= -SAFE_DECAY
    lax.cond(bounded, run(bounded_body, n_chunks // n_link), run(general_body, n_chunks))

    if not one_shot:
        @pl.when(t == pl.num_programs(1) - 1)
        def _():
            so_ref[0] = s_scr[...]


def _hgrn_scan(q, k, lf, v, zs, gw, s0, batch, seq, tile, chunk, out_dtype, n_seq=0):
    n_t = seq // tile
    per_step = max(n_seq, 1)
    assert batch % per_step == 0 and (n_seq == 0 or (s0 is not None and seq == tile == chunk))
    n_link = SCAN_LINK if (n_seq == 0 and (tile // chunk) % SCAN_LINK == 0) else 1
    emat = jnp.asarray(_exponent_matrix(chunk))
    tmat = jnp.asarray(_split_prefix_matrix(chunk, max(per_step, n_link)), dtype=BF16)
    row_block = lambda w: pl.BlockSpec((tile * per_step, w), lambda b, t: (b * n_t + t, 0))
    state_block = pl.BlockSpec((per_step, A_HEADS, A_KDIM, A_VDIM), lambda b, t: (b, 0, 0, 0))
    in_specs = [row_block(A_QK), row_block(A_QK), row_block(A_QK), row_block(A_V), row_block(A_V),
                pl.BlockSpec((1, A_VDIM), lambda b, t: (0, 0)),
                pl.BlockSpec(emat.shape, lambda b, t: (0, 0)),
                pl.BlockSpec(tmat.shape, lambda b, t: (0, 0))]
    args = [q, k, lf, v, zs, gw, emat, tmat]
    if s0 is not None:
        in_specs.append(state_block)
        args.append(s0)
    kern = functools.partial(_hgrn_scan_kernel, chunk=chunk, n_chunks=tile // chunk,
                             has_state=s0 is not None, n_seq=n_seq, n_link=n_link)
    return pl.pallas_call(
        kern,
        grid=(batch // per_step, n_t),
        in_specs=in_specs,
        out_specs=[row_block(A_V), state_block],
        out_shape=[jax.ShapeDtypeStruct((batch * seq, A_V), out_dtype),
                   jax.ShapeDtypeStruct((batch, A_HEADS, A_KDIM, A_VDIM), F32)],
        scratch_shapes=[pltpu.VMEM((A_HEADS, A_KDIM, A_VDIM), F32)],
        compiler_params=_params(("parallel", "arbitrary")),
        name="hgrn_scan",
    )(*args)


def _outproj_kernel(*refs):
    *g_refs, w_ref, x_ref, ada_ref, o_ref = refs
    gs = [g_ref[...] for g_ref in g_refs]
    gs = [g.reshape(-1, g.shape[-1]).astype(BF16) for g in gs]
    ada = ada_ref[...]
    if len(x_ref.shape) == 2:
        ada = ada.reshape(1, ada.shape[-1])
    for c0 in range(0, D_MODEL, COL_TILE):
        cols = (Ellipsis, slice(c0, c0 + COL_TILE))
        y, k0 = None, 0
        for g in gs:
            part = _dot(g, w_ref[k0:k0 + g.shape[1], c0:c0 + COL_TILE])
            y = part if y is None else y + part
            k0 += g.shape[1]
        gate = ada[..., 2 * D_MODEL + c0:2 * D_MODEL + c0 + COL_TILE]
        o_ref[cols] = x_ref[cols] + gate * y.reshape(x_ref.shape[:-1] + (COL_TILE,))


def _outproj(gs, w_bf, x, ada, batch, seq, tm):
    n_tiles, tok_block, ada_block, shape, _ = _token_specs(batch, seq, tm)
    assert sum(g.shape[-1] for g in gs) == w_bf.shape[0]
    return pl.pallas_call(
        _outproj_kernel,
        grid=(n_tiles,),
        in_specs=[tok_block(g.shape[-1]) for g in gs] + [_resident(w_bf.shape), tok_block(D_MODEL), ada_block],
        out_specs=tok_block(D_MODEL),
        out_shape=jax.ShapeDtypeStruct(shape(D_MODEL), F32),
        compiler_params=_params(("parallel",)),
        name="outproj",
    )(*gs, w_bf, x, ada)


def _rope_table_kernel(pos_ref, inv_ref, cos_ref, sin_ref):
    ang = pos_ref[...] * inv_ref[...]
    lane = lax.broadcasted_iota(jnp.int32, ang.shape, 1)
    cos_ref[...] = jnp.cos(ang)
    sin_ref[...] = jnp.where(lane < B_HD // 2, -jnp.sin(ang), jnp.sin(ang))


def _rope_tables(start, seq):
    half = B_HD // 2
    inv = ROPE_THETA ** (-jnp.arange(half, dtype=F32) / half)
    inv2 = jnp.concatenate([inv, inv]).reshape(1, B_HD)
    pos = (start + jnp.arange(seq, dtype=jnp.int32)).astype(F32).reshape(seq, 1)
    return pl.pallas_call(
        _rope_table_kernel,
        out_shape=[jax.ShapeDtypeStruct((seq, B_HD), F32)] * 2,
        name="rope_tables",
    )(pos, inv2)


KV_ROWS = 2 * B_HG


def _l1_inproj_kernel(x_ref, ada_ref, nw_ref, w_ref, qn_ref, kn_ref, cos_ref, sin_ref, q_ref, kv_ref, zs_ref,
                      *row_refs, row_tokens, row_first, per):
    h = _modulated_norm(x_ref, ada_ref, nw_ref)
    tm = h.shape[0]
    cos = cos_ref[...]
    sin = sin_ref[...]
    reps = tm // cos.shape[0]
    if reps > 1:
        cos = jnp.broadcast_to(cos[None], (reps,) + cos.shape).reshape(tm, B_HD)
        sin = jnp.broadcast_to(sin[None], (reps,) + sin.shape).reshape(tm, B_HD)

    def norm_rope(acc, nw):
        outs = []
        for hd in range(acc.shape[1] // B_HD):
            xh = acc[:, hd * B_HD:(hd + 1) * B_HD]
            ms = jnp.mean(xh * xh, axis=-1, keepdims=True)
            y = xh * lax.rsqrt(ms + EPS) * nw
            outs.append(y * cos + pltpu.roll(y, B_HD // 2, 1) * sin)
        return jnp.concatenate(outs, axis=-1)

    for c0 in range(0, w_ref.shape[1], B_GW):
        acc = _dot(h, w_ref[:, c0:c0 + B_GW])
        seg, g = divmod(c0 // B_GW, len(B_GROUPS))
        if seg == 0:
            _put(q_ref, g * B_GW, norm_rope(acc, qn_ref[...]))
        elif seg == 1:
            _put(kv_ref, 2 * g * B_GW, norm_rope(acc, kn_ref[...]))
        elif seg == 2:
            _put(kv_ref, (2 * g + 1) * B_GW, acc)
        else:
            _put(zs_ref, g * B_GW, _silu(acc))

    for g, rows_ref in enumerate(row_refs):
        @pl.when(pl.program_id(0) % per >= row_first[g])
        def _(g=g, rows_ref=rows_ref):
            n = row_tokens[g]
            for j in range(KV_ROWS):
                c0 = 2 * g * B_GW + j * B_HD
                rows_ref[pl.ds(j, n, stride=KV_ROWS), :] = kv_ref[tm - n:tm, c0:c0 + B_HD]


def _l1_inproj(x, ada, nw, w_bf, qn, kn, cos, sin, batch, seq, tm, act_dtype):
    n_tiles, tok_block, ada_block, shape, per = _token_specs(batch, seq, tm)
    out_specs = [tok_block(B_WIDTH), tok_block(2 * B_WIDTH), tok_block(B_WIDTH)]
    out_shape = [jax.ShapeDtypeStruct(shape(B_WIDTH), F32),
                 jax.ShapeDtypeStruct(shape(2 * B_WIDTH), F32),
                 jax.ShapeDtypeStruct(shape(B_WIDTH), act_dtype)]
    row_tokens, row_first = (), ()
    if per is None:
        tab_block = _resident((seq, B_HD))
    else:
        tab_block = pl.BlockSpec((tm, B_HD), lambda i: (i % per, 0))
        for window, _ in B_GROUPS:
            keep = min(window, seq)
            blk = min(keep, tm)
            n_blk = keep // blk
            first = per - n_blk
            out_specs.append(pl.BlockSpec(
                (blk * KV_ROWS, B_HD),
                lambda i, n_blk=n_blk, first=first: ((i // per) * n_blk + jnp.clip(i % per - first, 0, n_blk - 1), 0)))
            out_shape.append(jax.ShapeDtypeStruct((batch * keep * KV_ROWS, B_HD), F32))
            row_tokens += (blk,)
            row_first += (first,)
    return pl.pallas_call(
        functools.partial(_l1_inproj_kernel, row_tokens=row_tokens, row_first=row_first, per=per),
        grid=(n_tiles,),
        in_specs=[tok_block(D_MODEL), ada_block, _resident((1, D_MODEL)), _resident(w_bf.shape),
                  _resident((1, B_HD)), _resident((1, B_HD)), tab_block, tab_block],
        out_specs=out_specs,
        out_shape=out_shape,
        compiler_params=_params(("arbitrary",)),
        name="l1_inproj",
    )(x, ada, nw, w_bf, qn, kn, cos, sin)


Q_SUB = 128
ATTN_TILE = 2048
ATTN_BATCH = 8
MIX_ROWS = 256


def _attend(qs, k2s, v2s, valids):
    scale = B_HD ** -0.5
    ss = [jnp.where(ok, _dot(q, k2, _NT) * scale, -jnp.inf) for q, k2, ok in zip(qs, k2s, valids)]
    ms = [jnp.max(s, axis=-1, keepdims=True) for s in ss]
    ps = [jnp.exp(s - m) for s, m in zip(ss, ms)]
    ls = [jnp.sum(p, axis=-1, keepdims=True) for p in ps]
    os = [_dot(p.astype(BF16), v2) * (1.0 / l) for p, v2, l in zip(ps, v2s, ls)]
    lses = [m + jnp.log(l) for m, l in zip(ms, ls)]
    return os, lses


def _prompt_attn_kernel(*refs):
    n_g = len(B_GROUPS)
    ins, zs_refs = refs[:5 * n_g], refs[5 * n_g:6 * n_g]
    out_refs, (o_scr, lse_scr) = refs[6 * n_g:7 * n_g], refs[7 * n_g:]
    first_key = jnp.where(pl.program_id(2) == 0, Q_SUB, 0)
    row = lax.broadcasted_iota(jnp.int32, (Q_SUB, 2 * Q_SUB), 0)
    col = lax.broadcasted_iota(jnp.int32, (Q_SUB, 2 * Q_SUB), 1)
    band = (col >= row) & (col <= row + Q_SUB)
    band_first = band & (col >= first_key)

    for g, (_, dil) in enumerate(B_GROUPS):
        q_ref, kc_ref, vc_ref, kp_ref, vp_ref = ins[5 * g:5 * g + 5]
        n_sb = ATTN_TILE // (Q_SUB * dil)

        def rows_of(r, first, n):
            start = r + first * dil
            return pl.ds(start, n) if dil == 1 else pl.ds(start, n, stride=dil)

        work = []
        for r in range(dil):
            cls = rows_of(r, 0, n_sb * Q_SUB)
            kr = jnp.concatenate([kp_ref[0, rows_of(r, 0, Q_SUB), :], kc_ref[0, cls, :]], axis=0).astype(BF16)
            vr = jnp.concatenate([vp_ref[0, rows_of(r, 0, Q_SUB), :], vc_ref[0, cls, :]], axis=0).astype(BF16)
            qr = q_ref[0, cls, :].astype(BF16)
            for sb in range(n_sb):
                work.append((rows_of(r, sb * Q_SUB, Q_SUB), qr[sb * Q_SUB:(sb + 1) * Q_SUB],
                             kr[sb * Q_SUB:(sb + 2) * Q_SUB], vr[sb * Q_SUB:(sb + 2) * Q_SUB],
                             band_first if sb == 0 else band))
        for i in range(0, len(work), ATTN_BATCH):
            part = work[i:i + ATTN_BATCH]
            os, lses = _attend([w[1] for w in part], [w[2] for w in part], [w[3] for w in part],
                               [w[4] for w in part])
            for w, o, lse in zip(part, os, lses):
                o_scr[g, w[0], :] = o
                lse_scr[g, w[0], :] = jnp.broadcast_to(lse, (Q_SUB, B_HD))

    for c in range(ATTN_TILE // MIX_ROWS):
        rows = slice(c * MIX_ROWS, (c + 1) * MIX_ROWS)
        lses = [lse_scr[g, rows, :] for g in range(n_g)]
        m = functools.reduce(jnp.maximum, lses)
        es = [jnp.exp(l - m) for l in lses]
        inv = 1.0 / functools.reduce(lambda a, b: a + b, es)
        for g in range(n_g):
            out_refs[g][0, rows, :] = (o_scr[g, rows, :] * (es[g] * inv) * zs_refs[g][0, rows, :]
                                       ).astype(out_refs[g].dtype)


def _prompt_attn(q, kv, zs, batch, seq):
    assert seq % ATTN_TILE == 0
    n_tiles = seq // ATTN_TILE
    tok = lambda c0: pl.BlockSpec((1, ATTN_TILE, B_HD), lambda b, h, i: (b, i, c0 + h))
    in_specs, args = [], []
    for g, (_, dil) in enumerate(B_GROUPS):
        back = Q_SUB * dil
        assert ATTN_TILE % back == 0
        k0, v0 = 2 * g * B_HG, (2 * g + 1) * B_HG
        prev = lambda c0, back=back: pl.BlockSpec(
            (1, back, B_HD), lambda b, h, i: (b, jnp.maximum(i * (ATTN_TILE // back) - 1, 0), c0 + h))
        in_specs += [tok(g * B_HG), tok(k0), tok(v0), prev(k0), prev(v0)]
        args += [q, kv, kv, kv, kv]
    in_specs += [tok(g * B_HG) for g in range(len(B_GROUPS))]
    args += [zs] * len(B_GROUPS)
    out = pl.BlockSpec((1, ATTN_TILE, B_HD), lambda b, h, i: (b, i, h))
    return pl.pallas_call(
        _prompt_attn_kernel,
        grid=(batch, B_HG, n_tiles),
        in_specs=in_specs,
        out_specs=[out] * len(B_GROUPS),
        out_shape=[jax.ShapeDtypeStruct((batch, seq, B_GW), BF16)] * len(B_GROUPS),
        scratch_shapes=[pltpu.VMEM((len(B_GROUPS), ATTN_TILE, B_HD), F32)] * 2,
        compiler_params=_params(("parallel", "parallel", "arbitrary")),
        name="prompt_attn",
    )(*args)


SAMPLE_NB = 2


def _sample_masks(seq):
    hq = np.arange(B_HG * seq)[:, None] // seq
    iq = np.arange(B_HG * seq)[:, None] % seq
    masks = []
    for window, dil in B_GROUPS:
        n_tok = window if dil < seq else (window // dil) * seq
        col = np.arange(n_tok * KV_ROWS)[None, :]
        tok, is_v, head = col // KV_ROWS, (col // B_HG) % 2, col % B_HG
        if dil < seq:
            ok = (tok >= iq) & ((tok - iq) % dil == 0)
        else:
            ok = (tok % seq) == iq
        masks.append(np.where(ok & (is_v == 0) & (head == hq), 0.0, -np.inf).astype(np.float32))
    col = np.arange(B_HG * seq)[None, :]
    new = []
    for _, dil in B_GROUPS:
        ok = (col // seq == hq) & (col % seq <= iq) & ((iq - col % seq) % dil == 0)
        new.append(np.where(ok, 0.0, -np.inf).astype(np.float32))
    return masks, np.stack(new)


def _sample_attn_kernel(q_ref, kv_ref, zs_ref, c0_ref, c1_ref, c2_ref, m0_ref, m1_ref, m2_ref, mn_ref,
                        out_ref, *, seq, nb):
    scale = B_HD ** -0.5
    caches = (c0_ref, c1_ref, c2_ref)
    masks = (m0_ref, m1_ref, m2_ref)
    groups = range(len(B_GROUPS))

    def heads(ref, e, c0):
        return jnp.concatenate([ref[e, :, c0 + h * B_HD:c0 + (h + 1) * B_HD] for h in range(B_HG)],
                               axis=0).astype(BF16)

    def scores(e):
        qa = [heads(q_ref, e, g * B_GW) for g in groups]
        k_new = [heads(kv_ref, e, 2 * g * B_GW) for g in groups]
        rows = [caches[g][e].reshape(-1, B_HD).astype(BF16) for g in groups]
        s_c = [_dot(qa[g], rows[g], _NT) * scale + masks[g][...] for g in groups]
        s_n = [_dot(qa[g], k_new[g], _NT) * scale + mn_ref[g] for g in groups]
        return rows, s_c, s_n

    def weights(s_c, s_n):
        ms = [jnp.maximum(jnp.max(s_c[g], axis=-1, keepdims=True), jnp.max(s_n[g], axis=-1, keepdims=True))
              for g in groups]
        p_c = [jnp.exp(s_c[g] - ms[g]) for g in groups]
        p_n = [jnp.exp(s_n[g] - ms[g]) for g in groups]
        ls = [jnp.sum(p_c[g], axis=-1, keepdims=True) + jnp.sum(p_n[g], axis=-1, keepdims=True) for g in groups]
        p_v = [pltpu.roll(p_c[g], B_HG, 1).astype(BF16) for g in groups]
        return ms, ls, p_v, [p.astype(BF16) for p in p_n]

    def values(e, rows, p_v, p_n):
        v_new = [heads(kv_ref, e, (2 * g + 1) * B_GW) for g in groups]
        return [_dot(p_v[g], rows[g]) + _dot(p_n[g], v_new[g]) for g in groups]

    scored = [scores(e) for e in range(nb)]
    weighted = [weights(s_c, s_n) for _, s_c, s_n in scored]
    accs = [values(e, scored[e][0], weighted[e][2], weighted[e][3]) for e in range(nb)]
    for e in range(nb):
        ms, ls = weighted[e][0], weighted[e][1]
        outs = [accs[e][g] / ls[g] for g in groups]
        lses = [ms[g] + jnp.log(ls[g]) for g in groups]
        m = functools.reduce(jnp.maximum, lses)
        es = [jnp.exp(l - m) for l in lses]
        inv = 1.0 / (es[0] + es[1] + es[2])
        for g in groups:
            mixed = outs[g] * (es[g] * inv)
            for h in range(B_HG):
                cols = slice(g * B_GW + h * B_HD, g * B_GW + (h + 1) * B_HD)
                out_ref[e, :, cols] = mixed[h * seq:(h + 1) * seq] * zs_ref[e, :, cols]


def _sample_attn(q, kv, zs, caches, batch, seq, nb):
    assert batch % nb == 0
    masks, mask_new = _sample_masks(seq)
    specs, views = [], []
    for (window, dil), c in zip(B_GROUPS, caches):
        assert window == dil * (N_KEYS - 1) and c.shape[2] == window
        if dil >= seq:
            views.append(c.reshape(batch, window // dil, dil * KV_ROWS, B_HD))
            specs.append(pl.BlockSpec((nb, window // dil, seq * KV_ROWS, B_HD), lambda b: (b, 0, 0, 0)))
        else:
            assert seq % dil == 0
            views.append(c.reshape(batch, window * KV_ROWS, B_HD))
            specs.append(pl.BlockSpec((nb, window * KV_ROWS, B_HD), lambda b: (b, 0, 0)))
    const = lambda a: pl.BlockSpec(a.shape, lambda b: (0,) * a.ndim)
    tok = lambda w: pl.BlockSpec((nb, seq, w), lambda b: (b, 0, 0))
    return pl.pallas_call(
        functools.partial(_sample_attn_kernel, seq=seq, nb=nb),
        grid=(batch // nb,),
        in_specs=[tok(B_WIDTH), tok(2 * B_WIDTH), tok(B_WIDTH)] + specs
                 + [const(a) for a in masks] + [const(mask_new)],
        out_specs=tok(B_WIDTH),
        out_shape=jax.ShapeDtypeStruct((batch, seq, B_WIDTH), F32),
        compiler_params=_params(("parallel",)),
        name="sample_attn",
    )(q, kv, zs, *views, *[jnp.asarray(a) for a in masks], jnp.asarray(mask_new))


def _trunk(x, ada, start, state, caches, weights, tm_in, tm_out, scan_tile, scan_chunk):
    (norm_w, alb, a_w_in, a_b_f, a_gw, a_w_out, b_w_in, b_qn, b_kn, b_w_out) = weights
    batch, seq, _ = x.shape
    flat = seq >= tm_in
    act = BF16 if flat else F32
    tok = (lambda a: a.reshape(batch * seq, a.shape[-1])) if flat else (lambda a: a)
    rows = lambda a: a.reshape(batch * seq, a.shape[-1])
    ada0 = ada[0].reshape(batch, 1, 3 * D_MODEL)
    ada1 = ada[1].reshape(batch, 1, 3 * D_MODEL)

    x0 = tok(x)
    q, k, lf, v, zs = _l0_inproj(x0, ada0, norm_w[0:1], a_w_in, alb, a_b_f, batch, seq, tm_in, act)
    og, s_new = _hgrn_scan(rows(q), rows(k), rows(lf), rows(v), rows(zs), a_gw, state,
                           batch, seq, scan_tile, scan_chunk, act, n_seq=0 if state is None else SAMPLE_NB)
    og = og if flat else og.reshape(batch, seq, A_V)
    x1 = _outproj([og], a_w_out, x0, ada0, batch, seq, tm_out)

    cos, sin = _rope_tables(start, seq)
    qr, kv, zs1, *new_rows = _l1_inproj(x1, ada1, norm_w[1:2], b_w_in, b_qn, b_kn, cos, sin, batch, seq, tm_in, act)
    kv3 = kv.reshape(batch, seq, 2 * B_WIDTH)
    if caches is None:
        og1 = _prompt_attn(qr.reshape(batch, seq, B_WIDTH), kv3, zs1.reshape(batch, seq, B_WIDTH), batch, seq)
        og1 = [o.reshape(batch * seq, B_GW) for o in og1]
    else:
        og1 = [_sample_attn(qr, kv, zs1, caches, batch, seq, SAMPLE_NB)]
    x2 = _outproj(og1, b_w_out, x1, ada1, batch, seq, tm_out)

    if new_rows:
        kv_rows = [r.reshape(batch, min(window, seq), 2, B_HG, B_HD) for r, (window, _) in zip(new_rows, B_GROUPS)]
    else:
        kv_rows = [kv3[:, seq - min(window, seq):, 2 * g * B_GW:2 * (g + 1) * B_GW]
                   .reshape(batch, min(window, seq), 2, B_HG, B_HD) for g, (window, _) in enumerate(B_GROUPS)]
    return x2.reshape(batch, seq, D_MODEL), s_new, kv_rows


def kernel(x_prompt, x_sample, state_hgrn, cache_kv_w128, cache_kv_w512, cache_kv_w2048, c_prompt, c_sample,
           norm_w, ada_w, ada_b, a_lower_bounds, a_w_in, a_b_f, a_g_norm_w, a_w_out, b_w_in, b_q_norm_w,
           b_k_norm_w, b_w_out):
    bp, lp, _ = x_prompt.shape
    bs, ls, _ = x_sample.shape

    pad = (-(bp + bs)) % 8
    c_all = jnp.concatenate([c_prompt, c_sample, jnp.zeros((pad, D_MODEL), F32)], axis=0)
    ada = _ada_vectors(c_all, ada_w, ada_b)
    ada_p, ada_s = ada[:, :bp], ada[:, bp:bp + bs]

    weights = (norm_w, a_lower_bounds, a_w_in[0].astype(BF16), a_b_f, a_g_norm_w, a_w_out[0].astype(BF16),
               b_w_in[0].astype(BF16), b_q_norm_w, b_k_norm_w, b_w_out[0].astype(BF16))

    y_p, s_p, kv_p = _trunk(x_prompt, ada_p, 0, None, None, weights,
                            tm_in=256, tm_out=512, scan_tile=256, scan_chunk=64)
    caches = (cache_kv_w128, cache_kv_w512, cache_kv_w2048)
    y_s, s_s, kv_s = _trunk(x_sample, ada_s, PAST_LEN, state_hgrn[0], caches, weights,
                            tm_in=256, tm_out=256, scan_tile=ls, scan_chunk=ls)

    kv_out = []
    for g in range(len(B_GROUPS)):
        kv_out.append(kv_p[g][None])
        kv_out.append(kv_s[g][None])
    return (y_p, y_s, s_p[None], s_s[None], *kv_out)
```

```python
import functools
import math

import numpy as np
import jax
import jax.numpy as jnp
from jax import lax
from jax.experimental import pallas as pl
from jax.experimental.pallas import tpu as pltpu

F32 = jnp.float32
BF16 = jnp.bfloat16

D_MODEL = 1024
EPS = 1e-6
A_HEADS = 8
A_KDIM = 128
A_VDIM = 256
A_QK = A_HEADS * A_KDIM
A_V = A_HEADS * A_VDIM
B_GROUPS = ((128, 1), (512, 4), (2048, 16))
B_HG = 4
B_HD = 128
B_GW = B_HG * B_HD
B_WIDTH = len(B_GROUPS) * B_GW
ROPE_THETA = 10000.0
PAST_LEN = 2048
N_KEYS = 129

COL_TILE = 512
VMEM_LIMIT = 56 * 1024 * 1024

_NT = (((1,), (1,)), ((), ()))
_TN = (((0,), (0,)), ((), ()))


def _sigmoid(x):
    return 1.0 / (1.0 + jnp.exp(-x))


def _silu(x):
    h = 0.5 * x
    return h * jnp.tanh(h) + h


def _dot(a, b, dims=None):
    if dims is None:
        return jnp.dot(a, b, preferred_element_type=F32)
    return lax.dot_general(a, b, dims, preferred_element_type=F32)


def _params(sem):
    return pltpu.CompilerParams(dimension_semantics=sem, vmem_limit_bytes=VMEM_LIMIT)


def _ada_kernel(c_ref, w_ref, b_ref, o_ref):
    a = _silu(c_ref[...]).astype(BF16)
    o_ref[0] = _dot(a, w_ref[0].astype(BF16)) + b_ref[0]


def _ada_vectors(c_all, ada_w, ada_b):
    n_layers, _, width = ada_w.shape
    rows = c_all.shape[0]
    return pl.pallas_call(
        _ada_kernel,
        grid=(n_layers, width // COL_TILE),
        in_specs=[
            pl.BlockSpec((rows, D_MODEL), lambda l, j: (0, 0)),
            pl.BlockSpec((1, D_MODEL, COL_TILE), lambda l, j: (l, 0, j)),
            pl.BlockSpec((1, 1, COL_TILE), lambda l, j: (l, 0, j)),
        ],
        out_specs=pl.BlockSpec((1, rows, COL_TILE), lambda l, j: (l, 0, j)),
        out_shape=jax.ShapeDtypeStruct((n_layers, rows, width), F32),
        compiler_params=_params(("parallel", "parallel")),
        name="ada_vectors",
    )(c_all, ada_w, ada_b.reshape(n_layers, 1, width))


def _modulated_norm(x_ref, ada_ref, nw_ref):
    x = x_ref[...]
    ada = ada_ref[...]
    if x.ndim == 2:
        ada = ada.reshape(1, ada.shape[-1])
    shift = ada[..., :D_MODEL]
    scale = ada[..., D_MODEL:2 * D_MODEL]
    ms = jnp.mean(x * x, axis=-1, keepdims=True)
    y = x * lax.rsqrt(ms + EPS) * nw_ref[...].reshape((1,) * (x.ndim - 1) + (D_MODEL,))
    h = y * (1.0 + scale) + shift
    return h.reshape(-1, D_MODEL).astype(BF16)


def _token_specs(batch, seq, tm):
    if seq >= tm:
        assert seq % tm == 0
        per = seq // tm
        n_tiles = batch * per
        tok_block = lambda w: pl.BlockSpec((tm, w), lambda i: (i, 0))
        ada_block = pl.BlockSpec((1, 1, 3 * D_MODEL), lambda i: (i // per, 0, 0))
        shape = lambda w: (batch * seq, w)
        return n_tiles, tok_block, ada_block, shape, per
    tb = tm // seq
    assert batch % tb == 0
    n_tiles = batch // tb
    tok_block = lambda w: pl.BlockSpec((tb, seq, w), lambda i: (i, 0, 0))
    ada_block = pl.BlockSpec((tb, 1, 3 * D_MODEL), lambda i: (i, 0, 0))
    shape = lambda w: (batch, seq, w)
    return n_tiles, tok_block, ada_block, shape, None


def _resident(shape):
    return pl.BlockSpec(shape, lambda i: (0,) * len(shape), pipeline_mode=pl.Buffered(1))


def _put(ref, c0, val):
    ref[(Ellipsis, slice(c0, c0 + val.shape[-1]))] = val.reshape(ref.shape[:-1] + val.shape[-1:]).astype(ref.dtype)


def _l0_inproj_kernel(x_ref, ada_ref, nw_ref, w_ref, alb_ref, bf_ref, q_ref, k_ref, lf_ref, v_ref, zs_ref):
    h = _modulated_norm(x_ref, ada_ref, nw_ref)
    a = alb_ref[...]
    m = jnp.max(a, axis=0, keepdims=True)
    e = jnp.exp(a - m)
    lb = e[0:1] / jnp.sum(e, axis=0, keepdims=True)
    for c0 in range(0, w_ref.shape[1], COL_TILE):
        acc = _dot(h, w_ref[:, c0:c0 + COL_TILE].astype(BF16))
        if c0 < A_QK:
            _put(q_ref, c0, _silu(acc))
        elif c0 < 2 * A_QK:
            c = c0 - A_QK
            f = lb[:, c:c + COL_TILE] + (1.0 - lb[:, c:c + COL_TILE]) * _sigmoid(acc + bf_ref[:, c:c + COL_TILE])
            _put(lf_ref, c, jnp.log(f))
            _put(k_ref, c, 1.0 - f)
        elif c0 < 2 * A_QK + A_V:
            _put(v_ref, c0 - 2 * A_QK, acc)
        else:
            _put(zs_ref, c0 - 2 * A_QK - A_V, _silu(acc))


def _l0_inproj(x, ada, nw, w_bf, alb, b_f, batch, seq, tm, act_dtype):
    n_tiles, tok_block, ada_block, shape, _ = _token_specs(batch, seq, tm)
    return pl.pallas_call(
        _l0_inproj_kernel,
        grid=(n_tiles,),
        in_specs=[tok_block(D_MODEL), ada_block, _resident((1, D_MODEL)), _resident(w_bf.shape),
                  _resident(alb.shape), _resident(b_f.shape)],
        out_specs=[tok_block(A_QK), tok_block(A_QK), tok_block(A_QK), tok_block(A_V), tok_block(A_V)],
        out_shape=[
            jax.ShapeDtypeStruct(shape(A_QK), F32),
            jax.ShapeDtypeStruct(shape(A_QK), F32),
            jax.ShapeDtypeStruct(shape(A_QK), F32),
            jax.ShapeDtypeStruct(shape(A_V), act_dtype),
            jax.ShapeDtypeStruct(shape(A_V), act_dtype),
        ],
        compiler_params=_params(("parallel",)),
        name="l0_inproj",
    )(x, ada, nw, w_bf, alb, b_f)


def _scan_levels(chunk):
    return [1 << l for l in range(int(math.log2(chunk)))]


def _exponent_matrix(chunk):
    r = np.arange(chunk)[:, None]
    c = np.arange(chunk)[None, :]
    blocks = [c <= r, c > r]
    for s in _scan_levels(chunk):
        bound = (r & ~(2 * s - 1)) | s
        upper = (r & s) != 0
        blocks.append(np.where(upper, (c > bound) & (c <= r), (c > r) & (c <= bound)))
    return np.concatenate(blocks, axis=0).astype(np.float32)


def _hgrn_chunk(q, k, lf, v, state, emat, chunk):
    dk = q.shape[-1]
    ex = jnp.exp(jnp.dot(emat, lf, preferred_element_type=F32, precision=lax.Precision.HIGHEST))
    e_b = ex[0:chunk]
    e_u = ex[chunk:2 * chunk]
    row = lax.broadcasted_iota(jnp.int32, (chunk, 1), 0)
    col = lax.broadcasted_iota(jnp.int32, (1, chunk), 1)

    att = jnp.where(row == col, jnp.sum(q * k, axis=-1, keepdims=True), 0.0)
    for l, s in enumerate(_scan_levels(chunk)):
        g = ex[(2 + l) * chunk:(3 + l) * chunk]
        upper = (row & s) != 0
        xb = (jnp.where(upper, q, k) * g).astype(BF16)
        part = _dot(xb, xb, _NT)
        keep = upper & ((col & s) == 0) & ((row >> (l + 1)) == (col >> (l + 1)))
        att = att + jnp.where(keep, part, 0.0)

    o = _dot(att.astype(BF16), v.astype(BF16)) + _dot((q * e_b).astype(BF16), state.astype(BF16))

    e_last = e_b[chunk - 1:chunk]
    eye = lax.broadcasted_iota(jnp.int32, (dk, dk), 0) == lax.broadcasted_iota(jnp.int32, (dk, dk), 1)
    e_col = jnp.sum(jnp.where(eye, e_last, 0.0), axis=-1, keepdims=True)
    new_state = state * e_col + _dot((k * e_u).astype(BF16), v.astype(BF16), _TN)
    return o, new_state


SAFE_DECAY = 60.0
SPLIT_PAD = 32
SCAN_LINK = 2
SCAN_SEQS = 4


def _split_prefix_matrix(chunk, n_seq):
    t = np.kron(np.eye(n_seq, dtype=np.float32), _exponent_matrix(chunk)[:chunk])
    pad = (-3 * t.shape[1]) % SPLIT_PAD
    return np.concatenate([t, t, t, np.zeros((t.shape[0], pad), np.float32)], axis=1)


def _hgrn_chunk_bounded(q, k, lf, v, states, tmat, chunk, n_seq, linked):
    n_rows = n_seq * chunk
    hi = lf.astype(BF16)
    r1 = lf - hi.astype(F32)
    mid = r1.astype(BF16)
    lo = r1 - mid.astype(F32)
    parts = [hi.astype(F32), mid.astype(F32), lo]
    pad = tmat.shape[1] - 3 * n_rows
    if pad:
        parts.append(jnp.zeros((pad, lf.shape[1]), F32))
    b = _dot(tmat, jnp.concatenate(parts, axis=0).astype(BF16))
    e_b = jnp.exp(b)
    qb = q * e_b
    kn_f = k * jnp.exp(-b)
    kn = kn_f.astype(BF16)
    row = lax.broadcasted_iota(jnp.int32, (n_rows, n_rows), 0)
    col = lax.broadcasted_iota(jnp.int32, (n_rows, n_rows), 1)
    causal = row >= col
    if n_seq > 1:
        shift = chunk.bit_length() - 1
        causal = causal & ((row >> shift) == (col >> shift))
    eye = lax.broadcasted_iota(jnp.int32, (A_KDIM, A_KDIM), 0) == lax.broadcasted_iota(jnp.int32, (A_KDIM, A_KDIM), 1)
    ks = [slice(h * A_KDIM, (h + 1) * A_KDIM) for h in range(A_HEADS)]
    vs = [slice(h * A_VDIM, (h + 1) * A_VDIM) for h in range(A_HEADS)]
    sq = [slice(s * chunk, (s + 1) * chunk) for s in range(n_seq)]
    qbb, vb = qb.astype(BF16), v.astype(BF16)
    qb_s = [qbb] if n_seq == 1 else [qb[r].astype(BF16) for r in sq]
    ku_s = [(kn_f[r] * e_b[r.stop - 1:r.stop]).astype(BF16) for r in sq]
    v_s = [vb] if n_seq == 1 else [v[r].astype(BF16) for r in sq]
    atts = [jnp.where(causal, _dot(qbb[:, ks[h]], kn[:, ks[h]], _NT), 0.0).astype(BF16) for h in range(A_HEADS)]
    e_cols = [[jnp.sum(jnp.where(eye, e_b[(s + 1) * chunk - 1:(s + 1) * chunk, ks[h]], 0.0), axis=-1, keepdims=True)
               for h in range(A_HEADS)] for s in range(n_seq)]
    new_states = [[None] * A_HEADS for _ in range(n_seq)]
    carried = [[None] * A_HEADS for _ in range(n_seq)]
    for s in range(n_seq):
        for h in range(A_HEADS):
            state = new_states[s - 1][h] if (linked and s > 0) else states[0 if linked else s][h]
            carried[s][h] = _dot(qb_s[s][:, ks[h]], state.astype(BF16))
            new_states[s][h] = state * e_cols[s][h] + _dot(ku_s[s][:, ks[h]], v_s[s][:, vs[h]], _TN)
    outs = []
    for h in range(A_HEADS):
        inter = carried[0][h] if n_seq == 1 else jnp.concatenate([carried[s][h] for s in range(n_seq)], axis=0)
        outs.append(_dot(atts[h], vb[:, vs[h]]) + inter)
    return outs, new_states


def _hgrn_scan_kernel(*refs, chunk, n_chunks, has_state, n_seq, n_link):
    one_shot = n_seq > 0
    if has_state:
        q_ref, k_ref, lf_ref, v_ref, zs_ref, gw_ref, e_ref, t_ref, s0_ref, og_ref, so_ref, s_scr = refs
    else:
        q_ref, k_ref, lf_ref, v_ref, zs_ref, gw_ref, e_ref, t_ref, og_ref, so_ref, s_scr = refs
    t = pl.program_id(1)

    if not one_shot:
        @pl.when(t == 0)
        def _():
            if has_state:
                s_scr[...] = s0_ref[0]
            else:
                s_scr[...] = jnp.zeros_like(s_scr)

    def load_state(s, h):
        return s0_ref[s, h] if one_shot else s_scr[h]

    def store_state(s, h, val):
        if one_shot:
            so_ref[s, h] = val
        else:
            s_scr[h] = val

    gw = gw_ref[...]

    def finish(rows, h, o):
        vc = slice(h * A_VDIM, (h + 1) * A_VDIM)
        ms = jnp.mean(o * o, axis=-1, keepdims=True)
        og_ref[rows, vc] = (o * lax.rsqrt(ms + EPS) * gw * zs_ref[rows, vc]).astype(og_ref.dtype)

    seqs = range(max(n_seq, 1))

    def chunk_rows(c, s=None, n=1):
        if n_chunks > 1:
            return pl.ds(pl.multiple_of(c * (n * chunk), n * chunk), n * chunk)
        return slice(0, len(seqs) * chunk) if s is None else slice(s * chunk, (s + 1) * chunk)

    def bounded_body(c, carry):
        rows = chunk_rows(c, n=n_link)
        states = [[load_state(s, h) for h in range(A_HEADS)] for s in seqs]
        n_at_once = len(seqs) if one_shot else n_link
        outs, new_states = _hgrn_chunk_bounded(q_ref[rows, :], k_ref[rows, :], lf_ref[rows, :], v_ref[rows, :],
                                               states, t_ref[...], chunk, n_at_once, linked=not one_shot)
        for h in range(A_HEADS):
            if one_shot:
                for s in seqs:
                    store_state(s, h, new_states[s][h])
            else:
                store_state(0, h, new_states[-1][h])
            finish(rows, h, outs[h])
        return carry

    def general_body(c, carry):
        for s in seqs:
            rows = chunk_rows(c, s)
            for h in range(A_HEADS):
                kc = slice(h * A_KDIM, (h + 1) * A_KDIM)
                vc = slice(h * A_VDIM, (h + 1) * A_VDIM)
                o, s_new = _hgrn_chunk(q_ref[rows, kc], k_ref[rows, kc], lf_ref[rows, kc],
                                       v_ref[rows, vc], load_state(s, h), e_ref[...], chunk)
                store_state(s, h, s_new)
                finish(rows, h, o)
        return carry

    def run(body, trips):
        def go():
            if trips == 1:
                body(0, 0)
            else:
                lax.fori_loop(0, trips, body, 0)
        return go

    bounded = jnp.min(lf_ref[...]) * chunk >= -SAFE_DECAY
    lax.cond(bounded, run(bounded_body, n_chunks // n_link), run(general_body, n_chunks))

    if not one_shot:
        @pl.when(t == pl.num_programs(1) - 1)
        def _():
            so_ref[0] = s_scr[...]


def _hgrn_scan(q, k, lf, v, zs, gw, s0, batch, seq, tile, chunk, out_dtype, n_seq=0):
    n_t = seq // tile
    per_step = max(n_seq, 1)
    assert batch % per_step == 0 and (n_seq == 0 or (s0 is not None and seq == tile == chunk))
    n_link = SCAN_LINK if (n_seq == 0 and (tile // chunk) % SCAN_LINK == 0) else 1
    emat = jnp.asarray(_exponent_matrix(chunk))
    tmat = jnp.asarray(_split_prefix_matrix(chunk, max(per_step, n_link)), dtype=BF16)
    row_block = lambda w: pl.BlockSpec((tile * per_step, w), lambda b, t: (b * n_t + t, 0))
    state_block = pl.BlockSpec((per_step, A_HEADS, A_KDIM, A_VDIM), lambda b, t: (b, 0, 0, 0))
    in_specs = [row_block(A_QK), row_block(A_QK), row_block(A_QK), row_block(A_V), row_block(A_V),
                pl.BlockSpec((1, A_VDIM), lambda b, t: (0, 0)),
                pl.BlockSpec(emat.shape, lambda b, t: (0, 0)),
                pl.BlockSpec(tmat.shape, lambda b, t: (0, 0))]
    args = [q, k, lf, v, zs, gw, emat, tmat]
    if s0 is not None:
        in_specs.append(state_block)
        args.append(s0)
    kern = functools.partial(_hgrn_scan_kernel, chunk=chunk, n_chunks=tile // chunk,
                             has_state=s0 is not None, n_seq=n_seq, n_link=n_link)
    return pl.pallas_call(
        kern,
        grid=(batch // per_step, n_t),
        in_specs=in_specs,
        out_specs=[row_block(A_V), state_block],
        out_shape=[jax.ShapeDtypeStruct((batch * seq, A_V), out_dtype),
                   jax.ShapeDtypeStruct((batch, A_HEADS, A_KDIM, A_VDIM), F32)],
        scratch_shapes=[pltpu.VMEM((A_HEADS, A_KDIM, A_VDIM), F32)],
        compiler_params=_params(("parallel", "arbitrary")),
        name="hgrn_scan",
    )(*args)


def _outproj_kernel(*refs):
    *g_refs, w_ref, x_ref, ada_ref, o_ref = refs
    gs = [g_ref[...] for g_ref in g_refs]
    gs = [g.reshape(-1, g.shape[-1]).astype(BF16) for g in gs]
    ada = ada_ref[...]
    if len(x_ref.shape) == 2:
        ada = ada.reshape(1, ada.shape[-1])
    for c0 in range(0, D_MODEL, COL_TILE):
        cols = (Ellipsis, slice(c0, c0 + COL_TILE))
        y, k0 = None, 0
        for g in gs:
            part = _dot(g, w_ref[k0:k0 + g.shape[1], c0:c0 + COL_TILE].astype(BF16))
            y = part if y is None else y + part
            k0 += g.shape[1]
        gate = ada[..., 2 * D_MODEL + c0:2 * D_MODEL + c0 + COL_TILE]
        o_ref[cols] = x_ref[cols] + gate * y.reshape(x_ref.shape[:-1] + (COL_TILE,))


def _outproj(gs, w_bf, x, ada, batch, seq, tm):
    n_tiles, tok_block, ada_block, shape, _ = _token_specs(batch, seq, tm)
    assert sum(g.shape[-1] for g in gs) == w_bf.shape[0]
    return pl.pallas_call(
        _outproj_kernel,
        grid=(n_tiles,),
        in_specs=[tok_block(g.shape[-1]) for g in gs] + [_resident(w_bf.shape), tok_block(D_MODEL), ada_block],
        out_specs=tok_block(D_MODEL),
        out_shape=jax.ShapeDtypeStruct(shape(D_MODEL), F32),
        compiler_params=_params(("parallel",)),
        name="outproj",
    )(*gs, w_bf, x, ada)


def _rope_table_kernel(pos_ref, inv_ref, cos_ref, sin_ref):
    ang = pos_ref[...] * inv_ref[...]
    lane = lax.broadcasted_iota(jnp.int32, ang.shape, 1)
    cos_ref[...] = jnp.cos(ang)
    sin_ref[...] = jnp.where(lane < B_HD // 2, -jnp.sin(ang), jnp.sin(ang))


def _rope_tables(start, seq):
    half = B_HD // 2
    inv = ROPE_THETA ** (-jnp.arange(half, dtype=F32) / half)
    inv2 = jnp.concatenate([inv, inv]).reshape(1, B_HD)
    pos = (start + jnp.arange(seq, dtype=jnp.int32)).astype(F32).reshape(seq, 1)
    return pl.pallas_call(
        _rope_table_kernel,
        out_shape=[jax.ShapeDtypeStruct((seq, B_HD), F32)] * 2,
        name="rope_tables",
    )(pos, inv2)


KV_ROWS = 2 * B_HG


def _l1_inproj_kernel(x_ref, ada_ref, nw_ref, w_ref, qn_ref, kn_ref, cos_ref, sin_ref, q_ref, kv_ref, zs_ref,
                      *row_refs, row_tokens, row_first, per):
    h = _modulated_norm(x_ref, ada_ref, nw_ref)
    tm = h.shape[0]
    cos = cos_ref[...]
    sin = sin_ref[...]
    reps = tm // cos.shape[0]
    if reps > 1:
        cos = jnp.broadcast_to(cos[None], (reps,) + cos.shape).reshape(tm, B_HD)
        sin = jnp.broadcast_to(sin[None], (reps,) + sin.shape).reshape(tm, B_HD)

    def norm_rope(acc, nw):
        outs = []
        for hd in range(acc.shape[1] // B_HD):
            xh = acc[:, hd * B_HD:(hd + 1) * B_HD]
            ms = jnp.mean(xh * xh, axis=-1, keepdims=True)
            y = xh * lax.rsqrt(ms + EPS) * nw
            outs.append(y * cos + pltpu.roll(y, B_HD // 2, 1) * sin)
        return jnp.concatenate(outs, axis=-1)

    for c0 in range(0, w_ref.shape[1], B_GW):
        acc = _dot(h, w_ref[:, c0:c0 + B_GW].astype(BF16))
        seg, g = divmod(c0 // B_GW, len(B_GROUPS))
        if seg == 0:
            _put(q_ref, g * B_GW, norm_rope(acc, qn_ref[...]))
        elif seg == 1:
            _put(kv_ref, 2 * g * B_GW, norm_rope(acc, kn_ref[...]))
        elif seg == 2:
            _put(kv_ref, (2 * g + 1) * B_GW, acc)
        else:
            _put(zs_ref, g * B_GW, _silu(acc))

    for g, rows_ref in enumerate(row_refs):
        @pl.when(pl.program_id(0) % per >= row_first[g])
        def _(g=g, rows_ref=rows_ref):
            n = row_tokens[g]
            for j in range(KV_ROWS):
                c0 = 2 * g * B_GW + j * B_HD
                rows_ref[pl.ds(j, n, stride=KV_ROWS), :] = kv_ref[tm - n:tm, c0:c0 + B_HD]


def _l1_inproj(x, ada, nw, w_bf, qn, kn, cos, sin, batch, seq, tm, act_dtype):
    n_tiles, tok_block, ada_block, shape, per = _token_specs(batch, seq, tm)
    out_specs = [tok_block(B_WIDTH), tok_block(2 * B_WIDTH), tok_block(B_WIDTH)]
    out_shape = [jax.ShapeDtypeStruct(shape(B_WIDTH), F32),
                 jax.ShapeDtypeStruct(shape(2 * B_WIDTH), F32),
                 jax.ShapeDtypeStruct(shape(B_WIDTH), act_dtype)]
    row_tokens, row_first = (), ()
    if per is None:
        tab_block = _resident((seq, B_HD))
    else:
        tab_block = pl.BlockSpec((tm, B_HD), lambda i: (i % per, 0))
        for window, _ in B_GROUPS:
            keep = min(window, seq)
            blk = min(keep, tm)
            n_blk = keep // blk
            first = per - n_blk
            out_specs.append(pl.BlockSpec(
                (blk * KV_ROWS, B_HD),
                lambda i, n_blk=n_blk, first=first: ((i // per) * n_blk + jnp.clip(i % per - first, 0, n_blk - 1), 0)))
            out_shape.append(jax.ShapeDtypeStruct((batch * keep * KV_ROWS, B_HD), F32))
            row_tokens += (blk,)
            row_first += (first,)
    return pl.pallas_call(
        functools.partial(_l1_inproj_kernel, row_tokens=row_tokens, row_first=row_first, per=per),
        grid=(n_tiles,),
        in_specs=[tok_block(D_MODEL), ada_block, _resident((1, D_MODEL)), _resident(w_bf.shape),
                  _resident((1, B_HD)), _resident((1, B_HD)), tab_block, tab_block],
        out_specs=out_specs,
        out_shape=out_shape,
        compiler_params=_params(("arbitrary",)),
        name="l1_inproj",
    )(x, ada, nw, w_bf, qn, kn, cos, sin)


Q_SUB = 128
ATTN_TILE = 2048
ATTN_BATCH = 8
MIX_ROWS = 256


def _attend(qs, k2s, v2s, valids):
    scale = B_HD ** -0.5
    ss = [jnp.where(ok, _dot(q, k2, _NT) * scale, -jnp.inf) for q, k2, ok in zip(qs, k2s, valids)]
    ms = [jnp.max(s, axis=-1, keepdims=True) for s in ss]
    ps = [jnp.exp(s - m) for s, m in zip(ss, ms)]
    ls = [jnp.sum(p, axis=-1, keepdims=True) for p in ps]
    os = [_dot(p.astype(BF16), v2) * (1.0 / l) for p, v2, l in zip(ps, v2s, ls)]
    lses = [m + jnp.log(l) for m, l in zip(ms, ls)]
    return os, lses


def _prompt_attn_kernel(*refs):
    n_g = len(B_GROUPS)
    ins, zs_refs = refs[:5 * n_g], refs[5 * n_g:6 * n_g]
    out_refs, (o_scr, lse_scr) = refs[6 * n_g:7 * n_g], refs[7 * n_g:]
    first_key = jnp.where(pl.program_id(2) == 0, Q_SUB, 0)
    row = lax.broadcasted_iota(jnp.int32, (Q_SUB, 2 * Q_SUB), 0)
    col = lax.broadcasted_iota(jnp.int32, (Q_SUB, 2 * Q_SUB), 1)
    band = (col >= row) & (col <= row + Q_SUB)
    band_first = band & (col >= first_key)

    for g, (_, dil) in enumerate(B_GROUPS):
        q_ref, kc_ref, vc_ref, kp_ref, vp_ref = ins[5 * g:5 * g + 5]
        n_sb = ATTN_TILE // (Q_SUB * dil)

        def rows_of(r, first, n):
            start = r + first * dil
            return pl.ds(start, n) if dil == 1 else pl.ds(start, n, stride=dil)

        work = []
        for r in range(dil):
            cls = rows_of(r, 0, n_sb * Q_SUB)
            kr = jnp.concatenate([kp_ref[0, rows_of(r, 0, Q_SUB), :], kc_ref[0, cls, :]], axis=0).astype(BF16)
            vr = jnp.concatenate([vp_ref[0, rows_of(r, 0, Q_SUB), :], vc_ref[0, cls, :]], axis=0).astype(BF16)
            qr = q_ref[0, cls, :].astype(BF16)
            for sb in range(n_sb):
                work.append((rows_of(r, sb * Q_SUB, Q_SUB), qr[sb * Q_SUB:(sb + 1) * Q_SUB],
                             kr[sb * Q_SUB:(sb + 2) * Q_SUB], vr[sb * Q_SUB:(sb + 2) * Q_SUB],
                             band_first if sb == 0 else band))
        for i in range(0, len(work), ATTN_BATCH):
            part = work[i:i + ATTN_BATCH]
            os, lses = _attend([w[1] for w in part], [w[2] for w in part], [w[3] for w in part],
                               [w[4] for w in part])
            for w, o, lse in zip(part, os, lses):
                o_scr[g, w[0], :] = o
                lse_scr[g, w[0], :] = jnp.broadcast_to(lse, (Q_SUB, B_HD))

    for c in range(ATTN_TILE // MIX_ROWS):
        rows = slice(c * MIX_ROWS, (c + 1) * MIX_ROWS)
        lses = [lse_scr[g, rows, :] for g in range(n_g)]
        m = functools.reduce(jnp.maximum, lses)
        es = [jnp.exp(l - m) for l in lses]
        inv = 1.0 / functools.reduce(lambda a, b: a + b, es)
        for g in range(n_g):
            out_refs[g][0, rows, :] = (o_scr[g, rows, :] * (es[g] * inv) * zs_refs[g][0, rows, :]
                                       ).astype(out_refs[g].dtype)


def _prompt_attn(q, kv, zs, batch, seq):
    assert seq % ATTN_TILE == 0
    n_tiles = seq // ATTN_TILE
    tok = lambda c0: pl.BlockSpec((1, ATTN_TILE, B_HD), lambda b, h, i: (b, i, c0 + h))
    in_specs, args = [], []
    for g, (_, dil) in enumerate(B_GROUPS):
        back = Q_SUB * dil
        assert ATTN_TILE % back == 0
        k0, v0 = 2 * g * B_HG, (2 * g + 1) * B_HG
        prev = lambda c0, back=back: pl.BlockSpec(
            (1, back, B_HD), lambda b, h, i: (b, jnp.maximum(i * (ATTN_TILE // back) - 1, 0), c0 + h))
        in_specs += [tok(g * B_HG), tok(k0), tok(v0), prev(k0), prev(v0)]
        args += [q, kv, kv, kv, kv]
    in_specs += [tok(g * B_HG) for g in range(len(B_GROUPS))]
    args += [zs] * len(B_GROUPS)
    out = pl.BlockSpec((1, ATTN_TILE, B_HD), lambda b, h, i: (b, i, h))
    return pl.pallas_call(
        _prompt_attn_kernel,
        grid=(batch, B_HG, n_tiles),
        in_specs=in_specs,
        out_specs=[out] * len(B_GROUPS),
        out_shape=[jax.ShapeDtypeStruct((batch, seq, B_GW), BF16)] * len(B_GROUPS),
        scratch_shapes=[pltpu.VMEM((len(B_GROUPS), ATTN_TILE, B_HD), F32)] * 2,
        compiler_params=_params(("parallel", "parallel", "arbitrary")),
        name="prompt_attn",
    )(*args)


SAMPLE_NB = 2


def _sample_masks(seq):
    hq = np.arange(B_HG * seq)[:, None] // seq
    iq = np.arange(B_HG * seq)[:, None] % seq
    masks = []
    for window, dil in B_GROUPS:
        n_tok = window if dil < seq else (window // dil) * seq
        col = np.arange(n_tok * KV_ROWS)[None, :]
        tok, is_v, head = col // KV_ROWS, (col // B_HG) % 2, col % B_HG
        if dil < seq:
            ok = (tok >= iq) & ((tok - iq) % dil == 0)
        else:
            ok = (tok % seq) == iq
        masks.append(np.where(ok & (is_v == 0) & (head == hq), 0.0, -np.inf).astype(np.float32))
    col = np.arange(B_HG * seq)[None, :]
    new = []
    for _, dil in B_GROUPS:
        ok = (col // seq == hq) & (col % seq <= iq) & ((iq - col % seq) % dil == 0)
        new.append(np.where(ok, 0.0, -np.inf).astype(np.float32))
    return masks, np.stack(new)


def _sample_attn_kernel(q_ref, kv_ref, zs_ref, c0_ref, c1_ref, c2_ref, m0_ref, m1_ref, m2_ref, mn_ref,
                        out_ref, *, seq, nb):
    scale = B_HD ** -0.5
    caches = (c0_ref, c1_ref, c2_ref)
    masks = (m0_ref, m1_ref, m2_ref)
    groups = range(len(B_GROUPS))

    def heads(ref, e, c0):
        return jnp.concatenate([ref[e, :, c0 + h * B_HD:c0 + (h + 1) * B_HD] for h in range(B_HG)],
                               axis=0).astype(BF16)

    def scores(e):
        qa = [heads(q_ref, e, g * B_GW) for g in groups]
        k_new = [heads(kv_ref, e, 2 * g * B_GW) for g in groups]
        rows = [caches[g][e].reshape(-1, B_HD).astype(BF16) for g in groups]
        s_c = [_dot(qa[g], rows[g], _NT) * scale + masks[g][...] for g in groups]
        s_n = [_dot(qa[g], k_new[g], _NT) * scale + mn_ref[g] for g in groups]
        return rows, s_c, s_n

    def weights(s_c, s_n):
        ms = [jnp.maximum(jnp.max(s_c[g], axis=-1, keepdims=True), jnp.max(s_n[g], axis=-1, keepdims=True))
              for g in groups]
        p_c = [jnp.exp(s_c[g] - ms[g]) for g in groups]
        p_n = [jnp.exp(s_n[g] - ms[g]) for g in groups]
        ls = [jnp.sum(p_c[g], axis=-1, keepdims=True) + jnp.sum(p_n[g], axis=-1, keepdims=True) for g in groups]
        p_v = [pltpu.roll(p_c[g], B_HG, 1).astype(BF16) for g in groups]
        return ms, ls, p_v, [p.astype(BF16) for p in p_n]

    def values(e, rows, p_v, p_n):
        v_new = [heads(kv_ref, e, (2 * g + 1) * B_GW) for g in groups]
        return [_dot(p_v[g], rows[g]) + _dot(p_n[g], v_new[g]) for g in groups]

    scored = [scores(e) for e in range(nb)]
    weighted = [weights(s_c, s_n) for _, s_c, s_n in scored]
    accs = [values(e, scored[e][0], weighted[e][2], weighted[e][3]) for e in range(nb)]
    for e in range(nb):
        ms, ls = weighted[e][0], weighted[e][1]
        outs = [accs[e][g] / ls[g] for g in groups]
        lses = [ms[g] + jnp.log(ls[g]) for g in groups]
        m = functools.reduce(jnp.maximum, lses)
        es = [jnp.exp(l - m) for l in lses]
        inv = 1.0 / (es[0] + es[1] + es[2])
        for g in groups:
            mixed = outs[g] * (es[g] * inv)
            for h in range(B_HG):
                cols = slice(g * B_GW + h * B_HD, g * B_GW + (h + 1) * B_HD)
                out_ref[e, :, cols] = mixed[h * seq:(h + 1) * seq] * zs_ref[e, :, cols]


def _sample_attn(q, kv, zs, caches, batch, seq, nb):
    assert batch % nb == 0
    masks, mask_new = _sample_masks(seq)
    specs, views = [], []
    for (window, dil), c in zip(B_GROUPS, caches):
        assert window == dil * (N_KEYS - 1) and c.shape[2] == window
        if dil >= seq:
            views.append(c.reshape(batch, window // dil, dil * KV_ROWS, B_HD))
            specs.append(pl.BlockSpec((nb, window // dil, seq * KV_ROWS, B_HD), lambda b: (b, 0, 0, 0)))
        else:
            assert seq % dil == 0
            views.append(c.reshape(batch, window * KV_ROWS, B_HD))
            specs.append(pl.BlockSpec((nb, window * KV_ROWS, B_HD), lambda b: (b, 0, 0)))
    const = lambda a: pl.BlockSpec(a.shape, lambda b: (0,) * a.ndim)
    tok = lambda w: pl.BlockSpec((nb, seq, w), lambda b: (b, 0, 0))
    return pl.pallas_call(
        functools.partial(_sample_attn_kernel, seq=seq, nb=nb),
        grid=(batch // nb,),
        in_specs=[tok(B_WIDTH), tok(2 * B_WIDTH), tok(B_WIDTH)] + specs
                 + [const(a) for a in masks] + [const(mask_new)],
        out_specs=tok(B_WIDTH),
        out_shape=jax.ShapeDtypeStruct((batch, seq, B_WIDTH), F32),
        compiler_params=_params(("parallel",)),
        name="sample_attn",
    )(q, kv, zs, *views, *[jnp.asarray(a) for a in masks], jnp.asarray(mask_new))


def _trunk(x, ada, start, state, caches, weights, tm_in, tm_out, scan_tile, scan_chunk):
    (norm_w, alb, a_w_in, a_b_f, a_gw, a_w_out, b_w_in, b_qn, b_kn, b_w_out) = weights
    batch, seq, _ = x.shape
    flat = seq >= tm_in
    act = BF16 if flat else F32
    tok = (lambda a: a.reshape(batch * seq, a.shape[-1])) if flat else (lambda a: a)
    rows = lambda a: a.reshape(batch * seq, a.shape[-1])
    ada0 = ada[0].reshape(batch, 1, 3 * D_MODEL)
    ada1 = ada[1].reshape(batch, 1, 3 * D_MODEL)

    x0 = tok(x)
    q, k, lf, v, zs = _l0_inproj(x0, ada0, norm_w[0:1], a_w_in, alb, a_b_f, batch, seq, tm_in, act)
    og, s_new = _hgrn_scan(rows(q), rows(k), rows(lf), rows(v), rows(zs), a_gw, state,
                           batch, seq, scan_tile, scan_chunk, act, n_seq=0 if state is None else SCAN_SEQS)
    og = og if flat else og.reshape(batch, seq, A_V)
    x1 = _outproj([og], a_w_out, x0, ada0, batch, seq, tm_out)

    cos, sin = _rope_tables(start, seq)
    qr, kv, zs1, *new_rows = _l1_inproj(x1, ada1, norm_w[1:2], b_w_in, b_qn, b_kn, cos, sin, batch, seq, tm_in, act)
    kv3 = kv.reshape(batch, seq, 2 * B_WIDTH)
    if caches is None:
        og1 = _prompt_attn(qr.reshape(batch, seq, B_WIDTH), kv3, zs1.reshape(batch, seq, B_WIDTH), batch, seq)
        og1 = [o.reshape(batch * seq, B_GW) for o in og1]
    else:
        og1 = [_sample_attn(qr, kv, zs1, caches, batch, seq, SAMPLE_NB)]
    x2 = _outproj(og1, b_w_out, x1, ada1, batch, seq, tm_out)

    if new_rows:
        kv_rows = [r.reshape(batch, min(window, seq), 2, B_HG, B_HD) for r, (window, _) in zip(new_rows, B_GROUPS)]
    else:
        kv_rows = [kv3[:, seq - min(window, seq):, 2 * g * B_GW:2 * (g + 1) * B_GW]
                   .reshape(batch, min(window, seq), 2, B_HG, B_HD) for g, (window, _) in enumerate(B_GROUPS)]
    return x2.reshape(batch, seq, D_MODEL), s_new, kv_rows


def kernel(x_prompt, x_sample, state_hgrn, cache_kv_w128, cache_kv_w512, cache_kv_w2048, c_prompt, c_sample,
           norm_w, ada_w, ada_b, a_lower_bounds, a_w_in, a_b_f, a_g_norm_w, a_w_out, b_w_in, b_q_norm_w,
           b_k_norm_w, b_w_out):
    bp, lp, _ = x_prompt.shape
    bs, ls, _ = x_sample.shape

    pad = (-(bp + bs)) % 8
    c_all = jnp.concatenate([c_prompt, c_sample, jnp.zeros((pad, D_MODEL), F32)], axis=0)
    ada = _ada_vectors(c_all, ada_w, ada_b)
    ada_p, ada_s = ada[:, :bp], ada[:, bp:bp + bs]

    weights = (norm_w, a_lower_bounds, a_w_in[0], a_b_f, a_g_norm_w, a_w_out[0],
               b_w_in[0], b_q_norm_w, b_k_norm_w, b_w_out[0])

    y_p, s_p, kv_p = _trunk(x_prompt, ada_p, 0, None, None, weights,
                            tm_in=256, tm_out=1024, scan_tile=256, scan_chunk=64)
    caches = (cache_kv_w128, cache_kv_w512, cache_kv_w2048)
    y_s, s_s, kv_s = _trunk(x_sample, ada_s, PAST_LEN, state_hgrn[0], caches, weights,
                            tm_in=256, tm_out=256, scan_tile=ls, scan_chunk=ls)

    kv_out = []
    for g in range(len(B_GROUPS)):
        kv_out.append(kv_p[g][None])
        kv_out.append(kv_s[g][None])
    return (y_p, y_s, s_p[None], s_s[None], *kv_out)
```

```python
import functools
import math
from typing import NamedTuple

import numpy as np
import jax
import jax.numpy as jnp
from jax import lax
from jax.experimental import pallas as pl
from jax.experimental.pallas import tpu as pltpu

F32 = jnp.float32
BF16 = jnp.bfloat16

D_MODEL = 1024
EPS = 1e-6
A_HEADS = 8
A_KDIM = 128
A_VDIM = 256
A_QK = A_HEADS * A_KDIM
A_V = A_HEADS * A_VDIM
B_GROUPS = ((128, 1), (512, 4), (2048, 16))
B_HG = 4
B_HD = 128
B_GW = B_HG * B_HD
B_WIDTH = len(B_GROUPS) * B_GW
ROPE_THETA = 10000.0
PAST_LEN = 2048
N_KEYS = 129

COL_TILE = 512
VMEM_LIMIT = 56 * 1024 * 1024

_NT = (((1,), (1,)), ((), ()))
_TN = (((0,), (0,)), ((), ()))


def _sigmoid(x):
    return 1.0 / (1.0 + jnp.exp(-x))


def _silu(x):
    h = 0.5 * x
    return h * jnp.tanh(h) + h


def _dot(a, b, dims=None):
    if dims is None:
        return jnp.dot(a, b, preferred_element_type=F32)
    return lax.dot_general(a, b, dims, preferred_element_type=F32)


def _params(sem):
    return pltpu.CompilerParams(dimension_semantics=sem, vmem_limit_bytes=VMEM_LIMIT)


def _ada_kernel(c_ref, w_ref, b_ref, o_ref):
    a = _silu(c_ref[...]).astype(BF16)
    o_ref[0] = _dot(a, w_ref[0].astype(BF16)) + b_ref[0]


def _ada_vectors(c_all, ada_w, ada_b):
    n_layers, _, width = ada_w.shape
    rows = c_all.shape[0]
    return pl.pallas_call(
        _ada_kernel,
        grid=(n_layers, width // COL_TILE),
        in_specs=[
            pl.BlockSpec((rows, D_MODEL), lambda l, j: (0, 0)),
            pl.BlockSpec((1, D_MODEL, COL_TILE), lambda l, j: (l, 0, j)),
            pl.BlockSpec((1, 1, COL_TILE), lambda l, j: (l, 0, j)),
        ],
        out_specs=pl.BlockSpec((1, rows, COL_TILE), lambda l, j: (l, 0, j)),
        out_shape=jax.ShapeDtypeStruct((n_layers, rows, width), F32),
        compiler_params=_params(("parallel", "parallel")),
        name="ada_vectors",
    )(c_all, ada_w, ada_b.reshape(n_layers, 1, width))


def _modulated_norm(x_ref, ada_ref, nw_ref):
    x = x_ref[...]
    ada = ada_ref[...]
    if x.ndim == 2:
        ada = ada.reshape(1, ada.shape[-1])
    shift = ada[..., :D_MODEL]
    scale = ada[..., D_MODEL:2 * D_MODEL]
    ms = jnp.mean(x * x, axis=-1, keepdims=True)
    y = x * lax.rsqrt(ms + EPS) * nw_ref[...].reshape((1,) * (x.ndim - 1) + (D_MODEL,))
    h = y * (1.0 + scale) + shift
    return h.reshape(-1, D_MODEL).astype(BF16)


def _token_specs(batch, seq, tm):
    if seq >= tm:
        assert seq % tm == 0
        per = seq // tm
        n_tiles = batch * per
        tok_block = lambda w: pl.BlockSpec((tm, w), lambda i: (i, 0))
        ada_block = pl.BlockSpec((1, 1, 3 * D_MODEL), lambda i: (i // per, 0, 0))
        shape = lambda w: (batch * seq, w)
        return n_tiles, tok_block, ada_block, shape, per
    tb = tm // seq
    assert batch % tb == 0
    n_tiles = batch // tb
    tok_block = lambda w: pl.BlockSpec((tb, seq, w), lambda i: (i, 0, 0))
    ada_block = pl.BlockSpec((tb, 1, 3 * D_MODEL), lambda i: (i, 0, 0))
    shape = lambda w: (batch, seq, w)
    return n_tiles, tok_block, ada_block, shape, None


def _resident(shape):
    return pl.BlockSpec(shape, lambda i: (0,) * len(shape), pipeline_mode=pl.Buffered(1))


def _put(ref, c0, val):
    ref[(Ellipsis, slice(c0, c0 + val.shape[-1]))] = val.reshape(ref.shape[:-1] + val.shape[-1:]).astype(ref.dtype)


def _l0_inproj_kernel(x_ref, ada_ref, nw_ref, w_ref, alb_ref, bf_ref, q_ref, k_ref, lf_ref, v_ref, zs_ref):
    h = _modulated_norm(x_ref, ada_ref, nw_ref)
    a = alb_ref[...]
    m = jnp.max(a, axis=0, keepdims=True)
    e = jnp.exp(a - m)
    lb = e[0:1] / jnp.sum(e, axis=0, keepdims=True)
    for c0 in range(0, w_ref.shape[1], COL_TILE):
        acc = _dot(h, w_ref[:, c0:c0 + COL_TILE].astype(BF16))
        if c0 < A_QK:
            _put(q_ref, c0, _silu(acc))
        elif c0 < 2 * A_QK:
            c = c0 - A_QK
            f = lb[:, c:c + COL_TILE] + (1.0 - lb[:, c:c + COL_TILE]) * _sigmoid(acc + bf_ref[:, c:c + COL_TILE])
            _put(lf_ref, c, jnp.log(f))
            _put(k_ref, c, 1.0 - f)
        elif c0 < 2 * A_QK + A_V:
            _put(v_ref, c0 - 2 * A_QK, acc)
        else:
            _put(zs_ref, c0 - 2 * A_QK - A_V, _silu(acc))


def _l0_inproj(x, ada, nw, w_bf, alb, b_f, batch, seq, tm, act_dtype):
    n_tiles, tok_block, ada_block, shape, _ = _token_specs(batch, seq, tm)
    return pl.pallas_call(
        _l0_inproj_kernel,
        grid=(n_tiles,),
        in_specs=[tok_block(D_MODEL), ada_block, _resident((1, D_MODEL)), _resident(w_bf.shape),
                  _resident(alb.shape), _resident(b_f.shape)],
        out_specs=[tok_block(A_QK), tok_block(A_QK), tok_block(A_QK), tok_block(A_V), tok_block(A_V)],
        out_shape=[
            jax.ShapeDtypeStruct(shape(A_QK), F32),
            jax.ShapeDtypeStruct(shape(A_QK), F32),
            jax.ShapeDtypeStruct(shape(A_QK), F32),
            jax.ShapeDtypeStruct(shape(A_V), act_dtype),
            jax.ShapeDtypeStruct(shape(A_V), act_dtype),
        ],
        compiler_params=_params(("parallel",)),
        name="l0_inproj",
    )(x, ada, nw, w_bf, alb, b_f)


def _scan_levels(chunk):
    return [1 << l for l in range(int(math.log2(chunk)))]


def _exponent_matrix(chunk):
    r = np.arange(chunk)[:, None]
    c = np.arange(chunk)[None, :]
    blocks = [c <= r, c > r]
    for s in _scan_levels(chunk):
        bound = (r & ~(2 * s - 1)) | s
        upper = (r & s) != 0
        blocks.append(np.where(upper, (c > bound) & (c <= r), (c > r) & (c <= bound)))
    return np.concatenate(blocks, axis=0).astype(np.float32)


def _hgrn_chunk(q, k, lf, v, state, emat, chunk):
    dk = q.shape[-1]
    ex = jnp.exp(jnp.dot(emat, lf, preferred_element_type=F32, precision=lax.Precision.HIGHEST))
    e_b = ex[0:chunk]
    e_u = ex[chunk:2 * chunk]
    row = lax.broadcasted_iota(jnp.int32, (chunk, 1), 0)
    col = lax.broadcasted_iota(jnp.int32, (1, chunk), 1)

    att = jnp.where(row == col, jnp.sum(q * k, axis=-1, keepdims=True), 0.0)
    for l, s in enumerate(_scan_levels(chunk)):
        g = ex[(2 + l) * chunk:(3 + l) * chunk]
        upper = (row & s) != 0
        xb = (jnp.where(upper, q, k) * g).astype(BF16)
        part = _dot(xb, xb, _NT)
        keep = upper & ((col & s) == 0) & ((row >> (l + 1)) == (col >> (l + 1)))
        att = att + jnp.where(keep, part, 0.0)

    o = _dot(att.astype(BF16), v.astype(BF16)) + _dot((q * e_b).astype(BF16), state.astype(BF16))

    e_last = e_b[chunk - 1:chunk]
    eye = lax.broadcasted_iota(jnp.int32, (dk, dk), 0) == lax.broadcasted_iota(jnp.int32, (dk, dk), 1)
    e_col = jnp.sum(jnp.where(eye, e_last, 0.0), axis=-1, keepdims=True)
    new_state = state * e_col + _dot((k * e_u).astype(BF16), v.astype(BF16), _TN)
    return o, new_state


SAFE_DECAY = 60.0
SPLIT_PAD = 32
SCAN_LINK = 2
SCAN_SEQS = 4


def _split_prefix_matrix(chunk, n_seq):
    t = np.kron(np.eye(n_seq, dtype=np.float32), _exponent_matrix(chunk)[:chunk])
    pad = (-3 * t.shape[1]) % SPLIT_PAD
    return np.concatenate([t, t, t, np.zeros((t.shape[0], pad), np.float32)], axis=1)


def _hgrn_chunk_bounded(q, k, lf, v, states, tmat, chunk, n_seq, linked):
    n_rows = n_seq * chunk
    hi = lf.astype(BF16)
    r1 = lf - hi.astype(F32)
    mid = r1.astype(BF16)
    lo = r1 - mid.astype(F32)
    parts = [hi.astype(F32), mid.astype(F32), lo]
    pad = tmat.shape[1] - 3 * n_rows
    if pad:
        parts.append(jnp.zeros((pad, lf.shape[1]), F32))
    b = _dot(tmat, jnp.concatenate(parts, axis=0).astype(BF16))
    e_b = jnp.exp(b)
    qb = q * e_b
    kn_f = k * jnp.exp(-b)
    kn = kn_f.astype(BF16)
    row = lax.broadcasted_iota(jnp.int32, (n_rows, n_rows), 0)
    col = lax.broadcasted_iota(jnp.int32, (n_rows, n_rows), 1)
    causal = row >= col
    if n_seq > 1:
        shift = chunk.bit_length() - 1
        causal = causal & ((row >> shift) == (col >> shift))
    eye = lax.broadcasted_iota(jnp.int32, (A_KDIM, A_KDIM), 0) == lax.broadcasted_iota(jnp.int32, (A_KDIM, A_KDIM), 1)
    ks = [slice(h * A_KDIM, (h + 1) * A_KDIM) for h in range(A_HEADS)]
    vs = [slice(h * A_VDIM, (h + 1) * A_VDIM) for h in range(A_HEADS)]
    sq = [slice(s * chunk, (s + 1) * chunk) for s in range(n_seq)]
    qbb, vb = qb.astype(BF16), v.astype(BF16)
    qb_s = [qbb] if n_seq == 1 else [qb[r].astype(BF16) for r in sq]
    ku_s = [(kn_f[r] * e_b[r.stop - 1:r.stop]).astype(BF16) for r in sq]
    v_s = [vb] if n_seq == 1 else [v[r].astype(BF16) for r in sq]
    atts = [jnp.where(causal, _dot(qbb[:, ks[h]], kn[:, ks[h]], _NT), 0.0).astype(BF16) for h in range(A_HEADS)]
    e_cols = [[jnp.sum(jnp.where(eye, e_b[(s + 1) * chunk - 1:(s + 1) * chunk, ks[h]], 0.0), axis=-1, keepdims=True)
               for h in range(A_HEADS)] for s in range(n_seq)]
    new_states = [[None] * A_HEADS for _ in range(n_seq)]
    carried = [[None] * A_HEADS for _ in range(n_seq)]
    for s in range(n_seq):
        for h in range(A_HEADS):
            state = new_states[s - 1][h] if (linked and s > 0) else states[0 if linked else s][h]
            carried[s][h] = _dot(qb_s[s][:, ks[h]], state.astype(BF16))
            new_states[s][h] = state * e_cols[s][h] + _dot(ku_s[s][:, ks[h]], v_s[s][:, vs[h]], _TN)
    outs = []
    for h in range(A_HEADS):
        inter = carried[0][h] if n_seq == 1 else jnp.concatenate([carried[s][h] for s in range(n_seq)], axis=0)
        outs.append(_dot(atts[h], vb[:, vs[h]]) + inter)
    return outs, new_states


def _hgrn_scan_kernel(*refs, chunk, n_chunks, has_state, n_seq, n_link):
    one_shot = n_seq > 0
    if has_state:
        q_ref, k_ref, lf_ref, v_ref, zs_ref, gw_ref, e_ref, t_ref, s0_ref, og_ref, so_ref, s_scr = refs
    else:
        q_ref, k_ref, lf_ref, v_ref, zs_ref, gw_ref, e_ref, t_ref, og_ref, so_ref, s_scr = refs
    t = pl.program_id(1)

    if not one_shot:
        @pl.when(t == 0)
        def _():
            if has_state:
                s_scr[...] = s0_ref[0]
            else:
                s_scr[...] = jnp.zeros_like(s_scr)

    def load_state(s, h):
        return s0_ref[s, h] if one_shot else s_scr[h]

    def store_state(s, h, val):
        if one_shot:
            so_ref[s, h] = val
        else:
            s_scr[h] = val

    gw = gw_ref[...]

    def finish(rows, h, o):
        vc = slice(h * A_VDIM, (h + 1) * A_VDIM)
        ms = jnp.mean(o * o, axis=-1, keepdims=True)
        og_ref[rows, vc] = (o * lax.rsqrt(ms + EPS) * gw * zs_ref[rows, vc]).astype(og_ref.dtype)

    seqs = range(max(n_seq, 1))

    def chunk_rows(c, s=None, n=1):
        if n_chunks > 1:
            return pl.ds(pl.multiple_of(c * (n * chunk), n * chunk), n * chunk)
        return slice(0, len(seqs) * chunk) if s is None else slice(s * chunk, (s + 1) * chunk)

    def bounded_body(c, carry):
        rows = chunk_rows(c, n=n_link)
        states = [[load_state(s, h) for h in range(A_HEADS)] for s in seqs]
        n_at_once = len(seqs) if one_shot else n_link
        outs, new_states = _hgrn_chunk_bounded(q_ref[rows, :], k_ref[rows, :], lf_ref[rows, :], v_ref[rows, :],
                                               states, t_ref[...], chunk, n_at_once, linked=not one_shot)
        for h in range(A_HEADS):
            if one_shot:
                for s in seqs:
                    store_state(s, h, new_states[s][h])
            else:
                store_state(0, h, new_states[-1][h])
            finish(rows, h, outs[h])
        return carry

    def general_body(c, carry):
        for s in seqs:
            rows = chunk_rows(c, s)
            for h in range(A_HEADS):
                kc = slice(h * A_KDIM, (h + 1) * A_KDIM)
                vc = slice(h * A_VDIM, (h + 1) * A_VDIM)
                o, s_new = _hgrn_chunk(q_ref[rows, kc], k_ref[rows, kc], lf_ref[rows, kc],
                                       v_ref[rows, vc], load_state(s, h), e_ref[...], chunk)
                store_state(s, h, s_new)
                finish(rows, h, o)
        return carry

    def run(body, trips):
        def go():
            if trips == 1:
                body(0, 0)
            else:
                lax.fori_loop(0, trips, body, 0)
        return go

    bounded = jnp.min(lf_ref[...]) * chunk >= -SAFE_DECAY
    lax.cond(bounded, run(bounded_body, n_chunks // n_link), run(general_body, n_chunks))

    if not one_shot:
        @pl.when(t == pl.num_programs(1) - 1)
        def _():
            so_ref[0] = s_scr[...]


def _hgrn_scan(q, k, lf, v, zs, gw, s0, batch, seq, tile, chunk, out_dtype, n_seq=0):
    n_t = seq // tile
    per_step = max(n_seq, 1)
    assert batch % per_step == 0 and (n_seq == 0 or (s0 is not None and seq == tile == chunk))
    n_link = SCAN_LINK if (n_seq == 0 and (tile // chunk) % SCAN_LINK == 0) else 1
    emat = jnp.asarray(_exponent_matrix(chunk))
    tmat = jnp.asarray(_split_prefix_matrix(chunk, max(per_step, n_link)), dtype=BF16)
    row_block = lambda w: pl.BlockSpec((tile * per_step, w), lambda b, t: (b * n_t + t, 0))
    state_block = pl.BlockSpec((per_step, A_HEADS, A_KDIM, A_VDIM), lambda b, t: (b, 0, 0, 0))
    in_specs = [row_block(A_QK), row_block(A_QK), row_block(A_QK), row_block(A_V), row_block(A_V),
                pl.BlockSpec((1, A_VDIM), lambda b, t: (0, 0)),
                pl.BlockSpec(emat.shape, lambda b, t: (0, 0)),
                pl.BlockSpec(tmat.shape, lambda b, t: (0, 0))]
    args = [q, k, lf, v, zs, gw, emat, tmat]
    if s0 is not None:
        in_specs.append(state_block)
        args.append(s0)
    kern = functools.partial(_hgrn_scan_kernel, chunk=chunk, n_chunks=tile // chunk,
                             has_state=s0 is not None, n_seq=n_seq, n_link=n_link)
    return pl.pallas_call(
        kern,
        grid=(batch // per_step, n_t),
        in_specs=in_specs,
        out_specs=[row_block(A_V), state_block],
        out_shape=[jax.ShapeDtypeStruct((batch * seq, A_V), out_dtype),
                   jax.ShapeDtypeStruct((batch, A_HEADS, A_KDIM, A_VDIM), F32)],
        scratch_shapes=[pltpu.VMEM((A_HEADS, A_KDIM, A_VDIM), F32)],
        compiler_params=_params(("parallel", "arbitrary")),
        name="hgrn_scan",
    )(*args)


def _outproj_kernel(*refs):
    *g_refs, w_ref, x_ref, ada_ref, o_ref = refs
    gs = [g_ref[...] for g_ref in g_refs]
    gs = [g.reshape(-1, g.shape[-1]).astype(BF16) for g in gs]
    ada = ada_ref[...]
    if len(x_ref.shape) == 2:
        ada = ada.reshape(1, ada.shape[-1])
    for c0 in range(0, D_MODEL, COL_TILE):
        cols = (Ellipsis, slice(c0, c0 + COL_TILE))
        y, k0 = None, 0
        for g in gs:
            part = _dot(g, w_ref[k0:k0 + g.shape[1], c0:c0 + COL_TILE].astype(BF16))
            y = part if y is None else y + part
            k0 += g.shape[1]
        gate = ada[..., 2 * D_MODEL + c0:2 * D_MODEL + c0 + COL_TILE]
        o_ref[cols] = x_ref[cols] + gate * y.reshape(x_ref.shape[:-1] + (COL_TILE,))


def _outproj(gs, w_bf, x, ada, batch, seq, tm):
    n_tiles, tok_block, ada_block, shape, _ = _token_specs(batch, seq, tm)
    assert sum(g.shape[-1] for g in gs) == w_bf.shape[0]
    return pl.pallas_call(
        _outproj_kernel,
        grid=(n_tiles,),
        in_specs=[tok_block(g.shape[-1]) for g in gs] + [_resident(w_bf.shape), tok_block(D_MODEL), ada_block],
        out_specs=tok_block(D_MODEL),
        out_shape=jax.ShapeDtypeStruct(shape(D_MODEL), F32),
        compiler_params=_params(("parallel",)),
        name="outproj",
    )(*gs, w_bf, x, ada)


def _rope_table_kernel(pos_ref, inv_ref, cos_ref, sin_ref):
    ang = pos_ref[...] * inv_ref[...]
    lane = lax.broadcasted_iota(jnp.int32, ang.shape, 1)
    cos_ref[...] = jnp.cos(ang)
    sin_ref[...] = jnp.where(lane < B_HD // 2, -jnp.sin(ang), jnp.sin(ang))


def _rope_tables(start, seq):
    half = B_HD // 2
    inv = ROPE_THETA ** (-jnp.arange(half, dtype=F32) / half)
    inv2 = jnp.concatenate([inv, inv]).reshape(1, B_HD)
    pos = (start + jnp.arange(seq, dtype=jnp.int32)).astype(F32).reshape(seq, 1)
    return pl.pallas_call(
        _rope_table_kernel,
        out_shape=[jax.ShapeDtypeStruct((seq, B_HD), F32)] * 2,
        name="rope_tables",
    )(pos, inv2)


KV_ROWS = 2 * B_HG


def _l1_inproj_kernel(x_ref, ada_ref, nw_ref, w_ref, qn_ref, kn_ref, cos_ref, sin_ref, q_ref, kv_ref, zs_ref,
                      *row_refs, row_tokens, row_first, per):
    h = _modulated_norm(x_ref, ada_ref, nw_ref)
    tm = h.shape[0]
    cos = cos_ref[...]
    sin = sin_ref[...]
    reps = tm // cos.shape[0]
    if reps > 1:
        cos = jnp.broadcast_to(cos[None], (reps,) + cos.shape).reshape(tm, B_HD)
        sin = jnp.broadcast_to(sin[None], (reps,) + sin.shape).reshape(tm, B_HD)

    def norm_rope(acc, nw):
        outs = []
        for hd in range(acc.shape[1] // B_HD):
            xh = acc[:, hd * B_HD:(hd + 1) * B_HD]
            ms = jnp.mean(xh * xh, axis=-1, keepdims=True)
            y = xh * lax.rsqrt(ms + EPS) * nw
            outs.append(y * cos + pltpu.roll(y, B_HD // 2, 1) * sin)
        return jnp.concatenate(outs, axis=-1)

    for c0 in range(0, w_ref.shape[1], B_GW):
        acc = _dot(h, w_ref[:, c0:c0 + B_GW].astype(BF16))
        seg, g = divmod(c0 // B_GW, len(B_GROUPS))
        if seg == 0:
            _put(q_ref, g * B_GW, norm_rope(acc, qn_ref[...]))
        elif seg == 1:
            _put(kv_ref, 2 * g * B_GW, norm_rope(acc, kn_ref[...]))
        elif seg == 2:
            _put(kv_ref, (2 * g + 1) * B_GW, acc)
        else:
            _put(zs_ref, g * B_GW, _silu(acc))

    for g, rows_ref in enumerate(row_refs):
        @pl.when(pl.program_id(0) % per >= row_first[g])
        def _(g=g, rows_ref=rows_ref):
            n = row_tokens[g]
            for j in range(KV_ROWS):
                c0 = 2 * g * B_GW + j * B_HD
                rows_ref[pl.ds(j, n, stride=KV_ROWS), :] = kv_ref[tm - n:tm, c0:c0 + B_HD]


def _l1_inproj(x, ada, nw, w_bf, qn, kn, cos, sin, batch, seq, tm, act_dtype):
    n_tiles, tok_block, ada_block, shape, per = _token_specs(batch, seq, tm)
    out_specs = [tok_block(B_WIDTH), tok_block(2 * B_WIDTH), tok_block(B_WIDTH)]
    out_shape = [jax.ShapeDtypeStruct(shape(B_WIDTH), F32),
                 jax.ShapeDtypeStruct(shape(2 * B_WIDTH), F32),
                 jax.ShapeDtypeStruct(shape(B_WIDTH), act_dtype)]
    row_tokens, row_first = (), ()
    if per is None:
        tab_block = _resident((seq, B_HD))
    else:
        tab_block = pl.BlockSpec((tm, B_HD), lambda i: (i % per, 0))
        for window, _ in B_GROUPS:
            keep = min(window, seq)
            blk = min(keep, tm)
            n_blk = keep // blk
            first = per - n_blk
            out_specs.append(pl.BlockSpec(
                (blk * KV_ROWS, B_HD),
                lambda i, n_blk=n_blk, first=first: ((i // per) * n_blk + jnp.clip(i % per - first, 0, n_blk - 1), 0)))
            out_shape.append(jax.ShapeDtypeStruct((batch * keep * KV_ROWS, B_HD), F32))
            row_tokens += (blk,)
            row_first += (first,)
    return pl.pallas_call(
        functools.partial(_l1_inproj_kernel, row_tokens=row_tokens, row_first=row_first, per=per),
        grid=(n_tiles,),
        in_specs=[tok_block(D_MODEL), ada_block, _resident((1, D_MODEL)), _resident(w_bf.shape),
                  _resident((1, B_HD)), _resident((1, B_HD)), tab_block, tab_block],
        out_specs=out_specs,
        out_shape=out_shape,
        compiler_params=_params(("arbitrary",)),
        name="l1_inproj",
    )(x, ada, nw, w_bf, qn, kn, cos, sin)


Q_SUB = 128
ATTN_TILE = 2048
ATTN_BATCH = 8
MIX_ROWS = 256


def _attend(qs, k2s, v2s, valids):
    scale = B_HD ** -0.5
    ss = [jnp.where(ok, _dot(q, k2, _NT) * scale, -jnp.inf) for q, k2, ok in zip(qs, k2s, valids)]
    ms = [jnp.max(s, axis=-1, keepdims=True) for s in ss]
    ps = [jnp.exp(s - m) for s, m in zip(ss, ms)]
    ls = [jnp.sum(p, axis=-1, keepdims=True) for p in ps]
    os = [_dot(p.astype(BF16), v2) * (1.0 / l) for p, v2, l in zip(ps, v2s, ls)]
    lses = [m + jnp.log(l) for m, l in zip(ms, ls)]
    return os, lses


def _prompt_attn_kernel(*refs):
    n_g = len(B_GROUPS)
    ins, zs_refs = refs[:5 * n_g], refs[5 * n_g:6 * n_g]
    out_refs, (o_scr, lse_scr) = refs[6 * n_g:7 * n_g], refs[7 * n_g:]
    first_key = jnp.where(pl.program_id(2) == 0, Q_SUB, 0)
    row = lax.broadcasted_iota(jnp.int32, (Q_SUB, 2 * Q_SUB), 0)
    col = lax.broadcasted_iota(jnp.int32, (Q_SUB, 2 * Q_SUB), 1)
    band = (col >= row) & (col <= row + Q_SUB)
    band_first = band & (col >= first_key)

    for g, (_, dil) in enumerate(B_GROUPS):
        q_ref, kc_ref, vc_ref, kp_ref, vp_ref = ins[5 * g:5 * g + 5]
        n_sb = ATTN_TILE // (Q_SUB * dil)

        def rows_of(r, first, n):
            start = r + first * dil
            return pl.ds(start, n) if dil == 1 else pl.ds(start, n, stride=dil)

        work = []
        for r in range(dil):
            cls = rows_of(r, 0, n_sb * Q_SUB)
            kr = jnp.concatenate([kp_ref[0, rows_of(r, 0, Q_SUB), :], kc_ref[0, cls, :]], axis=0).astype(BF16)
            vr = jnp.concatenate([vp_ref[0, rows_of(r, 0, Q_SUB), :], vc_ref[0, cls, :]], axis=0).astype(BF16)
            qr = q_ref[0, cls, :].astype(BF16)
            for sb in range(n_sb):
                work.append((rows_of(r, sb * Q_SUB, Q_SUB), qr[sb * Q_SUB:(sb + 1) * Q_SUB],
                             kr[sb * Q_SUB:(sb + 2) * Q_SUB], vr[sb * Q_SUB:(sb + 2) * Q_SUB],
                             band_first if sb == 0 else band))
        for i in range(0, len(work), ATTN_BATCH):
            part = work[i:i + ATTN_BATCH]
            os, lses = _attend([w[1] for w in part], [w[2] for w in part], [w[3] for w in part],
                               [w[4] for w in part])
            for w, o, lse in zip(part, os, lses):
                o_scr[g, w[0], :] = o
                lse_scr[g, w[0], :] = jnp.broadcast_to(lse, (Q_SUB, B_HD))

    for c in range(ATTN_TILE // MIX_ROWS):
        rows = slice(c * MIX_ROWS, (c + 1) * MIX_ROWS)
        lses = [lse_scr[g, rows, :] for g in range(n_g)]
        m = functools.reduce(jnp.maximum, lses)
        es = [jnp.exp(l - m) for l in lses]
        inv = 1.0 / functools.reduce(lambda a, b: a + b, es)
        for g in range(n_g):
            out_refs[g][0, rows, :] = (o_scr[g, rows, :] * (es[g] * inv) * zs_refs[g][0, rows, :]
                                       ).astype(out_refs[g].dtype)


def _prompt_attn(q, kv, zs, batch, seq):
    assert seq % ATTN_TILE == 0
    n_tiles = seq // ATTN_TILE
    tok = lambda c0: pl.BlockSpec((1, ATTN_TILE, B_HD), lambda b, h, i: (b, i, c0 + h))
    in_specs, args = [], []
    for g, (_, dil) in enumerate(B_GROUPS):
        back = Q_SUB * dil
        assert ATTN_TILE % back == 0
        k0, v0 = 2 * g * B_HG, (2 * g + 1) * B_HG
        prev = lambda c0, back=back: pl.BlockSpec(
            (1, back, B_HD), lambda b, h, i: (b, jnp.maximum(i * (ATTN_TILE // back) - 1, 0), c0 + h))
        in_specs += [tok(g * B_HG), tok(k0), tok(v0), prev(k0), prev(v0)]
        args += [q, kv, kv, kv, kv]
    in_specs += [tok(g * B_HG) for g in range(len(B_GROUPS))]
    args += [zs] * len(B_GROUPS)
    out = pl.BlockSpec((1, ATTN_TILE, B_HD), lambda b, h, i: (b, i, h))
    return pl.pallas_call(
        _prompt_attn_kernel,
        grid=(batch, B_HG, n_tiles),
        in_specs=in_specs,
        out_specs=[out] * len(B_GROUPS),
        out_shape=[jax.ShapeDtypeStruct((batch, seq, B_GW), BF16)] * len(B_GROUPS),
        scratch_shapes=[pltpu.VMEM((len(B_GROUPS), ATTN_TILE, B_HD), F32)] * 2,
        compiler_params=_params(("parallel", "parallel", "arbitrary")),
        name="prompt_attn",
    )(*args)


SAMPLE_NB = 2


def _sample_masks(seq):
    hq = np.arange(B_HG * seq)[:, None] // seq
    iq = np.arange(B_HG * seq)[:, None] % seq
    masks = []
    for window, dil in B_GROUPS:
        n_tok = window if dil < seq else (window // dil) * seq
        col = np.arange(n_tok * KV_ROWS)[None, :]
        tok, is_v, head = col // KV_ROWS, (col // B_HG) % 2, col % B_HG
        if dil < seq:
            ok = (tok >= iq) & ((tok - iq) % dil == 0)
        else:
            ok = (tok % seq) == iq
        masks.append(np.where(ok & (is_v == 0) & (head == hq), 0.0, -np.inf).astype(np.float32))
    col = np.arange(B_HG * seq)[None, :]
    new = []
    for _, dil in B_GROUPS:
        ok = (col // seq == hq) & (col % seq <= iq) & ((iq - col % seq) % dil == 0)
        new.append(np.where(ok, 0.0, -np.inf).astype(np.float32))
    return masks, np.stack(new)


def _sample_attn_kernel(q_ref, kv_ref, zs_ref, c0_ref, c1_ref, c2_ref, m0_ref, m1_ref, m2_ref, mn_ref,
                        out_ref, *, seq, nb):
    scale = B_HD ** -0.5
    caches = (c0_ref, c1_ref, c2_ref)
    masks = (m0_ref, m1_ref, m2_ref)
    groups = range(len(B_GROUPS))

    def heads(ref, e, c0):
        return jnp.concatenate([ref[e, :, c0 + h * B_HD:c0 + (h + 1) * B_HD] for h in range(B_HG)],
                               axis=0).astype(BF16)

    def scores(e):
        qa = [heads(q_ref, e, g * B_GW) for g in groups]
        k_new = [heads(kv_ref, e, 2 * g * B_GW) for g in groups]
        rows = [caches[g][e].reshape(-1, B_HD).astype(BF16) for g in groups]
        s_c = [_dot(qa[g], rows[g], _NT) * scale + masks[g][...] for g in groups]
        s_n = [_dot(qa[g], k_new[g], _NT) * scale + mn_ref[g] for g in groups]
        return rows, s_c, s_n

    def weights(s_c, s_n):
        ms = [jnp.maximum(jnp.max(s_c[g], axis=-1, keepdims=True), jnp.max(s_n[g], axis=-1, keepdims=True))
              for g in groups]
        p_c = [jnp.exp(s_c[g] - ms[g]) for g in groups]
        p_n = [jnp.exp(s_n[g] - ms[g]) for g in groups]
        ls = [jnp.sum(p_c[g], axis=-1, keepdims=True) + jnp.sum(p_n[g], axis=-1, keepdims=True) for g in groups]
        p_v = [pltpu.roll(p_c[g], B_HG, 1).astype(BF16) for g in groups]
        return ms, ls, p_v, [p.astype(BF16) for p in p_n]

    def values(e, rows, p_v, p_n):
        v_new = [heads(kv_ref, e, (2 * g + 1) * B_GW) for g in groups]
        return [_dot(p_v[g], rows[g]) + _dot(p_n[g], v_new[g]) for g in groups]

    scored = [scores(e) for e in range(nb)]
    weighted = [weights(s_c, s_n) for _, s_c, s_n in scored]
    accs = [values(e, scored[e][0], weighted[e][2], weighted[e][3]) for e in range(nb)]
    for e in range(nb):
        ms, ls = weighted[e][0], weighted[e][1]
        outs = [accs[e][g] / ls[g] for g in groups]
        lses = [ms[g] + jnp.log(ls[g]) for g in groups]
        m = functools.reduce(jnp.maximum, lses)
        es = [jnp.exp(l - m) for l in lses]
        inv = 1.0 / (es[0] + es[1] + es[2])
        for g in groups:
            mixed = outs[g] * (es[g] * inv)
            for h in range(B_HG):
                cols = slice(g * B_GW + h * B_HD, g * B_GW + (h + 1) * B_HD)
                out_ref[e, :, cols] = mixed[h * seq:(h + 1) * seq] * zs_ref[e, :, cols]


def _sample_attn(q, kv, zs, caches, batch, seq, nb):
    assert batch % nb == 0
    masks, mask_new = _sample_masks(seq)
    specs, views = [], []
    for (window, dil), c in zip(B_GROUPS, caches):
        assert window == dil * (N_KEYS - 1) and c.shape[2] == window
        if dil >= seq:
            views.append(c.reshape(batch, window // dil, dil * KV_ROWS, B_HD))
            specs.append(pl.BlockSpec((nb, window // dil, seq * KV_ROWS, B_HD), lambda b: (b, 0, 0, 0)))
        else:
            assert seq % dil == 0
            views.append(c.reshape(batch, window * KV_ROWS, B_HD))
            specs.append(pl.BlockSpec((nb, window * KV_ROWS, B_HD), lambda b: (b, 0, 0)))
    const = lambda a: pl.BlockSpec(a.shape, lambda b: (0,) * a.ndim)
    tok = lambda w: pl.BlockSpec((nb, seq, w), lambda b: (b, 0, 0))
    return pl.pallas_call(
        functools.partial(_sample_attn_kernel, seq=seq, nb=nb),
        grid=(batch // nb,),
        in_specs=[tok(B_WIDTH), tok(2 * B_WIDTH), tok(B_WIDTH)] + specs
                 + [const(a) for a in masks] + [const(mask_new)],
        out_specs=tok(B_WIDTH),
        out_shape=jax.ShapeDtypeStruct((batch, seq, B_WIDTH), F32),
        compiler_params=_params(("parallel",)),
        name="sample_attn",
    )(q, kv, zs, *views, *[jnp.asarray(a) for a in masks], jnp.asarray(mask_new))


class _Tiles(NamedTuple):
    l0_in: int
    l1_in: int
    out: int
    scan: int
    chunk: int


LONG_SEQ = 512


def _tiles(seq):
    if seq >= LONG_SEQ:
        return _Tiles(l0_in=512, l1_in=256, out=1024, scan=512, chunk=64)
    return _Tiles(l0_in=256, l1_in=256, out=256, scan=seq, chunk=seq)


def _trunk(x, ada, start, state, caches, weights):
    (norm_w, alb, a_w_in, a_b_f, a_gw, a_w_out, b_w_in, b_qn, b_kn, b_w_out) = weights
    batch, seq, _ = x.shape
    tiles = _tiles(seq)
    flat = seq >= LONG_SEQ
    act = BF16 if flat else F32
    tok = (lambda a: a.reshape(batch * seq, a.shape[-1])) if flat else (lambda a: a)
    rows = lambda a: a.reshape(batch * seq, a.shape[-1])
    ada0 = ada[0].reshape(batch, 1, 3 * D_MODEL)
    ada1 = ada[1].reshape(batch, 1, 3 * D_MODEL)

    x0 = tok(x)
    q, k, lf, v, zs = _l0_inproj(x0, ada0, norm_w[0:1], a_w_in, alb, a_b_f, batch, seq, tiles.l0_in, act)
    og, s_new = _hgrn_scan(rows(q), rows(k), rows(lf), rows(v), rows(zs), a_gw, state,
                           batch, seq, tiles.scan, tiles.chunk, act, n_seq=0 if state is None else SCAN_SEQS)
    og = og if flat else og.reshape(batch, seq, A_V)
    x1 = _outproj([og], a_w_out, x0, ada0, batch, seq, tiles.out)

    cos, sin = _rope_tables(start, seq)
    qr, kv, zs1, *new_rows = _l1_inproj(x1, ada1, norm_w[1:2], b_w_in, b_qn, b_kn, cos, sin, batch, seq,
                                        tiles.l1_in, act)
    kv3 = kv.reshape(batch, seq, 2 * B_WIDTH)
    if caches is None:
        og1 = _prompt_attn(qr.reshape(batch, seq, B_WIDTH), kv3, zs1.reshape(batch, seq, B_WIDTH), batch, seq)
        og1 = [o.reshape(batch * seq, B_GW) for o in og1]
    else:
        og1 = [_sample_attn(qr, kv, zs1, caches, batch, seq, SAMPLE_NB)]
    x2 = _outproj(og1, b_w_out, x1, ada1, batch, seq, tiles.out)

    if new_rows:
        kv_rows = [r.reshape(batch, min(window, seq), 2, B_HG, B_HD) for r, (window, _) in zip(new_rows, B_GROUPS)]
    else:
        kv_rows = [kv3[:, seq - min(window, seq):, 2 * g * B_GW:2 * (g + 1) * B_GW]
                   .reshape(batch, min(window, seq), 2, B_HG, B_HD) for g, (window, _) in enumerate(B_GROUPS)]
    return x2.reshape(batch, seq, D_MODEL), s_new, kv_rows


def kernel(x_prompt, x_sample, state_hgrn, cache_kv_w128, cache_kv_w512, cache_kv_w2048, c_prompt, c_sample,
           norm_w, ada_w, ada_b, a_lower_bounds, a_w_in, a_b_f, a_g_norm_w, a_w_out, b_w_in, b_q_norm_w,
           b_k_norm_w, b_w_out):
    bp, lp, _ = x_prompt.shape
    bs, ls, _ = x_sample.shape

    pad = (-(bp + bs)) % 8
    c_all = jnp.concatenate([c_prompt, c_sample, jnp.zeros((pad, D_MODEL), F32)], axis=0)
    ada = _ada_vectors(c_all, ada_w, ada_b)
    ada_p, ada_s = ada[:, :bp], ada[:, bp:bp + bs]

    weights = (norm_w, a_lower_bounds, a_w_in[0], a_b_f, a_g_norm_w, a_w_out[0],
               b_w_in[0], b_q_norm_w, b_k_norm_w, b_w_out[0])

    y_p, s_p, kv_p = _trunk(x_prompt, ada_p, 0, None, None, weights)
    caches = (cache_kv_w128, cache_kv_w512, cache_kv_w2048)
    y_s, s_s, kv_s = _trunk(x_sample, ada_s, PAST_LEN, state_hgrn[0], caches, weights)

    kv_out = []
    for g in range(len(B_GROUPS)):
        kv_out.append(kv_p[g][None])
        kv_out.append(kv_s[g][None])
    return (y_p, y_s, s_p[None], s_s[None], *kv_out)
```

```python
import functools
import math
from typing import NamedTuple

import numpy as np
import jax
import jax.numpy as jnp
from jax import lax
from jax.experimental import pallas as pl
from jax.experimental.pallas import tpu as pltpu

F32 = jnp.float32
BF16 = jnp.bfloat16

D_MODEL = 1024
EPS = 1e-6
A_HEADS = 8
A_KDIM = 128
A_VDIM = 256
A_QK = A_HEADS * A_KDIM
A_V = A_HEADS * A_VDIM
B_GROUPS = ((128, 1), (512, 4), (2048, 16))
B_HG = 4
B_HD = 128
B_GW = B_HG * B_HD
B_WIDTH = len(B_GROUPS) * B_GW
ROPE_THETA = 10000.0
PAST_LEN = 2048
N_KEYS = 129

COL_TILE = 512
VMEM_LIMIT = 56 * 1024 * 1024

_NT = (((1,), (1,)), ((), ()))
_TN = (((0,), (0,)), ((), ()))


def _sigmoid(x):
    return 1.0 / (1.0 + jnp.exp(-x))


def _silu(x):
    h = 0.5 * x
    return h * jnp.tanh(h) + h


def _dot(a, b, dims=None):
    if dims is None:
        return jnp.dot(a, b, preferred_element_type=F32)
    return lax.dot_general(a, b, dims, preferred_element_type=F32)


def _params(sem):
    return pltpu.CompilerParams(dimension_semantics=sem, vmem_limit_bytes=VMEM_LIMIT)


def _ada_kernel(c_ref, w_ref, b_ref, o_ref):
    a = _silu(c_ref[...]).astype(BF16)
    o_ref[0] = _dot(a, w_ref[0].astype(BF16)) + b_ref[0]


def _ada_vectors(c_all, ada_w, ada_b):
    n_layers, _, width = ada_w.shape
    rows = c_all.shape[0]
    return pl.pallas_call(
        _ada_kernel,
        grid=(n_layers, width // COL_TILE),
        in_specs=[
            pl.BlockSpec((rows, D_MODEL), lambda l, j: (0, 0)),
            pl.BlockSpec((1, D_MODEL, COL_TILE), lambda l, j: (l, 0, j)),
            pl.BlockSpec((1, 1, COL_TILE), lambda l, j: (l, 0, j)),
        ],
        out_specs=pl.BlockSpec((1, rows, COL_TILE), lambda l, j: (l, 0, j)),
        out_shape=jax.ShapeDtypeStruct((n_layers, rows, width), F32),
        compiler_params=_params(("parallel", "parallel")),
        name="ada_vectors",
    )(c_all, ada_w, ada_b.reshape(n_layers, 1, width))


def _modulated_norm(x_ref, ada_ref, nw_ref):
    x = x_ref[...]
    ada = ada_ref[...]
    if x.ndim == 2:
        ada = ada.reshape(1, ada.shape[-1])
    shift = ada[..., :D_MODEL]
    scale = ada[..., D_MODEL:2 * D_MODEL]
    ms = jnp.mean(x * x, axis=-1, keepdims=True)
    y = x * lax.rsqrt(ms + EPS) * nw_ref[...].reshape((1,) * (x.ndim - 1) + (D_MODEL,))
    h = y * (1.0 + scale) + shift
    return h.reshape(-1, D_MODEL).astype(BF16)


def _token_specs(batch, seq, tm):
    if seq >= tm:
        assert seq % tm == 0
        per = seq // tm
        n_tiles = batch * per
        tok_block = lambda w: pl.BlockSpec((tm, w), lambda i: (i, 0))
        ada_block = pl.BlockSpec((1, 1, 3 * D_MODEL), lambda i: (i // per, 0, 0))
        shape = lambda w: (batch * seq, w)
        return n_tiles, tok_block, ada_block, shape, per
    tb = tm // seq
    assert batch % tb == 0
    n_tiles = batch // tb
    tok_block = lambda w: pl.BlockSpec((tb, seq, w), lambda i: (i, 0, 0))
    ada_block = pl.BlockSpec((tb, 1, 3 * D_MODEL), lambda i: (i, 0, 0))
    shape = lambda w: (batch, seq, w)
    return n_tiles, tok_block, ada_block, shape, None


def _resident(shape):
    return pl.BlockSpec(shape, lambda i: (0,) * len(shape), pipeline_mode=pl.Buffered(1))


def _put(ref, c0, val):
    ref[(Ellipsis, slice(c0, c0 + val.shape[-1]))] = val.reshape(ref.shape[:-1] + val.shape[-1:]).astype(ref.dtype)


def _l0_inproj_kernel(x_ref, ada_ref, nw_ref, w_ref, alb_ref, bf_ref, q_ref, k_ref, lf_ref, v_ref, zs_ref):
    h = _modulated_norm(x_ref, ada_ref, nw_ref)
    a = alb_ref[...]
    m = jnp.max(a, axis=0, keepdims=True)
    e = jnp.exp(a - m)
    lb = e[0:1] / jnp.sum(e, axis=0, keepdims=True)
    for c0 in range(0, w_ref.shape[1], COL_TILE):
        acc = _dot(h, w_ref[:, c0:c0 + COL_TILE].astype(BF16))
        if c0 < A_QK:
            _put(q_ref, c0, _silu(acc))
        elif c0 < 2 * A_QK:
            c = c0 - A_QK
            f = lb[:, c:c + COL_TILE] + (1.0 - lb[:, c:c + COL_TILE]) * _sigmoid(acc + bf_ref[:, c:c + COL_TILE])
            _put(lf_ref, c, jnp.log(f))
            _put(k_ref, c, 1.0 - f)
        elif c0 < 2 * A_QK + A_V:
            _put(v_ref, c0 - 2 * A_QK, acc)
        else:
            _put(zs_ref, c0 - 2 * A_QK - A_V, _silu(acc))


def _l0_inproj(x, ada, nw, w_bf, alb, b_f, batch, seq, tm, act_dtype):
    n_tiles, tok_block, ada_block, shape, _ = _token_specs(batch, seq, tm)
    return pl.pallas_call(
        _l0_inproj_kernel,
        grid=(n_tiles,),
        in_specs=[tok_block(D_MODEL), ada_block, _resident((1, D_MODEL)), _resident(w_bf.shape),
                  _resident(alb.shape), _resident(b_f.shape)],
        out_specs=[tok_block(A_QK), tok_block(A_QK), tok_block(A_QK), tok_block(A_V), tok_block(A_V)],
        out_shape=[
            jax.ShapeDtypeStruct(shape(A_QK), F32),
            jax.ShapeDtypeStruct(shape(A_QK), F32),
            jax.ShapeDtypeStruct(shape(A_QK), F32),
            jax.ShapeDtypeStruct(shape(A_V), act_dtype),
            jax.ShapeDtypeStruct(shape(A_V), act_dtype),
        ],
        compiler_params=_params(("parallel",)),
        name="l0_inproj",
    )(x, ada, nw, w_bf, alb, b_f)


def _scan_levels(chunk):
    return [1 << l for l in range(int(math.log2(chunk)))]


def _exponent_matrix(chunk):
    r = np.arange(chunk)[:, None]
    c = np.arange(chunk)[None, :]
    blocks = [c <= r, c > r]
    for s in _scan_levels(chunk):
        bound = (r & ~(2 * s - 1)) | s
        upper = (r & s) != 0
        blocks.append(np.where(upper, (c > bound) & (c <= r), (c > r) & (c <= bound)))
    return np.concatenate(blocks, axis=0).astype(np.float32)


def _hgrn_chunk(q, k, lf, v, state, emat, chunk):
    dk = q.shape[-1]
    ex = jnp.exp(jnp.dot(emat, lf, preferred_element_type=F32, precision=lax.Precision.HIGHEST))
    e_b = ex[0:chunk]
    e_u = ex[chunk:2 * chunk]
    row = lax.broadcasted_iota(jnp.int32, (chunk, 1), 0)
    col = lax.broadcasted_iota(jnp.int32, (1, chunk), 1)

    att = jnp.where(row == col, jnp.sum(q * k, axis=-1, keepdims=True), 0.0)
    for l, s in enumerate(_scan_levels(chunk)):
        g = ex[(2 + l) * chunk:(3 + l) * chunk]
        upper = (row & s) != 0
        xb = (jnp.where(upper, q, k) * g).astype(BF16)
        part = _dot(xb, xb, _NT)
        keep = upper & ((col & s) == 0) & ((row >> (l + 1)) == (col >> (l + 1)))
        att = att + jnp.where(keep, part, 0.0)

    o = _dot(att.astype(BF16), v.astype(BF16)) + _dot((q * e_b).astype(BF16), state.astype(BF16))

    e_last = e_b[chunk - 1:chunk]
    eye = lax.broadcasted_iota(jnp.int32, (dk, dk), 0) == lax.broadcasted_iota(jnp.int32, (dk, dk), 1)
    e_col = jnp.sum(jnp.where(eye, e_last, 0.0), axis=-1, keepdims=True)
    new_state = state * e_col + _dot((k * e_u).astype(BF16), v.astype(BF16), _TN)
    return o, new_state


SAFE_DECAY = 60.0
SPLIT_PAD = 32
SCAN_LINK = 2
SCAN_SEQS = 4


def _split_prefix_matrix(chunk, n_seq):
    t = np.kron(np.eye(n_seq, dtype=np.float32), _exponent_matrix(chunk)[:chunk])
    pad = (-3 * t.shape[1]) % SPLIT_PAD
    return np.concatenate([t, t, t, np.zeros((t.shape[0], pad), np.float32)], axis=1)


def _hgrn_chunk_bounded(q, k, lf, v, states, tmat, chunk, n_seq, linked):
    n_rows = n_seq * chunk
    hi = lf.astype(BF16)
    r1 = lf - hi.astype(F32)
    mid = r1.astype(BF16)
    lo = r1 - mid.astype(F32)
    parts = [hi.astype(F32), mid.astype(F32), lo]
    pad = tmat.shape[1] - 3 * n_rows
    if pad:
        parts.append(jnp.zeros((pad, lf.shape[1]), F32))
    b = _dot(tmat, jnp.concatenate(parts, axis=0).astype(BF16))
    e_b = jnp.exp(b)
    qb = q * e_b
    kn_f = k * jnp.exp(-b)
    kn = kn_f.astype(BF16)
    row = lax.broadcasted_iota(jnp.int32, (n_rows, n_rows), 0)
    col = lax.broadcasted_iota(jnp.int32, (n_rows, n_rows), 1)
    causal = row >= col
    if n_seq > 1:
        shift = chunk.bit_length() - 1
        causal = causal & ((row >> shift) == (col >> shift))
    eye = lax.broadcasted_iota(jnp.int32, (A_KDIM, A_KDIM), 0) == lax.broadcasted_iota(jnp.int32, (A_KDIM, A_KDIM), 1)
    ks = [slice(h * A_KDIM, (h + 1) * A_KDIM) for h in range(A_HEADS)]
    vs = [slice(h * A_VDIM, (h + 1) * A_VDIM) for h in range(A_HEADS)]
    sq = [slice(s * chunk, (s + 1) * chunk) for s in range(n_seq)]
    qbb, vb = qb.astype(BF16), v.astype(BF16)
    qb_s = [qbb] if n_seq == 1 else [qb[r].astype(BF16) for r in sq]
    ku_s = [(kn_f[r] * e_b[r.stop - 1:r.stop]).astype(BF16) for r in sq]
    v_s = [vb] if n_seq == 1 else [v[r].astype(BF16) for r in sq]
    atts = [jnp.where(causal, _dot(qbb[:, ks[h]], kn[:, ks[h]], _NT), 0.0).astype(BF16) for h in range(A_HEADS)]
    e_cols = [[jnp.sum(jnp.where(eye, e_b[(s + 1) * chunk - 1:(s + 1) * chunk, ks[h]], 0.0), axis=-1, keepdims=True)
               for h in range(A_HEADS)] for s in range(n_seq)]
    new_states = [[None] * A_HEADS for _ in range(n_seq)]
    carried = [[None] * A_HEADS for _ in range(n_seq)]
    for s in range(n_seq):
        for h in range(A_HEADS):
            state = new_states[s - 1][h] if (linked and s > 0) else states[0 if linked else s][h]
            carried[s][h] = _dot(qb_s[s][:, ks[h]], state.astype(BF16))
            new_states[s][h] = state * e_cols[s][h] + _dot(ku_s[s][:, ks[h]], v_s[s][:, vs[h]], _TN)
    outs = []
    for h in range(A_HEADS):
        inter = carried[0][h] if n_seq == 1 else jnp.concatenate([carried[s][h] for s in range(n_seq)], axis=0)
        outs.append(_dot(atts[h], vb[:, vs[h]]) + inter)
    return outs, new_states


def _hgrn_scan_kernel(*refs, chunk, n_chunks, has_state, n_seq, n_link):
    one_shot = n_seq > 0
    if has_state:
        q_ref, k_ref, lf_ref, v_ref, zs_ref, gw_ref, e_ref, t_ref, s0_ref, og_ref, so_ref, s_scr = refs
    else:
        q_ref, k_ref, lf_ref, v_ref, zs_ref, gw_ref, e_ref, t_ref, og_ref, so_ref, s_scr = refs
    t = pl.program_id(1)

    if not one_shot:
        @pl.when(t == 0)
        def _():
            if has_state:
                s_scr[...] = s0_ref[0]
            else:
                s_scr[...] = jnp.zeros_like(s_scr)

    def load_state(s, h):
        return s0_ref[s, h] if one_shot else s_scr[h]

    def store_state(s, h, val):
        if one_shot:
            so_ref[s, h] = val
        else:
            s_scr[h] = val

    gw = gw_ref[...]

    def finish(rows, h, o):
        vc = slice(h * A_VDIM, (h + 1) * A_VDIM)
        ms = jnp.mean(o * o, axis=-1, keepdims=True)
        og_ref[rows, vc] = (o * lax.rsqrt(ms + EPS) * gw * zs_ref[rows, vc]).astype(og_ref.dtype)

    seqs = range(max(n_seq, 1))

    def chunk_rows(c, s=None, n=1):
        if n_chunks > 1:
            return pl.ds(pl.multiple_of(c * (n * chunk), n * chunk), n * chunk)
        return slice(0, len(seqs) * chunk) if s is None else slice(s * chunk, (s + 1) * chunk)

    def bounded_body(c, carry):
        rows = chunk_rows(c, n=n_link)
        states = [[load_state(s, h) for h in range(A_HEADS)] for s in seqs]
        n_at_once = len(seqs) if one_shot else n_link
        outs, new_states = _hgrn_chunk_bounded(q_ref[rows, :], k_ref[rows, :], lf_ref[rows, :], v_ref[rows, :],
                                               states, t_ref[...], chunk, n_at_once, linked=not one_shot)
        for h in range(A_HEADS):
            if one_shot:
                for s in seqs:
                    store_state(s, h, new_states[s][h])
            else:
                store_state(0, h, new_states[-1][h])
            finish(rows, h, outs[h])
        return carry

    def general_body(c, carry):
        for s in seqs:
            rows = chunk_rows(c, s)
            for h in range(A_HEADS):
                kc = slice(h * A_KDIM, (h + 1) * A_KDIM)
                vc = slice(h * A_VDIM, (h + 1) * A_VDIM)
                o, s_new = _hgrn_chunk(q_ref[rows, kc], k_ref[rows, kc], lf_ref[rows, kc],
                                       v_ref[rows, vc], load_state(s, h), e_ref[...], chunk)
                store_state(s, h, s_new)
                finish(rows, h, o)
        return carry

    def run(body, trips):
        def go():
            if trips == 1:
                body(0, 0)
            else:
                lax.fori_loop(0, trips, body, 0)
        return go

    bounded = jnp.min(lf_ref[...]) * chunk >= -SAFE_DECAY
    lax.cond(bounded, run(bounded_body, n_chunks // n_link), run(general_body, n_chunks))

    if not one_shot:
        @pl.when(t == pl.num_programs(1) - 1)
        def _():
            so_ref[0] = s_scr[...]


def _hgrn_scan(q, k, lf, v, zs, gw, s0, batch, seq, tile, chunk, out_dtype, n_seq=0):
    n_t = seq // tile
    per_step = max(n_seq, 1)
    assert batch % per_step == 0 and (n_seq == 0 or (s0 is not None and seq == tile == chunk))
    n_link = SCAN_LINK if (n_seq == 0 and (tile // chunk) % SCAN_LINK == 0) else 1
    emat = jnp.asarray(_exponent_matrix(chunk))
    tmat = jnp.asarray(_split_prefix_matrix(chunk, max(per_step, n_link)), dtype=BF16)
    row_block = lambda w: pl.BlockSpec((tile * per_step, w), lambda b, t: (b * n_t + t, 0))
    state_block = pl.BlockSpec((per_step, A_HEADS, A_KDIM, A_VDIM), lambda b, t: (b, 0, 0, 0))
    in_specs = [row_block(A_QK), row_block(A_QK), row_block(A_QK), row_block(A_V), row_block(A_V),
                pl.BlockSpec((1, A_VDIM), lambda b, t: (0, 0)),
                pl.BlockSpec(emat.shape, lambda b, t: (0, 0)),
                pl.BlockSpec(tmat.shape, lambda b, t: (0, 0))]
    args = [q, k, lf, v, zs, gw, emat, tmat]
    if s0 is not None:
        in_specs.append(state_block)
        args.append(s0)
    kern = functools.partial(_hgrn_scan_kernel, chunk=chunk, n_chunks=tile // chunk,
                             has_state=s0 is not None, n_seq=n_seq, n_link=n_link)
    return pl.pallas_call(
        kern,
        grid=(batch // per_step, n_t),
        in_specs=in_specs,
        out_specs=[row_block(A_V), state_block],
        out_shape=[jax.ShapeDtypeStruct((batch * seq, A_V), out_dtype),
                   jax.ShapeDtypeStruct((batch, A_HEADS, A_KDIM, A_VDIM), F32)],
        scratch_shapes=[pltpu.VMEM((A_HEADS, A_KDIM, A_VDIM), F32)],
        compiler_params=_params(("parallel", "arbitrary")),
        name="hgrn_scan",
    )(*args)


def _outproj_kernel(*refs):
    *g_refs, w_ref, x_ref, ada_ref, o_ref = refs
    gs = [g_ref[...] for g_ref in g_refs]
    gs = [g.reshape(-1, g.shape[-1]).astype(BF16) for g in gs]
    ada = ada_ref[...]
    if len(x_ref.shape) == 2:
        ada = ada.reshape(1, ada.shape[-1])
    for c0 in range(0, D_MODEL, COL_TILE):
        cols = (Ellipsis, slice(c0, c0 + COL_TILE))
        y, k0 = None, 0
        for g in gs:
            part = _dot(g, w_ref[k0:k0 + g.shape[1], c0:c0 + COL_TILE].astype(BF16))
            y = part if y is None else y + part
            k0 += g.shape[1]
        gate = ada[..., 2 * D_MODEL + c0:2 * D_MODEL + c0 + COL_TILE]
        o_ref[cols] = x_ref[cols] + gate * y.reshape(x_ref.shape[:-1] + (COL_TILE,))


def _outproj(gs, w_bf, x, ada, batch, seq, tm):
    n_tiles, tok_block, ada_block, shape, _ = _token_specs(batch, seq, tm)
    assert sum(g.shape[-1] for g in gs) == w_bf.shape[0]
    return pl.pallas_call(
        _outproj_kernel,
        grid=(n_tiles,),
        in_specs=[tok_block(g.shape[-1]) for g in gs] + [_resident(w_bf.shape), tok_block(D_MODEL), ada_block],
        out_specs=tok_block(D_MODEL),
        out_shape=jax.ShapeDtypeStruct(shape(D_MODEL), F32),
        compiler_params=_params(("parallel",)),
        name="outproj",
    )(*gs, w_bf, x, ada)


def _rope_table_kernel(pos_ref, inv_ref, cos_ref, sin_ref):
    ang = pos_ref[...] * inv_ref[...]
    lane = lax.broadcasted_iota(jnp.int32, ang.shape, 1)
    cos_ref[...] = jnp.cos(ang)
    sin_ref[...] = jnp.where(lane < B_HD // 2, -jnp.sin(ang), jnp.sin(ang))


def _rope_tables(start, seq):
    half = B_HD // 2
    inv = ROPE_THETA ** (-jnp.arange(half, dtype=F32) / half)
    inv2 = jnp.concatenate([inv, inv]).reshape(1, B_HD)
    pos = (start + jnp.arange(seq, dtype=jnp.int32)).astype(F32).reshape(seq, 1)
    return pl.pallas_call(
        _rope_table_kernel,
        out_shape=[jax.ShapeDtypeStruct((seq, B_HD), F32)] * 2,
        name="rope_tables",
    )(pos, inv2)


KV_ROWS = 2 * B_HG


def _l1_inproj_kernel(x_ref, ada_ref, nw_ref, w_ref, qn_ref, kn_ref, cos_ref, sin_ref, q_ref, kv_ref, zs_ref,
                      *row_refs, row_tokens, row_first, per):
    h = _modulated_norm(x_ref, ada_ref, nw_ref)
    tm = h.shape[0]
    cos = cos_ref[...]
    sin = sin_ref[...]
    reps = tm // cos.shape[0]
    if reps > 1:
        cos = jnp.broadcast_to(cos[None], (reps,) + cos.shape).reshape(tm, B_HD)
        sin = jnp.broadcast_to(sin[None], (reps,) + sin.shape).reshape(tm, B_HD)

    def norm_rope(acc, nw):
        outs = []
        for hd in range(acc.shape[1] // B_HD):
            xh = acc[:, hd * B_HD:(hd + 1) * B_HD]
            ms = jnp.mean(xh * xh, axis=-1, keepdims=True)
            y = xh * lax.rsqrt(ms + EPS) * nw
            outs.append(y * cos + pltpu.roll(y, B_HD // 2, 1) * sin)
        return jnp.concatenate(outs, axis=-1)

    for c0 in range(0, w_ref.shape[1], B_GW):
        acc = _dot(h, w_ref[:, c0:c0 + B_GW].astype(BF16))
        seg, g = divmod(c0 // B_GW, len(B_GROUPS))
        if seg == 0:
            _put(q_ref, g * B_GW, norm_rope(acc, qn_ref[...]))
        elif seg == 1:
            _put(kv_ref, 2 * g * B_GW, norm_rope(acc, kn_ref[...]))
        elif seg == 2:
            _put(kv_ref, (2 * g + 1) * B_GW, acc)
        else:
            _put(zs_ref, g * B_GW, _silu(acc))

    for g, rows_ref in enumerate(row_refs):
        @pl.when(pl.program_id(0) % per >= row_first[g])
        def _(g=g, rows_ref=rows_ref):
            n = row_tokens[g]
            for j in range(KV_ROWS):
                cols = (Ellipsis, slice(2 * g * B_GW + j * B_HD, 2 * g * B_GW + (j + 1) * B_HD))
                rows_ref[pl.ds(j, n, stride=KV_ROWS), :] = kv_ref[cols].reshape(tm, B_HD)[tm - n:]


def _l1_inproj(x, ada, nw, w_bf, qn, kn, cos, sin, batch, seq, tm, act_dtype):
    n_tiles, tok_block, ada_block, shape, per = _token_specs(batch, seq, tm)
    out_specs = [tok_block(B_WIDTH), tok_block(2 * B_WIDTH), tok_block(B_WIDTH)]
    out_shape = [jax.ShapeDtypeStruct(shape(B_WIDTH), F32),
                 jax.ShapeDtypeStruct(shape(2 * B_WIDTH), F32),
                 jax.ShapeDtypeStruct(shape(B_WIDTH), act_dtype)]
    row_tokens, row_first = (), ()
    if per is None:
        tab_block = _resident((seq, B_HD))
        per = 1
        for _ in B_GROUPS:
            out_specs.append(pl.BlockSpec((tm * KV_ROWS, B_HD), lambda i: (i, 0)))
            out_shape.append(jax.ShapeDtypeStruct((batch * seq * KV_ROWS, B_HD), F32))
            row_tokens += (tm,)
            row_first += (0,)
    else:
        tab_block = pl.BlockSpec((tm, B_HD), lambda i: (i % per, 0))
        for window, _ in B_GROUPS:
            keep = min(window, seq)
            blk = min(keep, tm)
            n_blk = keep // blk
            first = per - n_blk
            out_specs.append(pl.BlockSpec(
                (blk * KV_ROWS, B_HD),
                lambda i, n_blk=n_blk, first=first: ((i // per) * n_blk + jnp.clip(i % per - first, 0, n_blk - 1), 0)))
            out_shape.append(jax.ShapeDtypeStruct((batch * keep * KV_ROWS, B_HD), F32))
            row_tokens += (blk,)
            row_first += (first,)
    return pl.pallas_call(
        functools.partial(_l1_inproj_kernel, row_tokens=row_tokens, row_first=row_first, per=per),
        grid=(n_tiles,),
        in_specs=[tok_block(D_MODEL), ada_block, _resident((1, D_MODEL)), _resident(w_bf.shape),
                  _resident((1, B_HD)), _resident((1, B_HD)), tab_block, tab_block],
        out_specs=out_specs,
        out_shape=out_shape,
        compiler_params=_params(("arbitrary",)),
        name="l1_inproj",
    )(x, ada, nw, w_bf, qn, kn, cos, sin)


Q_SUB = 128
ATTN_TILE = 2048
ATTN_BATCH = 8
MIX_ROWS = 256


def _attend(qs, k2s, v2s, valids):
    scale = B_HD ** -0.5
    ss = [jnp.where(ok, _dot(q, k2, _NT) * scale, -jnp.inf) for q, k2, ok in zip(qs, k2s, valids)]
    ms = [jnp.max(s, axis=-1, keepdims=True) for s in ss]
    ps = [jnp.exp(s - m) for s, m in zip(ss, ms)]
    ls = [jnp.sum(p, axis=-1, keepdims=True) for p in ps]
    os = [_dot(p.astype(BF16), v2) * (1.0 / l) for p, v2, l in zip(ps, v2s, ls)]
    lses = [m + jnp.log(l) for m, l in zip(ms, ls)]
    return os, lses


def _prompt_attn_kernel(*refs):
    n_g = len(B_GROUPS)
    ins, zs_refs = refs[:5 * n_g], refs[5 * n_g:6 * n_g]
    out_refs, (o_scr, lse_scr) = refs[6 * n_g:7 * n_g], refs[7 * n_g:]
    first_key = jnp.where(pl.program_id(2) == 0, Q_SUB, 0)
    row = lax.broadcasted_iota(jnp.int32, (Q_SUB, 2 * Q_SUB), 0)
    col = lax.broadcasted_iota(jnp.int32, (Q_SUB, 2 * Q_SUB), 1)
    band = (col >= row) & (col <= row + Q_SUB)
    band_first = band & (col >= first_key)

    for g, (_, dil) in enumerate(B_GROUPS):
        q_ref, kc_ref, vc_ref, kp_ref, vp_ref = ins[5 * g:5 * g + 5]
        n_sb = ATTN_TILE // (Q_SUB * dil)

        def rows_of(r, first, n):
            start = r + first * dil
            return pl.ds(start, n) if dil == 1 else pl.ds(start, n, stride=dil)

        work = []
        for r in range(dil):
            cls = rows_of(r, 0, n_sb * Q_SUB)
            kr = jnp.concatenate([kp_ref[0, rows_of(r, 0, Q_SUB), :], kc_ref[0, cls, :]], axis=0).astype(BF16)
            vr = jnp.concatenate([vp_ref[0, rows_of(r, 0, Q_SUB), :], vc_ref[0, cls, :]], axis=0).astype(BF16)
            qr = q_ref[0, cls, :].astype(BF16)
            for sb in range(n_sb):
                work.append((rows_of(r, sb * Q_SUB, Q_SUB), qr[sb * Q_SUB:(sb + 1) * Q_SUB],
                             kr[sb * Q_SUB:(sb + 2) * Q_SUB], vr[sb * Q_SUB:(sb + 2) * Q_SUB],
                             band_first if sb == 0 else band))
        for i in range(0, len(work), ATTN_BATCH):
            part = work[i:i + ATTN_BATCH]
            os, lses = _attend([w[1] for w in part], [w[2] for w in part], [w[3] for w in part],
                               [w[4] for w in part])
            for w, o, lse in zip(part, os, lses):
                o_scr[g, w[0], :] = o
                lse_scr[g, w[0], :] = jnp.broadcast_to(lse, (Q_SUB, B_HD))

    for c in range(ATTN_TILE // MIX_ROWS):
        rows = slice(c * MIX_ROWS, (c + 1) * MIX_ROWS)
        lses = [lse_scr[g, rows, :] for g in range(n_g)]
        m = functools.reduce(jnp.maximum, lses)
        es = [jnp.exp(l - m) for l in lses]
        inv = 1.0 / functools.reduce(lambda a, b: a + b, es)
        for g in range(n_g):
            out_refs[g][0, rows, :] = (o_scr[g, rows, :] * (es[g] * inv) * zs_refs[g][0, rows, :]
                                       ).astype(out_refs[g].dtype)


def _prompt_attn(q, kv, zs, batch, seq):
    assert seq % ATTN_TILE == 0
    n_tiles = seq // ATTN_TILE
    tok = lambda c0: pl.BlockSpec((1, ATTN_TILE, B_HD), lambda b, h, i: (b, i, c0 + h))
    in_specs, args = [], []
    for g, (_, dil) in enumerate(B_GROUPS):
        back = Q_SUB * dil
        assert ATTN_TILE % back == 0
        k0, v0 = 2 * g * B_HG, (2 * g + 1) * B_HG
        prev = lambda c0, back=back: pl.BlockSpec(
            (1, back, B_HD), lambda b, h, i: (b, jnp.maximum(i * (ATTN_TILE // back) - 1, 0), c0 + h))
        in_specs += [tok(g * B_HG), tok(k0), tok(v0), prev(k0), prev(v0)]
        args += [q, kv, kv, kv, kv]
    in_specs += [tok(g * B_HG) for g in range(len(B_GROUPS))]
    args += [zs] * len(B_GROUPS)
    out = pl.BlockSpec((1, ATTN_TILE, B_HD), lambda b, h, i: (b, i, h))
    return pl.pallas_call(
        _prompt_attn_kernel,
        grid=(batch, B_HG, n_tiles),
        in_specs=in_specs,
        out_specs=[out] * len(B_GROUPS),
        out_shape=[jax.ShapeDtypeStruct((batch, seq, B_GW), BF16)] * len(B_GROUPS),
        scratch_shapes=[pltpu.VMEM((len(B_GROUPS), ATTN_TILE, B_HD), F32)] * 2,
        compiler_params=_params(("parallel", "parallel", "arbitrary")),
        name="prompt_attn",
    )(*args)


SAMPLE_NB = 2


def _sample_masks(seq):
    hq = np.arange(B_HG * seq)[:, None] // seq
    iq = np.arange(B_HG * seq)[:, None] % seq
    masks = []
    for window, dil in B_GROUPS:
        n_tok = window if dil < seq else (window // dil) * seq
        col = np.arange(n_tok * KV_ROWS)[None, :]
        tok, is_v, head = col // KV_ROWS, (col // B_HG) % 2, col % B_HG
        if dil < seq:
            ok = (tok >= iq) & ((tok - iq) % dil == 0)
        else:
            ok = (tok % seq) == iq
        masks.append(np.where(ok & (is_v == 0) & (head == hq), 0.0, -np.inf).astype(np.float32))
    col = np.arange(B_HG * seq)[None, :]
    new = []
    for _, dil in B_GROUPS:
        ok = (col // seq == hq) & (col % seq <= iq) & ((iq - col % seq) % dil == 0)
        new.append(np.where(ok, 0.0, -np.inf).astype(np.float32))
    return masks, np.stack(new)


def _sample_attn_kernel(q_ref, kv_ref, zs_ref, c0_ref, c1_ref, c2_ref, m0_ref, m1_ref, m2_ref, mn_ref,
                        out_ref, *, seq, nb):
    scale = B_HD ** -0.5
    caches = (c0_ref, c1_ref, c2_ref)
    masks = (m0_ref, m1_ref, m2_ref)
    groups = range(len(B_GROUPS))

    def heads(ref, e, c0):
        return jnp.concatenate([ref[e, :, c0 + h * B_HD:c0 + (h + 1) * B_HD] for h in range(B_HG)],
                               axis=0).astype(BF16)

    def scores(e):
        qa = [heads(q_ref, e, g * B_GW) for g in groups]
        k_new = [heads(kv_ref, e, 2 * g * B_GW) for g in groups]
        rows = [caches[g][e].reshape(-1, B_HD).astype(BF16) for g in groups]
        s_c = [_dot(qa[g], rows[g], _NT) * scale + masks[g][...] for g in groups]
        s_n = [_dot(qa[g], k_new[g], _NT) * scale + mn_ref[g] for g in groups]
        return rows, s_c, s_n

    def weights(s_c, s_n):
        ms = [jnp.maximum(jnp.max(s_c[g], axis=-1, keepdims=True), jnp.max(s_n[g], axis=-1, keepdims=True))
              for g in groups]
        p_c = [jnp.exp(s_c[g] - ms[g]) for g in groups]
        p_n = [jnp.exp(s_n[g] - ms[g]) for g in groups]
        ls = [jnp.sum(p_c[g], axis=-1, keepdims=True) + jnp.sum(p_n[g], axis=-1, keepdims=True) for g in groups]
        p_v = [pltpu.roll(p_c[g], B_HG, 1).astype(BF16) for g in groups]
        return ms, ls, p_v, [p.astype(BF16) for p in p_n]

    def values(e, rows, p_v, p_n):
        v_new = [heads(kv_ref, e, (2 * g + 1) * B_GW) for g in groups]
        return [_dot(p_v[g], rows[g]) + _dot(p_n[g], v_new[g]) for g in groups]

    scored = [scores(e) for e in range(nb)]
    weighted = [weights(s_c, s_n) for _, s_c, s_n in scored]
    accs = [values(e, scored[e][0], weighted[e][2], weighted[e][3]) for e in range(nb)]
    for e in range(nb):
        ms, ls = weighted[e][0], weighted[e][1]
        outs = [accs[e][g] / ls[g] for g in groups]
        lses = [ms[g] + jnp.log(ls[g]) for g in groups]
        m = functools.reduce(jnp.maximum, lses)
        es = [jnp.exp(l - m) for l in lses]
        inv = 1.0 / (es[0] + es[1] + es[2])
        for g in groups:
            mixed = outs[g] * (es[g] * inv)
            for h in range(B_HG):
                cols = slice(g * B_GW + h * B_HD, g * B_GW + (h + 1) * B_HD)
                out_ref[e, :, cols] = mixed[h * seq:(h + 1) * seq] * zs_ref[e, :, cols]


def _sample_attn(q, kv, zs, caches, batch, seq, nb):
    assert batch % nb == 0
    masks, mask_new = _sample_masks(seq)
    specs, views = [], []
    for (window, dil), c in zip(B_GROUPS, caches):
        assert window == dil * (N_KEYS - 1) and c.shape[2] == window
        if dil >= seq:
            views.append(c.reshape(batch, window // dil, dil * KV_ROWS, B_HD))
            specs.append(pl.BlockSpec((nb, window // dil, seq * KV_ROWS, B_HD), lambda b: (b, 0, 0, 0)))
        else:
            assert seq % dil == 0
            views.append(c.reshape(batch, window * KV_ROWS, B_HD))
            specs.append(pl.BlockSpec((nb, window * KV_ROWS, B_HD), lambda b: (b, 0, 0)))
    const = lambda a: pl.BlockSpec(a.shape, lambda b: (0,) * a.ndim)
    tok = lambda w: pl.BlockSpec((nb, seq, w), lambda b: (b, 0, 0))
    return pl.pallas_call(
        functools.partial(_sample_attn_kernel, seq=seq, nb=nb),
        grid=(batch // nb,),
        in_specs=[tok(B_WIDTH), tok(2 * B_WIDTH), tok(B_WIDTH)] + specs
                 + [const(a) for a in masks] + [const(mask_new)],
        out_specs=tok(B_WIDTH),
        out_shape=jax.ShapeDtypeStruct((batch, seq, B_WIDTH), F32),
        compiler_params=_params(("parallel",)),
        name="sample_attn",
    )(q, kv, zs, *views, *[jnp.asarray(a) for a in masks], jnp.asarray(mask_new))


class _Tiles(NamedTuple):
    l0_in: int
    l1_in: int
    out: int
    scan: int
    chunk: int


LONG_SEQ = 512


def _tiles(seq):
    if seq >= LONG_SEQ:
        return _Tiles(l0_in=512, l1_in=256, out=1024, scan=512, chunk=64)
    return _Tiles(l0_in=256, l1_in=256, out=256, scan=seq, chunk=seq)


def _trunk(x, ada, start, state, caches, weights):
    (norm_w, alb, a_w_in, a_b_f, a_gw, a_w_out, b_w_in, b_qn, b_kn, b_w_out) = weights
    batch, seq, _ = x.shape
    tiles = _tiles(seq)
    flat = seq >= LONG_SEQ
    act = BF16 if flat else F32
    tok = (lambda a: a.reshape(batch * seq, a.shape[-1])) if flat else (lambda a: a)
    rows = lambda a: a.reshape(batch * seq, a.shape[-1])
    ada0 = ada[0].reshape(batch, 1, 3 * D_MODEL)
    ada1 = ada[1].reshape(batch, 1, 3 * D_MODEL)

    x0 = tok(x)
    q, k, lf, v, zs = _l0_inproj(x0, ada0, norm_w[0:1], a_w_in, alb, a_b_f, batch, seq, tiles.l0_in, act)
    og, s_new = _hgrn_scan(rows(q), rows(k), rows(lf), rows(v), rows(zs), a_gw, state,
                           batch, seq, tiles.scan, tiles.chunk, act, n_seq=0 if state is None else SCAN_SEQS)
    og = og if flat else og.reshape(batch, seq, A_V)
    x1 = _outproj([og], a_w_out, x0, ada0, batch, seq, tiles.out)

    cos, sin = _rope_tables(start, seq)
    qr, kv, zs1, *new_rows = _l1_inproj(x1, ada1, norm_w[1:2], b_w_in, b_qn, b_kn, cos, sin, batch, seq,
                                        tiles.l1_in, act)
    kv3 = kv.reshape(batch, seq, 2 * B_WIDTH)
    if caches is None:
        og1 = _prompt_attn(qr.reshape(batch, seq, B_WIDTH), kv3, zs1.reshape(batch, seq, B_WIDTH), batch, seq)
        og1 = [o.reshape(batch * seq, B_GW) for o in og1]
    else:
        og1 = [_sample_attn(qr, kv, zs1, caches, batch, seq, SAMPLE_NB)]
    x2 = _outproj(og1, b_w_out, x1, ada1, batch, seq, tiles.out)

    kv_rows = [r.reshape(batch, min(window, seq), 2, B_HG, B_HD) for r, (window, _) in zip(new_rows, B_GROUPS)]
    return x2.reshape(batch, seq, D_MODEL), s_new, kv_rows


def kernel(x_prompt, x_sample, state_hgrn, cache_kv_w128, cache_kv_w512, cache_kv_w2048, c_prompt, c_sample,
           norm_w, ada_w, ada_b, a_lower_bounds, a_w_in, a_b_f, a_g_norm_w, a_w_out, b_w_in, b_q_norm_w,
           b_k_norm_w, b_w_out):
    bp, lp, _ = x_prompt.shape
    bs, ls, _ = x_sample.shape

    pad = (-(bp + bs)) % 8
    c_all = jnp.concatenate([c_prompt, c_sample, jnp.zeros((pad, D_MODEL), F32)], axis=0)
    ada = _ada_vectors(c_all, ada_w, ada_b)
    ada_p, ada_s = ada[:, :bp], ada[:, bp:bp + bs]

    weights = (norm_w, a_lower_bounds, a_w_in[0], a_b_f, a_g_norm_w, a_w_out[0],
               b_w_in[0], b_q_norm_w, b_k_norm_w, b_w_out[0])

    y_p, s_p, kv_p = _trunk(x_prompt, ada_p, 0, None, None, weights)
    caches = (cache_kv_w128, cache_kv_w512, cache_kv_w2048)
    y_s, s_s, kv_s = _trunk(x_sample, ada_s, PAST_LEN, state_hgrn[0], caches, weights)

    kv_out = []
    for g in range(len(B_GROUPS)):
        kv_out.append(kv_p[g][None])
        kv_out.append(kv_s[g][None])
    return (y_p, y_s, s_p[None], s_s[None], *kv_out)
```

```python
import functools
import math
from typing import NamedTuple

import numpy as np
import jax
import jax.numpy as jnp
from jax import lax
from jax.experimental import pallas as pl
from jax.experimental.pallas import tpu as pltpu

F32 = jnp.float32
BF16 = jnp.bfloat16

D_MODEL = 1024
EPS = 1e-6
A_HEADS = 8
A_KDIM = 128
A_VDIM = 256
A_QK = A_HEADS * A_KDIM
A_V = A_HEADS * A_VDIM
B_GROUPS = ((128, 1), (512, 4), (2048, 16))
B_HG = 4
B_HD = 128
B_GW = B_HG * B_HD
B_WIDTH = len(B_GROUPS) * B_GW
ROPE_THETA = 10000.0
PAST_LEN = 2048
N_KEYS = 129

COL_TILE = 512
VMEM_LIMIT = 56 * 1024 * 1024

_NT = (((1,), (1,)), ((), ()))
_TN = (((0,), (0,)), ((), ()))


def _sigmoid(x):
    return 1.0 / (1.0 + jnp.exp(-x))


def _silu(x):
    h = 0.5 * x
    return h * jnp.tanh(h) + h


def _dot(a, b, dims=None):
    if dims is None:
        return jnp.dot(a, b, preferred_element_type=F32)
    return lax.dot_general(a, b, dims, preferred_element_type=F32)


def _params(sem):
    return pltpu.CompilerParams(dimension_semantics=sem, vmem_limit_bytes=VMEM_LIMIT)


def _ada_kernel(c_ref, w_ref, b_ref, o_ref):
    a = _silu(c_ref[...]).astype(BF16)
    o_ref[0] = _dot(a, w_ref[0].astype(BF16)) + b_ref[0]


def _ada_vectors(c_all, ada_w, ada_b):
    n_layers, _, width = ada_w.shape
    rows = c_all.shape[0]
    return pl.pallas_call(
        _ada_kernel,
        grid=(n_layers, width // COL_TILE),
        in_specs=[
            pl.BlockSpec((rows, D_MODEL), lambda l, j: (0, 0)),
            pl.BlockSpec((1, D_MODEL, COL_TILE), lambda l, j: (l, 0, j)),
            pl.BlockSpec((1, 1, COL_TILE), lambda l, j: (l, 0, j)),
        ],
        out_specs=pl.BlockSpec((1, rows, COL_TILE), lambda l, j: (l, 0, j)),
        out_shape=jax.ShapeDtypeStruct((n_layers, rows, width), F32),
        compiler_params=_params(("parallel", "parallel")),
        name="ada_vectors",
    )(c_all, ada_w, ada_b.reshape(n_layers, 1, width))


def _modulated_norm(x_ref, ada_ref, nw_ref):
    x = x_ref[...]
    ada = ada_ref[...]
    if x.ndim == 2:
        ada = ada.reshape(1, ada.shape[-1])
    shift = ada[..., :D_MODEL]
    scale = ada[..., D_MODEL:2 * D_MODEL]
    ms = jnp.mean(x * x, axis=-1, keepdims=True)
    y = x * lax.rsqrt(ms + EPS) * nw_ref[...].reshape((1,) * (x.ndim - 1) + (D_MODEL,))
    h = y * (1.0 + scale) + shift
    return h.reshape(-1, D_MODEL).astype(BF16)


def _token_specs(batch, seq, tm):
    if seq >= tm:
        assert seq % tm == 0
        per = seq // tm
        n_tiles = batch * per
        tok_block = lambda w: pl.BlockSpec((tm, w), lambda i: (i, 0))
        ada_block = pl.BlockSpec((1, 1, 3 * D_MODEL), lambda i: (i // per, 0, 0))
        shape = lambda w: (batch * seq, w)
        return n_tiles, tok_block, ada_block, shape, per
    tb = tm // seq
    assert batch % tb == 0
    n_tiles = batch // tb
    tok_block = lambda w: pl.BlockSpec((tb, seq, w), lambda i: (i, 0, 0))
    ada_block = pl.BlockSpec((tb, 1, 3 * D_MODEL), lambda i: (i, 0, 0))
    shape = lambda w: (batch, seq, w)
    return n_tiles, tok_block, ada_block, shape, None


def _resident(shape):
    return pl.BlockSpec(shape, lambda i: (0,) * len(shape), pipeline_mode=pl.Buffered(1))


def _put(ref, c0, val):
    ref[(Ellipsis, slice(c0, c0 + val.shape[-1]))] = val.reshape(ref.shape[:-1] + val.shape[-1:]).astype(ref.dtype)


def _l0_inproj_kernel(x_ref, ada_ref, nw_ref, w_ref, alb_ref, bf_ref, q_ref, k_ref, lf_ref, v_ref, zs_ref):
    h = _modulated_norm(x_ref, ada_ref, nw_ref)
    a = alb_ref[...]
    m = jnp.max(a, axis=0, keepdims=True)
    e = jnp.exp(a - m)
    lb = e[0:1] / jnp.sum(e, axis=0, keepdims=True)
    for c0 in range(0, w_ref.shape[1], COL_TILE):
        acc = _dot(h, w_ref[:, c0:c0 + COL_TILE].astype(BF16))
        if c0 < A_QK:
            _put(q_ref, c0, _silu(acc))
        elif c0 < 2 * A_QK:
            c = c0 - A_QK
            f = lb[:, c:c + COL_TILE] + (1.0 - lb[:, c:c + COL_TILE]) * _sigmoid(acc + bf_ref[:, c:c + COL_TILE])
            _put(lf_ref, c, jnp.log(f))
            _put(k_ref, c, 1.0 - f)
        elif c0 < 2 * A_QK + A_V:
            _put(v_ref, c0 - 2 * A_QK, acc)
        else:
            _put(zs_ref, c0 - 2 * A_QK - A_V, _silu(acc))


def _l0_inproj(x, ada, nw, w_bf, alb, b_f, batch, seq, tm, act_dtype):
    n_tiles, tok_block, ada_block, shape, _ = _token_specs(batch, seq, tm)
    return pl.pallas_call(
        _l0_inproj_kernel,
        grid=(n_tiles,),
        in_specs=[tok_block(D_MODEL), ada_block, _resident((1, D_MODEL)), _resident(w_bf.shape),
                  _resident(alb.shape), _resident(b_f.shape)],
        out_specs=[tok_block(A_QK), tok_block(A_QK), tok_block(A_QK), tok_block(A_V), tok_block(A_V)],
        out_shape=[
            jax.ShapeDtypeStruct(shape(A_QK), F32),
            jax.ShapeDtypeStruct(shape(A_QK), F32),
            jax.ShapeDtypeStruct(shape(A_QK), F32),
            jax.ShapeDtypeStruct(shape(A_V), act_dtype),
            jax.ShapeDtypeStruct(shape(A_V), act_dtype),
        ],
        compiler_params=_params(("parallel",)),
        name="l0_inproj",
    )(x, ada, nw, w_bf, alb, b_f)


def _scan_levels(chunk):
    return [1 << l for l in range(int(math.log2(chunk)))]


def _exponent_matrix(chunk):
    r = np.arange(chunk)[:, None]
    c = np.arange(chunk)[None, :]
    blocks = [c <= r, c > r]
    for s in _scan_levels(chunk):
        bound = (r & ~(2 * s - 1)) | s
        upper = (r & s) != 0
        blocks.append(np.where(upper, (c > bound) & (c <= r), (c > r) & (c <= bound)))
    return np.concatenate(blocks, axis=0).astype(np.float32)


def _hgrn_chunk(q, k, lf, v, state, emat, chunk):
    dk = q.shape[-1]
    ex = jnp.exp(jnp.dot(emat, lf, preferred_element_type=F32, precision=lax.Precision.HIGHEST))
    e_b = ex[0:chunk]
    e_u = ex[chunk:2 * chunk]
    row = lax.broadcasted_iota(jnp.int32, (chunk, 1), 0)
    col = lax.broadcasted_iota(jnp.int32, (1, chunk), 1)

    att = jnp.where(row == col, jnp.sum(q * k, axis=-1, keepdims=True), 0.0)
    for l, s in enumerate(_scan_levels(chunk)):
        g = ex[(2 + l) * chunk:(3 + l) * chunk]
        upper = (row & s) != 0
        xb = (jnp.where(upper, q, k) * g).astype(BF16)
        part = _dot(xb, xb, _NT)
        keep = upper & ((col & s) == 0) & ((row >> (l + 1)) == (col >> (l + 1)))
        att = att + jnp.where(keep, part, 0.0)

    o = _dot(att.astype(BF16), v.astype(BF16)) + _dot((q * e_b).astype(BF16), state.astype(BF16))

    e_last = e_b[chunk - 1:chunk]
    eye = lax.broadcasted_iota(jnp.int32, (dk, dk), 0) == lax.broadcasted_iota(jnp.int32, (dk, dk), 1)
    e_col = jnp.sum(jnp.where(eye, e_last, 0.0), axis=-1, keepdims=True)
    new_state = state * e_col + _dot((k * e_u).astype(BF16), v.astype(BF16), _TN)
    return o, new_state


SAFE_DECAY = 80.0
SPLIT_PAD = 32
SCAN_LINK = 2
SCAN_SEQS = 8


def _split_prefix_matrix(chunk, n_seq):
    t = np.kron(np.eye(n_seq, dtype=np.float32), _exponent_matrix(chunk)[:chunk])
    pad = (-3 * t.shape[1]) % SPLIT_PAD
    return np.concatenate([t, t, t, np.zeros((t.shape[0], pad), np.float32)], axis=1)


def _hgrn_chunk_bounded(q, k, lf, v, states, tmat, chunk, n_seq, linked):
    n_rows = n_seq * chunk
    hi = lf.astype(BF16)
    r1 = lf - hi.astype(F32)
    mid = r1.astype(BF16)
    lo = r1 - mid.astype(F32)
    parts = [hi.astype(F32), mid.astype(F32), lo]
    pad = tmat.shape[1] - 3 * n_rows
    if pad:
        parts.append(jnp.zeros((pad, lf.shape[1]), F32))
    b = _dot(tmat, jnp.concatenate(parts, axis=0).astype(BF16))
    e_b = jnp.exp(b)
    qb = q * e_b
    kn_f = k * jnp.exp(-b)
    kn = kn_f.astype(BF16)
    row = lax.broadcasted_iota(jnp.int32, (n_rows, n_rows), 0)
    col = lax.broadcasted_iota(jnp.int32, (n_rows, n_rows), 1)
    causal = row >= col
    if n_seq > 1:
        shift = chunk.bit_length() - 1
        causal = causal & ((row >> shift) == (col >> shift))
    eye = lax.broadcasted_iota(jnp.int32, (A_KDIM, A_KDIM), 0) == lax.broadcasted_iota(jnp.int32, (A_KDIM, A_KDIM), 1)
    ks = [slice(h * A_KDIM, (h + 1) * A_KDIM) for h in range(A_HEADS)]
    vs = [slice(h * A_VDIM, (h + 1) * A_VDIM) for h in range(A_HEADS)]
    sq = [slice(s * chunk, (s + 1) * chunk) for s in range(n_seq)]
    qbb, vb = qb.astype(BF16), v.astype(BF16)
    qb_s = [qbb] if n_seq == 1 else [qb[r].astype(BF16) for r in sq]
    ku_s = [(kn_f[r] * e_b[r.stop - 1:r.stop]).astype(BF16) for r in sq]
    v_s = [vb] if n_seq == 1 else [v[r].astype(BF16) for r in sq]
    atts = [jnp.where(causal, _dot(qbb[:, ks[h]], kn[:, ks[h]], _NT), 0.0).astype(BF16) for h in range(A_HEADS)]
    e_cols = [[jnp.sum(jnp.where(eye, e_b[(s + 1) * chunk - 1:(s + 1) * chunk, ks[h]], 0.0), axis=-1, keepdims=True)
               for h in range(A_HEADS)] for s in range(n_seq)]
    new_states = [[None] * A_HEADS for _ in range(n_seq)]
    carried = [[None] * A_HEADS for _ in range(n_seq)]
    for s in range(n_seq):
        for h in range(A_HEADS):
            state = new_states[s - 1][h] if (linked and s > 0) else states[0 if linked else s][h]
            carried[s][h] = _dot(qb_s[s][:, ks[h]], state.astype(BF16))
            new_states[s][h] = state * e_cols[s][h] + _dot(ku_s[s][:, ks[h]], v_s[s][:, vs[h]], _TN)
    outs = []
    for h in range(A_HEADS):
        inter = carried[0][h] if n_seq == 1 else jnp.concatenate([carried[s][h] for s in range(n_seq)], axis=0)
        outs.append(_dot(atts[h], vb[:, vs[h]]) + inter)
    return outs, new_states


def _hgrn_scan_kernel(*refs, chunk, n_chunks, has_state, n_seq, n_link):
    one_shot = n_seq > 0
    if has_state:
        q_ref, k_ref, lf_ref, v_ref, zs_ref, gw_ref, e_ref, t_ref, s0_ref, og_ref, so_ref, s_scr = refs
    else:
        q_ref, k_ref, lf_ref, v_ref, zs_ref, gw_ref, e_ref, t_ref, og_ref, so_ref, s_scr = refs
    t = pl.program_id(1)

    if not one_shot:
        @pl.when(t == 0)
        def _():
            if has_state:
                s_scr[...] = s0_ref[0]
            else:
                s_scr[...] = jnp.zeros_like(s_scr)

    def load_state(s, h):
        return s0_ref[s, h] if one_shot else s_scr[h]

    def store_state(s, h, val):
        if one_shot:
            so_ref[s, h] = val
        else:
            s_scr[h] = val

    gw = gw_ref[...]

    def finish(rows, h, o):
        vc = slice(h * A_VDIM, (h + 1) * A_VDIM)
        ms = jnp.mean(o * o, axis=-1, keepdims=True)
        og_ref[rows, vc] = (o * lax.rsqrt(ms + EPS) * gw * zs_ref[rows, vc]).astype(og_ref.dtype)

    seqs = range(max(n_seq, 1))

    def chunk_rows(c, s=None, n=1):
        if n_chunks > 1:
            return pl.ds(pl.multiple_of(c * (n * chunk), n * chunk), n * chunk)
        return slice(0, len(seqs) * chunk) if s is None else slice(s * chunk, (s + 1) * chunk)

    def bounded_body(c, carry):
        rows = chunk_rows(c, n=n_link)
        states = [[load_state(s, h) for h in range(A_HEADS)] for s in seqs]
        n_at_once = len(seqs) if one_shot else n_link
        outs, new_states = _hgrn_chunk_bounded(q_ref[rows, :], k_ref[rows, :], lf_ref[rows, :], v_ref[rows, :],
                                               states, t_ref[...], chunk, n_at_once, linked=not one_shot)
        for h in range(A_HEADS):
            if one_shot:
                for s in seqs:
                    store_state(s, h, new_states[s][h])
            else:
                store_state(0, h, new_states[-1][h])
            finish(rows, h, outs[h])
        return carry

    def general_body(c, carry):
        for s in seqs:
            rows = chunk_rows(c, s)
            for h in range(A_HEADS):
                kc = slice(h * A_KDIM, (h + 1) * A_KDIM)
                vc = slice(h * A_VDIM, (h + 1) * A_VDIM)
                o, s_new = _hgrn_chunk(q_ref[rows, kc], k_ref[rows, kc], lf_ref[rows, kc],
                                       v_ref[rows, vc], load_state(s, h), e_ref[...], chunk)
                store_state(s, h, s_new)
                finish(rows, h, o)
        return carry

    def run(body, trips):
        def go():
            if trips == 1:
                body(0, 0)
            else:
                lax.fori_loop(0, trips, body, 0)
        return go

    bounded = jnp.min(lf_ref[...]) * chunk >= -SAFE_DECAY
    lax.cond(bounded, run(bounded_body, n_chunks // n_link), run(general_body, n_chunks))

    if not one_shot:
        @pl.when(t == pl.num_programs(1) - 1)
        def _():
            so_ref[0] = s_scr[...]


def _hgrn_scan(q, k, lf, v, zs, gw, s0, batch, seq, tile, chunk, out_dtype, n_seq=0):
    n_t = seq // tile
    per_step = max(n_seq, 1)
    assert batch % per_step == 0 and (n_seq == 0 or (s0 is not None and seq == tile == chunk))
    n_link = SCAN_LINK if (n_seq == 0 and (tile // chunk) % SCAN_LINK == 0) else 1
    emat = jnp.asarray(_exponent_matrix(chunk))
    tmat = jnp.asarray(_split_prefix_matrix(chunk, max(per_step, n_link)), dtype=BF16)
    row_block = lambda w: pl.BlockSpec((tile * per_step, w), lambda b, t: (b * n_t + t, 0))
    state_block = pl.BlockSpec((per_step, A_HEADS, A_KDIM, A_VDIM), lambda b, t: (b, 0, 0, 0))
    in_specs = [row_block(A_QK), row_block(A_QK), row_block(A_QK), row_block(A_V), row_block(A_V),
                pl.BlockSpec((1, A_VDIM), lambda b, t: (0, 0)),
                pl.BlockSpec(emat.shape, lambda b, t: (0, 0)),
                pl.BlockSpec(tmat.shape, lambda b, t: (0, 0))]
    args = [q, k, lf, v, zs, gw, emat, tmat]
    if s0 is not None:
        in_specs.append(state_block)
        args.append(s0)
    kern = functools.partial(_hgrn_scan_kernel, chunk=chunk, n_chunks=tile // chunk,
                             has_state=s0 is not None, n_seq=n_seq, n_link=n_link)
    return pl.pallas_call(
        kern,
        grid=(batch // per_step, n_t),
        in_specs=in_specs,
        out_specs=[row_block(A_V), state_block],
        out_shape=[jax.ShapeDtypeStruct((batch * seq, A_V), out_dtype),
                   jax.ShapeDtypeStruct((batch, A_HEADS, A_KDIM, A_VDIM), F32)],
        scratch_shapes=[pltpu.VMEM((A_HEADS, A_KDIM, A_VDIM), F32)],
        compiler_params=_params(("parallel", "arbitrary")),
        name="hgrn_scan",
    )(*args)


def _outproj_kernel(*refs):
    *g_refs, w_ref, x_ref, ada_ref, o_ref = refs
    gs = [g_ref[...] for g_ref in g_refs]
    gs = [g.reshape(-1, g.shape[-1]).astype(BF16) for g in gs]
    ada = ada_ref[...]
    if len(x_ref.shape) == 2:
        ada = ada.reshape(1, ada.shape[-1])
    for c0 in range(0, D_MODEL, COL_TILE):
        cols = (Ellipsis, slice(c0, c0 + COL_TILE))
        y, k0 = None, 0
        for g in gs:
            part = _dot(g, w_ref[k0:k0 + g.shape[1], c0:c0 + COL_TILE].astype(BF16))
            y = part if y is None else y + part
            k0 += g.shape[1]
        gate = ada[..., 2 * D_MODEL + c0:2 * D_MODEL + c0 + COL_TILE]
        o_ref[cols] = x_ref[cols] + gate * y.reshape(x_ref.shape[:-1] + (COL_TILE,))


def _outproj(gs, w_bf, x, ada, batch, seq, tm):
    n_tiles, tok_block, ada_block, shape, _ = _token_specs(batch, seq, tm)
    assert sum(g.shape[-1] for g in gs) == w_bf.shape[0]
    return pl.pallas_call(
        _outproj_kernel,
        grid=(n_tiles,),
        in_specs=[tok_block(g.shape[-1]) for g in gs] + [_resident(w_bf.shape), tok_block(D_MODEL), ada_block],
        out_specs=tok_block(D_MODEL),
        out_shape=jax.ShapeDtypeStruct(shape(D_MODEL), F32),
        compiler_params=_params(("parallel",)),
        name="outproj",
    )(*gs, w_bf, x, ada)


def _rope_table_kernel(pos_ref, inv_ref, cos_ref, sin_ref):
    ang = pos_ref[...] * inv_ref[...]
    lane = lax.broadcasted_iota(jnp.int32, ang.shape, 1)
    cos_ref[...] = jnp.cos(ang)
    sin_ref[...] = jnp.where(lane < B_HD // 2, -jnp.sin(ang), jnp.sin(ang))


def _rope_tables(start, seq):
    half = B_HD // 2
    inv = ROPE_THETA ** (-jnp.arange(half, dtype=F32) / half)
    inv2 = jnp.concatenate([inv, inv]).reshape(1, B_HD)
    pos = (start + jnp.arange(seq, dtype=jnp.int32)).astype(F32).reshape(seq, 1)
    return pl.pallas_call(
        _rope_table_kernel,
        out_shape=[jax.ShapeDtypeStruct((seq, B_HD), F32)] * 2,
        name="rope_tables",
    )(pos, inv2)


KV_ROWS = 2 * B_HG


def _l1_inproj_kernel(x_ref, ada_ref, nw_ref, w_ref, qn_ref, kn_ref, cos_ref, sin_ref, q_ref, kv_ref, zs_ref,
                      *row_refs, row_tokens, row_first, per):
    h = _modulated_norm(x_ref, ada_ref, nw_ref)
    tm = h.shape[0]
    cos = cos_ref[...]
    sin = sin_ref[...]
    reps = tm // cos.shape[0]
    if reps > 1:
        cos = jnp.broadcast_to(cos[None], (reps,) + cos.shape).reshape(tm, B_HD)
        sin = jnp.broadcast_to(sin[None], (reps,) + sin.shape).reshape(tm, B_HD)

    def norm_rope(acc, nw):
        outs = []
        for hd in range(acc.shape[1] // B_HD):
            xh = acc[:, hd * B_HD:(hd + 1) * B_HD]
            ms = jnp.mean(xh * xh, axis=-1, keepdims=True)
            y = xh * lax.rsqrt(ms + EPS) * nw
            outs.append(y * cos + pltpu.roll(y, B_HD // 2, 1) * sin)
        return jnp.concatenate(outs, axis=-1)

    for c0 in range(0, w_ref.shape[1], B_GW):
        acc = _dot(h, w_ref[:, c0:c0 + B_GW].astype(BF16))
        seg, g = divmod(c0 // B_GW, len(B_GROUPS))
        if seg == 0:
            _put(q_ref, g * B_GW, norm_rope(acc, qn_ref[...]))
        elif seg == 1:
            _put(kv_ref, 2 * g * B_GW, norm_rope(acc, kn_ref[...]))
        elif seg == 2:
            _put(kv_ref, (2 * g + 1) * B_GW, acc)
        else:
            _put(zs_ref, g * B_GW, _silu(acc))

    for g, rows_ref in enumerate(row_refs):
        @pl.when(pl.program_id(0) % per >= row_first[g])
        def _(g=g, rows_ref=rows_ref):
            n = row_tokens[g]
            for j in range(KV_ROWS):
                cols = (Ellipsis, slice(2 * g * B_GW + j * B_HD, 2 * g * B_GW + (j + 1) * B_HD))
                rows_ref[pl.ds(j, n, stride=KV_ROWS), :] = kv_ref[cols].reshape(tm, B_HD)[tm - n:]


def _l1_inproj(x, ada, nw, w_bf, qn, kn, cos, sin, batch, seq, tm, act_dtype):
    n_tiles, tok_block, ada_block, shape, per = _token_specs(batch, seq, tm)
    out_specs = [tok_block(B_WIDTH), tok_block(2 * B_WIDTH), tok_block(B_WIDTH)]
    out_shape = [jax.ShapeDtypeStruct(shape(B_WIDTH), F32),
                 jax.ShapeDtypeStruct(shape(2 * B_WIDTH), F32),
                 jax.ShapeDtypeStruct(shape(B_WIDTH), act_dtype)]
    row_tokens, row_first = (), ()
    if per is None:
        tab_block = _resident((seq, B_HD))
        per = 1
        for _ in B_GROUPS:
            out_specs.append(pl.BlockSpec((tm * KV_ROWS, B_HD), lambda i: (i, 0)))
            out_shape.append(jax.ShapeDtypeStruct((batch * seq * KV_ROWS, B_HD), F32))
            row_tokens += (tm,)
            row_first += (0,)
    else:
        tab_block = pl.BlockSpec((tm, B_HD), lambda i: (i % per, 0))
        for window, _ in B_GROUPS:
            keep = min(window, seq)
            blk = min(keep, tm)
            n_blk = keep // blk
            first = per - n_blk
            out_specs.append(pl.BlockSpec(
                (blk * KV_ROWS, B_HD),
                lambda i, n_blk=n_blk, first=first: ((i // per) * n_blk + jnp.clip(i % per - first, 0, n_blk - 1), 0)))
            out_shape.append(jax.ShapeDtypeStruct((batch * keep * KV_ROWS, B_HD), F32))
            row_tokens += (blk,)
            row_first += (first,)
    return pl.pallas_call(
        functools.partial(_l1_inproj_kernel, row_tokens=row_tokens, row_first=row_first, per=per),
        grid=(n_tiles,),
        in_specs=[tok_block(D_MODEL), ada_block, _resident((1, D_MODEL)), _resident(w_bf.shape),
                  _resident((1, B_HD)), _resident((1, B_HD)), tab_block, tab_block],
        out_specs=out_specs,
        out_shape=out_shape,
        compiler_params=_params(("arbitrary",)),
        name="l1_inproj",
    )(x, ada, nw, w_bf, qn, kn, cos, sin)


Q_SUB = 128
ATTN_TILE = 2048
ATTN_BATCH = 8
MIX_ROWS = 256


def _attend(qs, k2s, v2s, valids):
    scale = B_HD ** -0.5
    ss = [jnp.where(ok, _dot(q, k2, _NT) * scale, -jnp.inf) for q, k2, ok in zip(qs, k2s, valids)]
    ms = [jnp.max(s, axis=-1, keepdims=True) for s in ss]
    ps = [jnp.exp(s - m) for s, m in zip(ss, ms)]
    ls = [jnp.sum(p, axis=-1, keepdims=True) for p in ps]
    os = [_dot(p.astype(BF16), v2) * (1.0 / l) for p, v2, l in zip(ps, v2s, ls)]
    lses = [m + jnp.log(l) for m, l in zip(ms, ls)]
    return os, lses


def _prompt_attn_kernel(*refs):
    n_g = len(B_GROUPS)
    ins, zs_refs = refs[:5 * n_g], refs[5 * n_g:6 * n_g]
    out_refs, (o_scr, lse_scr) = refs[6 * n_g:7 * n_g], refs[7 * n_g:]
    first_key = jnp.where(pl.program_id(2) == 0, Q_SUB, 0)
    row = lax.broadcasted_iota(jnp.int32, (Q_SUB, 2 * Q_SUB), 0)
    col = lax.broadcasted_iota(jnp.int32, (Q_SUB, 2 * Q_SUB), 1)
    band = (col >= row) & (col <= row + Q_SUB)
    band_first = band & (col >= first_key)

    for g, (_, dil) in enumerate(B_GROUPS):
        q_ref, kc_ref, vc_ref, kp_ref, vp_ref = ins[5 * g:5 * g + 5]
        n_sb = ATTN_TILE // (Q_SUB * dil)

        def rows_of(r, first, n):
            start = r + first * dil
            return pl.ds(start, n) if dil == 1 else pl.ds(start, n, stride=dil)

        work = []
        for r in range(dil):
            cls = rows_of(r, 0, n_sb * Q_SUB)
            kr = jnp.concatenate([kp_ref[0, rows_of(r, 0, Q_SUB), :], kc_ref[0, cls, :]], axis=0).astype(BF16)
            vr = jnp.concatenate([vp_ref[0, rows_of(r, 0, Q_SUB), :], vc_ref[0, cls, :]], axis=0).astype(BF16)
            qr = q_ref[0, cls, :].astype(BF16)
            for sb in range(n_sb):
                work.append((rows_of(r, sb * Q_SUB, Q_SUB), qr[sb * Q_SUB:(sb + 1) * Q_SUB],
                             kr[sb * Q_SUB:(sb + 2) * Q_SUB], vr[sb * Q_SUB:(sb + 2) * Q_SUB],
                             band_first if sb == 0 else band))
        for i in range(0, len(work), ATTN_BATCH):
            part = work[i:i + ATTN_BATCH]
            os, lses = _attend([w[1] for w in part], [w[2] for w in part], [w[3] for w in part],
                               [w[4] for w in part])
            for w, o, lse in zip(part, os, lses):
                o_scr[g, w[0], :] = o
                lse_scr[g, w[0], :] = jnp.broadcast_to(lse, (Q_SUB, B_HD))

    for c in range(ATTN_TILE // MIX_ROWS):
        rows = slice(c * MIX_ROWS, (c + 1) * MIX_ROWS)
        lses = [lse_scr[g, rows, :] for g in range(n_g)]
        m = functools.reduce(jnp.maximum, lses)
        es = [jnp.exp(l - m) for l in lses]
        inv = 1.0 / functools.reduce(lambda a, b: a + b, es)
        for g in range(n_g):
            out_refs[g][0, rows, :] = (o_scr[g, rows, :] * (es[g] * inv) * zs_refs[g][0, rows, :]
                                       ).astype(out_refs[g].dtype)


def _prompt_attn(q, kv, zs, batch, seq):
    assert seq % ATTN_TILE == 0
    n_tiles = seq // ATTN_TILE
    tok = lambda c0: pl.BlockSpec((1, ATTN_TILE, B_HD), lambda b, h, i: (b, i, c0 + h))
    in_specs, args = [], []
    for g, (_, dil) in enumerate(B_GROUPS):
        back = Q_SUB * dil
        assert ATTN_TILE % back == 0
        k0, v0 = 2 * g * B_HG, (2 * g + 1) * B_HG
        prev = lambda c0, back=back: pl.BlockSpec(
            (1, back, B_HD), lambda b, h, i: (b, jnp.maximum(i * (ATTN_TILE // back) - 1, 0), c0 + h))
        in_specs += [tok(g * B_HG), tok(k0), tok(v0), prev(k0), prev(v0)]
        args += [q, kv, kv, kv, kv]
    in_specs += [tok(g * B_HG) for g in range(len(B_GROUPS))]
    args += [zs] * len(B_GROUPS)
    out = pl.BlockSpec((1, ATTN_TILE, B_HD), lambda b, h, i: (b, i, h))
    return pl.pallas_call(
        _prompt_attn_kernel,
        grid=(batch, B_HG, n_tiles),
        in_specs=in_specs,
        out_specs=[out] * len(B_GROUPS),
        out_shape=[jax.ShapeDtypeStruct((batch, seq, B_GW), BF16)] * len(B_GROUPS),
        scratch_shapes=[pltpu.VMEM((len(B_GROUPS), ATTN_TILE, B_HD), F32)] * 2,
        compiler_params=_params(("parallel", "parallel", "arbitrary")),
        name="prompt_attn",
    )(*args)


SAMPLE_NB = 2


def _sample_masks(seq):
    hq = np.arange(B_HG * seq)[:, None] // seq
    iq = np.arange(B_HG * seq)[:, None] % seq
    masks = []
    for window, dil in B_GROUPS:
        n_tok = window if dil < seq else (window // dil) * seq
        col = np.arange(n_tok * KV_ROWS)[None, :]
        tok, is_v, head = col // KV_ROWS, (col // B_HG) % 2, col % B_HG
        if dil < seq:
            ok = (tok >= iq) & ((tok - iq) % dil == 0)
        else:
            ok = (tok % seq) == iq
        masks.append(np.where(ok & (is_v == 0) & (head == hq), 0.0, -np.inf).astype(np.float32))
    col = np.arange(B_HG * seq)[None, :]
    new = []
    for _, dil in B_GROUPS:
        ok = (col // seq == hq) & (col % seq <= iq) & ((iq - col % seq) % dil == 0)
        new.append(np.where(ok, 0.0, -np.inf).astype(np.float32))
    return masks, np.stack(new)


def _sample_attn_kernel(q_ref, kv_ref, zs_ref, c0_ref, c1_ref, c2_ref, m0_ref, m1_ref, m2_ref, mn_ref,
                        out_ref, *, seq, nb):
    scale = B_HD ** -0.5
    caches = (c0_ref, c1_ref, c2_ref)
    masks = (m0_ref, m1_ref, m2_ref)
    groups = range(len(B_GROUPS))

    def heads(ref, e, c0):
        return jnp.concatenate([ref[e, :, c0 + h * B_HD:c0 + (h + 1) * B_HD] for h in range(B_HG)],
                               axis=0).astype(BF16)

    def scores(e):
        qa = [heads(q_ref, e, g * B_GW) for g in groups]
        k_new = [heads(kv_ref, e, 2 * g * B_GW) for g in groups]
        rows = [caches[g][e].reshape(-1, B_HD).astype(BF16) for g in groups]
        s_c = [_dot(qa[g], rows[g], _NT) * scale + masks[g][...] for g in groups]
        s_n = [_dot(qa[g], k_new[g], _NT) * scale + mn_ref[g] for g in groups]
        return rows, s_c, s_n

    def weights(s_c, s_n):
        ms = [jnp.maximum(jnp.max(s_c[g], axis=-1, keepdims=True), jnp.max(s_n[g], axis=-1, keepdims=True))
              for g in groups]
        p_c = [jnp.exp(s_c[g] - ms[g]) for g in groups]
        p_n = [jnp.exp(s_n[g] - ms[g]) for g in groups]
        ls = [jnp.sum(p_c[g], axis=-1, keepdims=True) + jnp.sum(p_n[g], axis=-1, keepdims=True) for g in groups]
        p_v = [pltpu.roll(p_c[g], B_HG, 1).astype(BF16) for g in groups]
        return ms, ls, p_v, [p.astype(BF16) for p in p_n]

    def values(e, rows, p_v, p_n):
        v_new = [heads(kv_ref, e, (2 * g + 1) * B_GW) for g in groups]
        return [_dot(p_v[g], rows[g]) + _dot(p_n[g], v_new[g]) for g in groups]

    scored = [scores(e) for e in range(nb)]
    weighted = [weights(s_c, s_n) for _, s_c, s_n in scored]
    accs = [values(e, scored[e][0], weighted[e][2], weighted[e][3]) for e in range(nb)]
    for e in range(nb):
        ms, ls = weighted[e][0], weighted[e][1]
        outs = [accs[e][g] / ls[g] for g in groups]
        lses = [ms[g] + jnp.log(ls[g]) for g in groups]
        m = functools.reduce(jnp.maximum, lses)
        es = [jnp.exp(l - m) for l in lses]
        inv = 1.0 / (es[0] + es[1] + es[2])
        for g in groups:
            mixed = outs[g] * (es[g] * inv)
            for h in range(B_HG):
                cols = slice(g * B_GW + h * B_HD, g * B_GW + (h + 1) * B_HD)
                out_ref[e, :, cols] = mixed[h * seq:(h + 1) * seq] * zs_ref[e, :, cols]


def _sample_attn(q, kv, zs, caches, batch, seq, nb):
    assert batch % nb == 0
    masks, mask_new = _sample_masks(seq)
    specs, views = [], []
    for (window, dil), c in zip(B_GROUPS, caches):
        assert window == dil * (N_KEYS - 1) and c.shape[2] == window
        if dil >= seq:
            views.append(c.reshape(batch, window // dil, dil * KV_ROWS, B_HD))
            specs.append(pl.BlockSpec((nb, window // dil, seq * KV_ROWS, B_HD), lambda b: (b, 0, 0, 0)))
        else:
            assert seq % dil == 0
            views.append(c.reshape(batch, window * KV_ROWS, B_HD))
            specs.append(pl.BlockSpec((nb, window * KV_ROWS, B_HD), lambda b: (b, 0, 0)))
    const = lambda a: pl.BlockSpec(a.shape, lambda b: (0,) * a.ndim)
    tok = lambda w: pl.BlockSpec((nb, seq, w), lambda b: (b, 0, 0))
    return pl.pallas_call(
        functools.partial(_sample_attn_kernel, seq=seq, nb=nb),
        grid=(batch // nb,),
        in_specs=[tok(B_WIDTH), tok(2 * B_WIDTH), tok(B_WIDTH)] + specs
                 + [const(a) for a in masks] + [const(mask_new)],
        out_specs=tok(B_WIDTH),
        out_shape=jax.ShapeDtypeStruct((batch, seq, B_WIDTH), F32),
        compiler_params=_params(("parallel",)),
        name="sample_attn",
    )(q, kv, zs, *views, *[jnp.asarray(a) for a in masks], jnp.asarray(mask_new))


class _Tiles(NamedTuple):
    l0_in: int
    l1_in: int
    out: int
    scan: int
    chunk: int


LONG_SEQ = 512


def _tiles(seq):
    if seq >= LONG_SEQ:
        return _Tiles(l0_in=512, l1_in=256, out=1024, scan=512, chunk=64)
    return _Tiles(l0_in=256, l1_in=256, out=256, scan=seq, chunk=seq)


def _trunk(x, ada, start, state, caches, weights):
    (norm_w, alb, a_w_in, a_b_f, a_gw, a_w_out, b_w_in, b_qn, b_kn, b_w_out) = weights
    batch, seq, _ = x.shape
    tiles = _tiles(seq)
    flat = seq >= LONG_SEQ
    act = BF16 if flat else F32
    tok = (lambda a: a.reshape(batch * seq, a.shape[-1])) if flat else (lambda a: a)
    rows = lambda a: a.reshape(batch * seq, a.shape[-1])
    ada0 = ada[0].reshape(batch, 1, 3 * D_MODEL)
    ada1 = ada[1].reshape(batch, 1, 3 * D_MODEL)

    x0 = tok(x)
    q, k, lf, v, zs = _l0_inproj(x0, ada0, norm_w[0:1], a_w_in, alb, a_b_f, batch, seq, tiles.l0_in, act)
    og, s_new = _hgrn_scan(rows(q), rows(k), rows(lf), rows(v), rows(zs), a_gw, state,
                           batch, seq, tiles.scan, tiles.chunk, act, n_seq=0 if state is None else SCAN_SEQS)
    og = og if flat else og.reshape(batch, seq, A_V)
    x1 = _outproj([og], a_w_out, x0, ada0, batch, seq, tiles.out)

    cos, sin = _rope_tables(start, seq)
    qr, kv, zs1, *new_rows = _l1_inproj(x1, ada1, norm_w[1:2], b_w_in, b_qn, b_kn, cos, sin, batch, seq,
                                        tiles.l1_in, act)
    kv3 = kv.reshape(batch, seq, 2 * B_WIDTH)
    if caches is None:
        og1 = _prompt_attn(qr.reshape(batch, seq, B_WIDTH), kv3, zs1.reshape(batch, seq, B_WIDTH), batch, seq)
        og1 = [o.reshape(batch * seq, B_GW) for o in og1]
    else:
        og1 = [_sample_attn(qr, kv, zs1, caches, batch, seq, SAMPLE_NB)]
    x2 = _outproj(og1, b_w_out, x1, ada1, batch, seq, tiles.out)

    kv_rows = [r.reshape(batch, min(window, seq), 2, B_HG, B_HD) for r, (window, _) in zip(new_rows, B_GROUPS)]
    return x2.reshape(batch, seq, D_MODEL), s_new, kv_rows


def kernel(x_prompt, x_sample, state_hgrn, cache_kv_w128, cache_kv_w512, cache_kv_w2048, c_prompt, c_sample,
           norm_w, ada_w, ada_b, a_lower_bounds, a_w_in, a_b_f, a_g_norm_w, a_w_out, b_w_in, b_q_norm_w,
           b_k_norm_w, b_w_out):
    bp, lp, _ = x_prompt.shape
    bs, ls, _ = x_sample.shape

    pad = (-(bp + bs)) % 8
    c_all = jnp.concatenate([c_prompt, c_sample, jnp.zeros((pad, D_MODEL), F32)], axis=0)
    ada = _ada_vectors(c_all, ada_w, ada_b)
    ada_p, ada_s = ada[:, :bp], ada[:, bp:bp + bs]

    weights = (norm_w, a_lower_bounds, a_w_in[0], a_b_f, a_g_norm_w, a_w_out[0],
               b_w_in[0], b_q_norm_w, b_k_norm_w, b_w_out[0])

    y_p, s_p, kv_p = _trunk(x_prompt, ada_p, 0, None, None, weights)
    caches = (cache_kv_w128, cache_kv_w512, cache_kv_w2048)
    y_s, s_s, kv_s = _trunk(x_sample, ada_s, PAST_LEN, state_hgrn[0], caches, weights)

    kv_out = []
    for g in range(len(B_GROUPS)):
        kv_out.append(kv_p[g][None])
        kv_out.append(kv_s[g][None])
    return (y_p, y_s, s_p[None], s_s[None], *kv_out)
```

```python
import functools
import math
from typing import NamedTuple

import numpy as np
import jax
import jax.numpy as jnp
from jax import lax
from jax.experimental import pallas as pl
from jax.experimental.pallas import tpu as pltpu

F32 = jnp.float32
BF16 = jnp.bfloat16

D_MODEL = 1024
EPS = 1e-6
A_HEADS = 8
A_KDIM = 128
A_VDIM = 256
A_QK = A_HEADS * A_KDIM
A_V = A_HEADS * A_VDIM
B_GROUPS = ((128, 1), (512, 4), (2048, 16))
B_HG = 4
B_HD = 128
B_GW = B_HG * B_HD
B_WIDTH = len(B_GROUPS) * B_GW
ROPE_THETA = 10000.0
PAST_LEN = 2048
N_KEYS = 129

COL_TILE = 512
ADA_TILE = 1024
VMEM_LIMIT = 56 * 1024 * 1024

_NT = (((1,), (1,)), ((), ()))
_TN = (((0,), (0,)), ((), ()))


def _sigmoid(x):
    return 1.0 / (1.0 + jnp.exp(-x))


def _silu(x):
    h = 0.5 * x
    return h * jnp.tanh(h) + h


def _dot(a, b, dims=None):
    if dims is None:
        return jnp.dot(a, b, preferred_element_type=F32)
    return lax.dot_general(a, b, dims, preferred_element_type=F32)


def _params(sem):
    return pltpu.CompilerParams(dimension_semantics=sem, vmem_limit_bytes=VMEM_LIMIT)


def _ada_kernel(c_ref, w_ref, b_ref, o_ref):
    a = _silu(c_ref[...]).astype(BF16)
    o_ref[0] = _dot(a, w_ref[0].astype(BF16)) + b_ref[0]


def _ada_vectors(c_all, ada_w, ada_b):
    n_layers, _, width = ada_w.shape
    rows = c_all.shape[0]
    return pl.pallas_call(
        _ada_kernel,
        grid=(n_layers, width // ADA_TILE),
        in_specs=[
            pl.BlockSpec((rows, D_MODEL), lambda l, j: (0, 0)),
            pl.BlockSpec((1, D_MODEL, ADA_TILE), lambda l, j: (l, 0, j)),
            pl.BlockSpec((1, 1, ADA_TILE), lambda l, j: (l, 0, j)),
        ],
        out_specs=pl.BlockSpec((1, rows, ADA_TILE), lambda l, j: (l, 0, j)),
        out_shape=jax.ShapeDtypeStruct((n_layers, rows, width), F32),
        compiler_params=_params(("parallel", "parallel")),
        name="ada_vectors",
    )(c_all, ada_w, ada_b.reshape(n_layers, 1, width))


def _modulated_norm(x_ref, ada_ref, nw_ref):
    x = x_ref[...]
    ada = ada_ref[...]
    if x.ndim == 2:
        ada = ada.reshape(1, ada.shape[-1])
    shift = ada[..., :D_MODEL]
    scale = ada[..., D_MODEL:2 * D_MODEL]
    ms = jnp.mean(x * x, axis=-1, keepdims=True)
    y = x * lax.rsqrt(ms + EPS) * nw_ref[...].reshape((1,) * (x.ndim - 1) + (D_MODEL,))
    h = y * (1.0 + scale) + shift
    return h.reshape(-1, D_MODEL).astype(BF16)


def _token_specs(batch, seq, tm):
    if seq >= tm:
        assert seq % tm == 0
        per = seq // tm
        n_tiles = batch * per
        tok_block = lambda w: pl.BlockSpec((tm, w), lambda i: (i, 0))
        ada_block = pl.BlockSpec((1, 1, 3 * D_MODEL), lambda i: (i // per, 0, 0))
        shape = lambda w: (batch * seq, w)
        return n_tiles, tok_block, ada_block, shape, per
    tb = tm // seq
    assert batch % tb == 0
    n_tiles = batch // tb
    tok_block = lambda w: pl.BlockSpec((tb, seq, w), lambda i: (i, 0, 0))
    ada_block = pl.BlockSpec((tb, 1, 3 * D_MODEL), lambda i: (i, 0, 0))
    shape = lambda w: (batch, seq, w)
    return n_tiles, tok_block, ada_block, shape, None


def _resident(shape):
    return pl.BlockSpec(shape, lambda i: (0,) * len(shape), pipeline_mode=pl.Buffered(1))


def _put(ref, c0, val):
    ref[(Ellipsis, slice(c0, c0 + val.shape[-1]))] = val.reshape(ref.shape[:-1] + val.shape[-1:]).astype(ref.dtype)


def _l0_inproj_kernel(x_ref, ada_ref, nw_ref, w_ref, alb_ref, bf_ref, q_ref, k_ref, lf_ref, v_ref, zs_ref):
    h = _modulated_norm(x_ref, ada_ref, nw_ref)
    a = alb_ref[...]
    m = jnp.max(a, axis=0, keepdims=True)
    e = jnp.exp(a - m)
    lb = e[0:1] / jnp.sum(e, axis=0, keepdims=True)
    for c0 in range(0, w_ref.shape[1], COL_TILE):
        acc = _dot(h, w_ref[:, c0:c0 + COL_TILE].astype(BF16))
        if c0 < A_QK:
            _put(q_ref, c0, _silu(acc))
        elif c0 < 2 * A_QK:
            c = c0 - A_QK
            f = lb[:, c:c + COL_TILE] + (1.0 - lb[:, c:c + COL_TILE]) * _sigmoid(acc + bf_ref[:, c:c + COL_TILE])
            _put(lf_ref, c, jnp.log(f))
            _put(k_ref, c, 1.0 - f)
        elif c0 < 2 * A_QK + A_V:
            _put(v_ref, c0 - 2 * A_QK, acc)
        else:
            _put(zs_ref, c0 - 2 * A_QK - A_V, _silu(acc))


def _l0_inproj(x, ada, nw, w_bf, alb, b_f, batch, seq, tm, act_dtype):
    n_tiles, tok_block, ada_block, shape, _ = _token_specs(batch, seq, tm)
    return pl.pallas_call(
        _l0_inproj_kernel,
        grid=(n_tiles,),
        in_specs=[tok_block(D_MODEL), ada_block, _resident((1, D_MODEL)), _resident(w_bf.shape),
                  _resident(alb.shape), _resident(b_f.shape)],
        out_specs=[tok_block(A_QK), tok_block(A_QK), tok_block(A_QK), tok_block(A_V), tok_block(A_V)],
        out_shape=[
            jax.ShapeDtypeStruct(shape(A_QK), F32),
            jax.ShapeDtypeStruct(shape(A_QK), F32),
            jax.ShapeDtypeStruct(shape(A_QK), F32),
            jax.ShapeDtypeStruct(shape(A_V), act_dtype),
            jax.ShapeDtypeStruct(shape(A_V), act_dtype),
        ],
        compiler_params=_params(("parallel",)),
        name="l0_inproj",
    )(x, ada, nw, w_bf, alb, b_f)


def _scan_levels(chunk):
    return [1 << l for l in range(int(math.log2(chunk)))]


def _exponent_matrix(chunk):
    r = np.arange(chunk)[:, None]
    c = np.arange(chunk)[None, :]
    blocks = [c <= r, c > r]
    for s in _scan_levels(chunk):
        bound = (r & ~(2 * s - 1)) | s
        upper = (r & s) != 0
        blocks.append(np.where(upper, (c > bound) & (c <= r), (c > r) & (c <= bound)))
    return np.concatenate(blocks, axis=0).astype(np.float32)


def _hgrn_chunk(q, k, lf, v, state, emat, chunk):
    dk = q.shape[-1]
    ex = jnp.exp(jnp.dot(emat, lf, preferred_element_type=F32, precision=lax.Precision.HIGHEST))
    e_b = ex[0:chunk]
    e_u = ex[chunk:2 * chunk]
    row = lax.broadcasted_iota(jnp.int32, (chunk, 1), 0)
    col = lax.broadcasted_iota(jnp.int32, (1, chunk), 1)

    att = jnp.where(row == col, jnp.sum(q * k, axis=-1, keepdims=True), 0.0)
    for l, s in enumerate(_scan_levels(chunk)):
        g = ex[(2 + l) * chunk:(3 + l) * chunk]
        upper = (row & s) != 0
        xb = (jnp.where(upper, q, k) * g).astype(BF16)
        part = _dot(xb, xb, _NT)
        keep = upper & ((col & s) == 0) & ((row >> (l + 1)) == (col >> (l + 1)))
        att = att + jnp.where(keep, part, 0.0)

    o = _dot(att.astype(BF16), v.astype(BF16)) + _dot((q * e_b).astype(BF16), state.astype(BF16))

    e_last = e_b[chunk - 1:chunk]
    eye = lax.broadcasted_iota(jnp.int32, (dk, dk), 0) == lax.broadcasted_iota(jnp.int32, (dk, dk), 1)
    e_col = jnp.sum(jnp.where(eye, e_last, 0.0), axis=-1, keepdims=True)
    new_state = state * e_col + _dot((k * e_u).astype(BF16), v.astype(BF16), _TN)
    return o, new_state


SAFE_DECAY = 80.0
SPLIT_PAD = 32
SCAN_LINK = 2
SCAN_SEQS = 8


def _split_prefix_matrix(chunk, n_seq):
    t = np.kron(np.eye(n_seq, dtype=np.float32), _exponent_matrix(chunk)[:chunk])
    pad = (-3 * t.shape[1]) % SPLIT_PAD
    return np.concatenate([t, t, t, np.zeros((t.shape[0], pad), np.float32)], axis=1)


def _hgrn_chunk_bounded(q, k, lf, v, states, tmat, chunk, n_seq, linked):
    n_rows = n_seq * chunk
    hi = lf.astype(BF16)
    r1 = lf - hi.astype(F32)
    mid = r1.astype(BF16)
    lo = r1 - mid.astype(F32)
    parts = [hi.astype(F32), mid.astype(F32), lo]
    pad = tmat.shape[1] - 3 * n_rows
    if pad:
        parts.append(jnp.zeros((pad, lf.shape[1]), F32))
    b = _dot(tmat, jnp.concatenate(parts, axis=0).astype(BF16))
    e_b = jnp.exp(b)
    qb = q * e_b
    kn_f = k * jnp.exp(-b)
    kn = kn_f.astype(BF16)
    row = lax.broadcasted_iota(jnp.int32, (n_rows, n_rows), 0)
    col = lax.broadcasted_iota(jnp.int32, (n_rows, n_rows), 1)
    causal = row >= col
    if n_seq > 1:
        shift = chunk.bit_length() - 1
        causal = causal & ((row >> shift) == (col >> shift))
    eye = lax.broadcasted_iota(jnp.int32, (A_KDIM, A_KDIM), 0) == lax.broadcasted_iota(jnp.int32, (A_KDIM, A_KDIM), 1)
    ks = [slice(h * A_KDIM, (h + 1) * A_KDIM) for h in range(A_HEADS)]
    vs = [slice(h * A_VDIM, (h + 1) * A_VDIM) for h in range(A_HEADS)]
    sq = [slice(s * chunk, (s + 1) * chunk) for s in range(n_seq)]
    qbb, vb = qb.astype(BF16), v.astype(BF16)
    qb_s = [qbb] if n_seq == 1 else [qb[r].astype(BF16) for r in sq]
    ku_s = [(kn_f[r] * e_b[r.stop - 1:r.stop]).astype(BF16) for r in sq]
    v_s = [vb] if n_seq == 1 else [v[r].astype(BF16) for r in sq]
    atts = [jnp.where(causal, _dot(qbb[:, ks[h]], kn[:, ks[h]], _NT), 0.0).astype(BF16) for h in range(A_HEADS)]
    e_cols = [[jnp.sum(jnp.where(eye, e_b[(s + 1) * chunk - 1:(s + 1) * chunk, ks[h]], 0.0), axis=-1, keepdims=True)
               for h in range(A_HEADS)] for s in range(n_seq)]
    new_states = [[None] * A_HEADS for _ in range(n_seq)]
    carried = [[None] * A_HEADS for _ in range(n_seq)]
    for s in range(n_seq):
        for h in range(A_HEADS):
            state = new_states[s - 1][h] if (linked and s > 0) else states[0 if linked else s][h]
            carried[s][h] = _dot(qb_s[s][:, ks[h]], state.astype(BF16))
            new_states[s][h] = state * e_cols[s][h] + _dot(ku_s[s][:, ks[h]], v_s[s][:, vs[h]], _TN)
    outs = []
    for h in range(A_HEADS):
        inter = carried[0][h] if n_seq == 1 else jnp.concatenate([carried[s][h] for s in range(n_seq)], axis=0)
        outs.append(_dot(atts[h], vb[:, vs[h]]) + inter)
    return outs, new_states


def _hgrn_scan_kernel(*refs, chunk, n_chunks, has_state, n_seq, n_link):
    one_shot = n_seq > 0
    if has_state:
        q_ref, k_ref, lf_ref, v_ref, zs_ref, gw_ref, e_ref, t_ref, s0_ref, og_ref, so_ref, s_scr = refs
    else:
        q_ref, k_ref, lf_ref, v_ref, zs_ref, gw_ref, e_ref, t_ref, og_ref, so_ref, s_scr = refs
    t = pl.program_id(1)

    if not one_shot:
        @pl.when(t == 0)
        def _():
            if has_state:
                s_scr[...] = s0_ref[0]
            else:
                s_scr[...] = jnp.zeros_like(s_scr)

    def load_state(s, h):
        return s0_ref[s, h] if one_shot else s_scr[h]

    def store_state(s, h, val):
        if one_shot:
            so_ref[s, h] = val
        else:
            s_scr[h] = val

    gw = gw_ref[...]

    def finish(rows, h, o):
        vc = slice(h * A_VDIM, (h + 1) * A_VDIM)
        ms = jnp.mean(o * o, axis=-1, keepdims=True)
        og_ref[rows, vc] = (o * lax.rsqrt(ms + EPS) * gw * zs_ref[rows, vc]).astype(og_ref.dtype)

    seqs = range(max(n_seq, 1))

    def chunk_rows(c, s=None, n=1):
        if n_chunks > 1:
            return pl.ds(pl.multiple_of(c * (n * chunk), n * chunk), n * chunk)
        return slice(0, len(seqs) * chunk) if s is None else slice(s * chunk, (s + 1) * chunk)

    def bounded_body(c, carry):
        rows = chunk_rows(c, n=n_link)
        states = [[load_state(s, h) for h in range(A_HEADS)] for s in seqs]
        n_at_once = len(seqs) if one_shot else n_link
        outs, new_states = _hgrn_chunk_bounded(q_ref[rows, :], k_ref[rows, :], lf_ref[rows, :], v_ref[rows, :],
                                               states, t_ref[...], chunk, n_at_once, linked=not one_shot)
        for h in range(A_HEADS):
            if one_shot:
                for s in seqs:
                    store_state(s, h, new_states[s][h])
            else:
                store_state(0, h, new_states[-1][h])
            finish(rows, h, outs[h])
        return carry

    def general_body(c, carry):
        for s in seqs:
            rows = chunk_rows(c, s)
            for h in range(A_HEADS):
                kc = slice(h * A_KDIM, (h + 1) * A_KDIM)
                vc = slice(h * A_VDIM, (h + 1) * A_VDIM)
                o, s_new = _hgrn_chunk(q_ref[rows, kc], k_ref[rows, kc], lf_ref[rows, kc],
                                       v_ref[rows, vc], load_state(s, h), e_ref[...], chunk)
                store_state(s, h, s_new)
                finish(rows, h, o)
        return carry

    def run(body, trips):
        def go():
            if trips == 1:
                body(0, 0)
            else:
                lax.fori_loop(0, trips, body, 0)
        return go

    bounded = jnp.min(lf_ref[...]) * chunk >= -SAFE_DECAY
    lax.cond(bounded, run(bounded_body, n_chunks // n_link), run(general_body, n_chunks))

    if not one_shot:
        @pl.when(t == pl.num_programs(1) - 1)
        def _():
            so_ref[0] = s_scr[...]


def _hgrn_scan(q, k, lf, v, zs, gw, s0, batch, seq, tile, chunk, out_dtype, n_seq=0):
    n_t = seq // tile
    per_step = max(n_seq, 1)
    assert batch % per_step == 0 and (n_seq == 0 or (s0 is not None and seq == tile == chunk))
    n_link = SCAN_LINK if (n_seq == 0 and (tile // chunk) % SCAN_LINK == 0) else 1
    emat = jnp.asarray(_exponent_matrix(chunk))
    tmat = jnp.asarray(_split_prefix_matrix(chunk, max(per_step, n_link)), dtype=BF16)
    row_block = lambda w: pl.BlockSpec((tile * per_step, w), lambda b, t: (b * n_t + t, 0))
    state_block = pl.BlockSpec((per_step, A_HEADS, A_KDIM, A_VDIM), lambda b, t: (b, 0, 0, 0))
    in_specs = [row_block(A_QK), row_block(A_QK), row_block(A_QK), row_block(A_V), row_block(A_V),
                pl.BlockSpec((1, A_VDIM), lambda b, t: (0, 0)),
                pl.BlockSpec(emat.shape, lambda b, t: (0, 0)),
                pl.BlockSpec(tmat.shape, lambda b, t: (0, 0))]
    args = [q, k, lf, v, zs, gw, emat, tmat]
    if s0 is not None:
        in_specs.append(state_block)
        args.append(s0)
    kern = functools.partial(_hgrn_scan_kernel, chunk=chunk, n_chunks=tile // chunk,
                             has_state=s0 is not None, n_seq=n_seq, n_link=n_link)
    return pl.pallas_call(
        kern,
        grid=(batch // per_step, n_t),
        in_specs=in_specs,
        out_specs=[row_block(A_V), state_block],
        out_shape=[jax.ShapeDtypeStruct((batch * seq, A_V), out_dtype),
                   jax.ShapeDtypeStruct((batch, A_HEADS, A_KDIM, A_VDIM), F32)],
        scratch_shapes=[pltpu.VMEM((A_HEADS, A_KDIM, A_VDIM), F32)],
        compiler_params=_params(("parallel", "arbitrary")),
        name="hgrn_scan",
    )(*args)


def _outproj_kernel(*refs):
    *g_refs, w_ref, x_ref, ada_ref, o_ref = refs
    gs = [g_ref[...] for g_ref in g_refs]
    gs = [g.reshape(-1, g.shape[-1]).astype(BF16) for g in gs]
    ada = ada_ref[...]
    if len(x_ref.shape) == 2:
        ada = ada.reshape(1, ada.shape[-1])
    for c0 in range(0, D_MODEL, COL_TILE):
        cols = (Ellipsis, slice(c0, c0 + COL_TILE))
        y, k0 = None, 0
        for g in gs:
            part = _dot(g, w_ref[k0:k0 + g.shape[1], c0:c0 + COL_TILE].astype(BF16))
            y = part if y is None else y + part
            k0 += g.shape[1]
        gate = ada[..., 2 * D_MODEL + c0:2 * D_MODEL + c0 + COL_TILE]
        o_ref[cols] = x_ref[cols] + gate * y.reshape(x_ref.shape[:-1] + (COL_TILE,))


def _outproj(gs, w_bf, x, ada, batch, seq, tm):
    n_tiles, tok_block, ada_block, shape, _ = _token_specs(batch, seq, tm)
    assert sum(g.shape[-1] for g in gs) == w_bf.shape[0]
    return pl.pallas_call(
        _outproj_kernel,
        grid=(n_tiles,),
        in_specs=[tok_block(g.shape[-1]) for g in gs] + [_resident(w_bf.shape), tok_block(D_MODEL), ada_block],
        out_specs=tok_block(D_MODEL),
        out_shape=jax.ShapeDtypeStruct(shape(D_MODEL), F32),
        compiler_params=_params(("parallel",)),
        name="outproj",
    )(*gs, w_bf, x, ada)


def _rope_table_kernel(pos_ref, inv_ref, cos_ref, sin_ref):
    ang = pos_ref[...] * inv_ref[...]
    lane = lax.broadcasted_iota(jnp.int32, ang.shape, 1)
    cos_ref[...] = jnp.cos(ang)
    sin_ref[...] = jnp.where(lane < B_HD // 2, -jnp.sin(ang), jnp.sin(ang))


def _rope_tables(start, seq):
    half = B_HD // 2
    inv = ROPE_THETA ** (-jnp.arange(half, dtype=F32) / half)
    inv2 = jnp.concatenate([inv, inv]).reshape(1, B_HD)
    pos = (start + jnp.arange(seq, dtype=jnp.int32)).astype(F32).reshape(seq, 1)
    return pl.pallas_call(
        _rope_table_kernel,
        out_shape=[jax.ShapeDtypeStruct((seq, B_HD), F32)] * 2,
        name="rope_tables",
    )(pos, inv2)


KV_ROWS = 2 * B_HG


def _l1_inproj_kernel(x_ref, ada_ref, nw_ref, w_ref, qn_ref, kn_ref, cos_ref, sin_ref, q_ref, kv_ref, zs_ref,
                      *row_refs, row_tokens, row_first, per):
    h = _modulated_norm(x_ref, ada_ref, nw_ref)
    tm = h.shape[0]
    cos = cos_ref[...]
    sin = sin_ref[...]
    reps = tm // cos.shape[0]
    if reps > 1:
        cos = jnp.broadcast_to(cos[None], (reps,) + cos.shape).reshape(tm, B_HD)
        sin = jnp.broadcast_to(sin[None], (reps,) + sin.shape).reshape(tm, B_HD)

    def norm_rope(acc, nw):
        outs = []
        for hd in range(acc.shape[1] // B_HD):
            xh = acc[:, hd * B_HD:(hd + 1) * B_HD]
            ms = jnp.mean(xh * xh, axis=-1, keepdims=True)
            y = xh * lax.rsqrt(ms + EPS) * nw
            outs.append(y * cos + pltpu.roll(y, B_HD // 2, 1) * sin)
        return jnp.concatenate(outs, axis=-1)

    for c0 in range(0, w_ref.shape[1], B_GW):
        acc = _dot(h, w_ref[:, c0:c0 + B_GW].astype(BF16))
        seg, g = divmod(c0 // B_GW, len(B_GROUPS))
        if seg == 0:
            _put(q_ref, g * B_GW, norm_rope(acc, qn_ref[...]))
        elif seg == 1:
            _put(kv_ref, 2 * g * B_GW, norm_rope(acc, kn_ref[...]))
        elif seg == 2:
            _put(kv_ref, (2 * g + 1) * B_GW, acc)
        else:
            _put(zs_ref, g * B_GW, _silu(acc))

    for g, rows_ref in enumerate(row_refs):
        @pl.when(pl.program_id(0) % per >= row_first[g])
        def _(g=g, rows_ref=rows_ref):
            n = row_tokens[g]
            for j in range(KV_ROWS):
                cols = (Ellipsis, slice(2 * g * B_GW + j * B_HD, 2 * g * B_GW + (j + 1) * B_HD))
                rows_ref[pl.ds(j, n, stride=KV_ROWS), :] = kv_ref[cols].reshape(tm, B_HD)[tm - n:]


def _l1_inproj(x, ada, nw, w_bf, qn, kn, cos, sin, batch, seq, tm, act_dtype):
    n_tiles, tok_block, ada_block, shape, per = _token_specs(batch, seq, tm)
    out_specs = [tok_block(B_WIDTH), tok_block(2 * B_WIDTH), tok_block(B_WIDTH)]
    out_shape = [jax.ShapeDtypeStruct(shape(B_WIDTH), F32),
                 jax.ShapeDtypeStruct(shape(2 * B_WIDTH), F32),
                 jax.ShapeDtypeStruct(shape(B_WIDTH), act_dtype)]
    row_tokens, row_first = (), ()
    if per is None:
        tab_block = _resident((seq, B_HD))
        per = 1
        for _ in B_GROUPS:
            out_specs.append(pl.BlockSpec((tm * KV_ROWS, B_HD), lambda i: (i, 0)))
            out_shape.append(jax.ShapeDtypeStruct((batch * seq * KV_ROWS, B_HD), F32))
            row_tokens += (tm,)
            row_first += (0,)
    else:
        tab_block = pl.BlockSpec((tm, B_HD), lambda i: (i % per, 0))
        for window, _ in B_GROUPS:
            keep = min(window, seq)
            blk = min(keep, tm)
            n_blk = keep // blk
            first = per - n_blk
            out_specs.append(pl.BlockSpec(
                (blk * KV_ROWS, B_HD),
                lambda i, n_blk=n_blk, first=first: ((i // per) * n_blk + jnp.clip(i % per - first, 0, n_blk - 1), 0)))
            out_shape.append(jax.ShapeDtypeStruct((batch * keep * KV_ROWS, B_HD), F32))
            row_tokens += (blk,)
            row_first += (first,)
    return pl.pallas_call(
        functools.partial(_l1_inproj_kernel, row_tokens=row_tokens, row_first=row_first, per=per),
        grid=(n_tiles,),
        in_specs=[tok_block(D_MODEL), ada_block, _resident((1, D_MODEL)), _resident(w_bf.shape),
                  _resident((1, B_HD)), _resident((1, B_HD)), tab_block, tab_block],
        out_specs=out_specs,
        out_shape=out_shape,
        compiler_params=_params(("arbitrary",)),
        name="l1_inproj",
    )(x, ada, nw, w_bf, qn, kn, cos, sin)


Q_SUB = 128
ATTN_TILE = 2048
ATTN_BATCH = 8
MIX_ROWS = 256


def _attend(qs, k2s, v2s, valids):
    scale = B_HD ** -0.5
    ss = [jnp.where(ok, _dot(q, k2, _NT) * scale, -jnp.inf) for q, k2, ok in zip(qs, k2s, valids)]
    ms = [jnp.max(s, axis=-1, keepdims=True) for s in ss]
    ps = [jnp.exp(s - m) for s, m in zip(ss, ms)]
    ls = [jnp.sum(p, axis=-1, keepdims=True) for p in ps]
    os = [_dot(p.astype(BF16), v2) * (1.0 / l) for p, v2, l in zip(ps, v2s, ls)]
    lses = [m + jnp.log(l) for m, l in zip(ms, ls)]
    return os, lses


def _prompt_attn_kernel(*refs):
    n_g = len(B_GROUPS)
    ins, zs_refs = refs[:5 * n_g], refs[5 * n_g:6 * n_g]
    out_refs, (o_scr, lse_scr) = refs[6 * n_g:7 * n_g], refs[7 * n_g:]
    first_key = jnp.where(pl.program_id(2) == 0, Q_SUB, 0)
    row = lax.broadcasted_iota(jnp.int32, (Q_SUB, 2 * Q_SUB), 0)
    col = lax.broadcasted_iota(jnp.int32, (Q_SUB, 2 * Q_SUB), 1)
    band = (col >= row) & (col <= row + Q_SUB)
    band_first = band & (col >= first_key)

    for g, (_, dil) in enumerate(B_GROUPS):
        q_ref, kc_ref, vc_ref, kp_ref, vp_ref = ins[5 * g:5 * g + 5]
        n_sb = ATTN_TILE // (Q_SUB * dil)

        def rows_of(r, first, n):
            start = r + first * dil
            return pl.ds(start, n) if dil == 1 else pl.ds(start, n, stride=dil)

        work = []
        for r in range(dil):
            cls = rows_of(r, 0, n_sb * Q_SUB)
            kr = jnp.concatenate([kp_ref[0, rows_of(r, 0, Q_SUB), :], kc_ref[0, cls, :]], axis=0).astype(BF16)
            vr = jnp.concatenate([vp_ref[0, rows_of(r, 0, Q_SUB), :], vc_ref[0, cls, :]], axis=0).astype(BF16)
            qr = q_ref[0, cls, :].astype(BF16)
            for sb in range(n_sb):
                work.append((rows_of(r, sb * Q_SUB, Q_SUB), qr[sb * Q_SUB:(sb + 1) * Q_SUB],
                             kr[sb * Q_SUB:(sb + 2) * Q_SUB], vr[sb * Q_SUB:(sb + 2) * Q_SUB],
                             band_first if sb == 0 else band))
        for i in range(0, len(work), ATTN_BATCH):
            part = work[i:i + ATTN_BATCH]
            os, lses = _attend([w[1] for w in part], [w[2] for w in part], [w[3] for w in part],
                               [w[4] for w in part])
            for w, o, lse in zip(part, os, lses):
                o_scr[g, w[0], :] = o
                lse_scr[g, w[0], :] = jnp.broadcast_to(lse, (Q_SUB, B_HD))

    for c in range(ATTN_TILE // MIX_ROWS):
        rows = slice(c * MIX_ROWS, (c + 1) * MIX_ROWS)
        lses = [lse_scr[g, rows, :] for g in range(n_g)]
        m = functools.reduce(jnp.maximum, lses)
        es = [jnp.exp(l - m) for l in lses]
        inv = 1.0 / functools.reduce(lambda a, b: a + b, es)
        for g in range(n_g):
            out_refs[g][0, rows, :] = (o_scr[g, rows, :] * (es[g] * inv) * zs_refs[g][0, rows, :]
                                       ).astype(out_refs[g].dtype)


def _prompt_attn(q, kv, zs, batch, seq):
    assert seq % ATTN_TILE == 0
    n_tiles = seq // ATTN_TILE
    tok = lambda c0: pl.BlockSpec((1, ATTN_TILE, B_HD), lambda b, h, i: (b, i, c0 + h))
    in_specs, args = [], []
    for g, (_, dil) in enumerate(B_GROUPS):
        back = Q_SUB * dil
        assert ATTN_TILE % back == 0
        k0, v0 = 2 * g * B_HG, (2 * g + 1) * B_HG
        prev = lambda c0, back=back: pl.BlockSpec(
            (1, back, B_HD), lambda b, h, i: (b, jnp.maximum(i * (ATTN_TILE // back) - 1, 0), c0 + h))
        in_specs += [tok(g * B_HG), tok(k0), tok(v0), prev(k0), prev(v0)]
        args += [q, kv, kv, kv, kv]
    in_specs += [tok(g * B_HG) for g in range(len(B_GROUPS))]
    args += [zs] * len(B_GROUPS)
    out = pl.BlockSpec((1, ATTN_TILE, B_HD), lambda b, h, i: (b, i, h))
    return pl.pallas_call(
        _prompt_attn_kernel,
        grid=(batch, B_HG, n_tiles),
        in_specs=in_specs,
        out_specs=[out] * len(B_GROUPS),
        out_shape=[jax.ShapeDtypeStruct((batch, seq, B_GW), BF16)] * len(B_GROUPS),
        scratch_shapes=[pltpu.VMEM((len(B_GROUPS), ATTN_TILE, B_HD), F32)] * 2,
        compiler_params=_params(("parallel", "parallel", "arbitrary")),
        name="prompt_attn",
    )(*args)


SAMPLE_NB = 2


def _sample_masks(seq):
    hq = np.arange(B_HG * seq)[:, None] // seq
    iq = np.arange(B_HG * seq)[:, None] % seq
    masks = []
    for window, dil in B_GROUPS:
        n_tok = window if dil < seq else (window // dil) * seq
        col = np.arange(n_tok * KV_ROWS)[None, :]
        tok, is_v, head = col // KV_ROWS, (col // B_HG) % 2, col % B_HG
        if dil < seq:
            ok = (tok >= iq) & ((tok - iq) % dil == 0)
        else:
            ok = (tok % seq) == iq
        masks.append(np.where(ok & (is_v == 0) & (head == hq), 0.0, -np.inf).astype(np.float32))
    col = np.arange(B_HG * seq)[None, :]
    new = []
    for _, dil in B_GROUPS:
        ok = (col // seq == hq) & (col % seq <= iq) & ((iq - col % seq) % dil == 0)
        new.append(np.where(ok, 0.0, -np.inf).astype(np.float32))
    return masks, np.stack(new)


def _sample_attn_kernel(q_ref, kv_ref, zs_ref, c0_ref, c1_ref, c2_ref, m0_ref, m1_ref, m2_ref, mn_ref,
                        out_ref, *, seq, nb):
    scale = B_HD ** -0.5
    caches = (c0_ref, c1_ref, c2_ref)
    masks = (m0_ref, m1_ref, m2_ref)
    groups = range(len(B_GROUPS))

    def heads(ref, e, c0):
        return jnp.concatenate([ref[e, :, c0 + h * B_HD:c0 + (h + 1) * B_HD] for h in range(B_HG)],
                               axis=0).astype(BF16)

    def scores(e):
        qa = [heads(q_ref, e, g * B_GW) for g in groups]
        k_new = [heads(kv_ref, e, 2 * g * B_GW) for g in groups]
        rows = [caches[g][e].reshape(-1, B_HD).astype(BF16) for g in groups]
        s_c = [_dot(qa[g], rows[g], _NT) * scale + masks[g][...] for g in groups]
        s_n = [_dot(qa[g], k_new[g], _NT) * scale + mn_ref[g] for g in groups]
        return rows, s_c, s_n

    def weights(s_c, s_n):
        ms = [jnp.maximum(jnp.max(s_c[g], axis=-1, keepdims=True), jnp.max(s_n[g], axis=-1, keepdims=True))
              for g in groups]
        p_c = [jnp.exp(s_c[g] - ms[g]) for g in groups]
        p_n = [jnp.exp(s_n[g] - ms[g]) for g in groups]
        ls = [jnp.sum(p_c[g], axis=-1, keepdims=True) + jnp.sum(p_n[g], axis=-1, keepdims=True) for g in groups]
        p_v = [pltpu.roll(p_c[g], B_HG, 1).astype(BF16) for g in groups]
        return ms, ls, p_v, [p.astype(BF16) for p in p_n]

    def values(e, rows, p_v, p_n):
        v_new = [heads(kv_ref, e, (2 * g + 1) * B_GW) for g in groups]
        return [_dot(p_v[g], rows[g]) + _dot(p_n[g], v_new[g]) for g in groups]

    scored = [scores(e) for e in range(nb)]
    weighted = [weights(s_c, s_n) for _, s_c, s_n in scored]
    accs = [values(e, scored[e][0], weighted[e][2], weighted[e][3]) for e in range(nb)]
    for e in range(nb):
        ms, ls = weighted[e][0], weighted[e][1]
        outs = [accs[e][g] / ls[g] for g in groups]
        lses = [ms[g] + jnp.log(ls[g]) for g in groups]
        m = functools.reduce(jnp.maximum, lses)
        es = [jnp.exp(l - m) for l in lses]
        inv = 1.0 / (es[0] + es[1] + es[2])
        for g in groups:
            mixed = outs[g] * (es[g] * inv)
            for h in range(B_HG):
                cols = slice(g * B_GW + h * B_HD, g * B_GW + (h + 1) * B_HD)
                out_ref[e, :, cols] = mixed[h * seq:(h + 1) * seq] * zs_ref[e, :, cols]


def _sample_attn(q, kv, zs, caches, batch, seq, nb):
    assert batch % nb == 0
    masks, mask_new = _sample_masks(seq)
    specs, views = [], []
    for (window, dil), c in zip(B_GROUPS, caches):
        assert window == dil * (N_KEYS - 1) and c.shape[2] == window
        if dil >= seq:
            views.append(c.reshape(batch, window // dil, dil * KV_ROWS, B_HD))
            specs.append(pl.BlockSpec((nb, window // dil, seq * KV_ROWS, B_HD), lambda b: (b, 0, 0, 0)))
        else:
            assert seq % dil == 0
            views.append(c.reshape(batch, window * KV_ROWS, B_HD))
            specs.append(pl.BlockSpec((nb, window * KV_ROWS, B_HD), lambda b: (b, 0, 0)))
    const = lambda a: pl.BlockSpec(a.shape, lambda b: (0,) * a.ndim)
    tok = lambda w: pl.BlockSpec((nb, seq, w), lambda b: (b, 0, 0))
    return pl.pallas_call(
        functools.partial(_sample_attn_kernel, seq=seq, nb=nb),
        grid=(batch // nb,),
        in_specs=[tok(B_WIDTH), tok(2 * B_WIDTH), tok(B_WIDTH)] + specs
                 + [const(a) for a in masks] + [const(mask_new)],
        out_specs=tok(B_WIDTH),
        out_shape=jax.ShapeDtypeStruct((batch, seq, B_WIDTH), F32),
        compiler_params=_params(("parallel",)),
        name="sample_attn",
    )(q, kv, zs, *views, *[jnp.asarray(a) for a in masks], jnp.asarray(mask_new))


class _Tiles(NamedTuple):
    l0_in: int
    l1_in: int
    out: int
    scan: int
    chunk: int


LONG_SEQ = 512


def _tiles(batch, seq):
    if seq >= LONG_SEQ:
        return _Tiles(l0_in=512, l1_in=256, out=1024, scan=512, chunk=64)
    n = batch * seq
    return _Tiles(l0_in=min(256, n), l1_in=min(256, n), out=min(1024, n), scan=seq, chunk=seq)


def _trunk(x, ada, start, state, caches, weights):
    (norm_w, alb, a_w_in, a_b_f, a_gw, a_w_out, b_w_in, b_qn, b_kn, b_w_out) = weights
    batch, seq, _ = x.shape
    tiles = _tiles(batch, seq)
    flat = seq >= LONG_SEQ
    act = BF16 if flat else F32
    tok = (lambda a: a.reshape(batch * seq, a.shape[-1])) if flat else (lambda a: a)
    rows = lambda a: a.reshape(batch * seq, a.shape[-1])
    ada0 = ada[0].reshape(batch, 1, 3 * D_MODEL)
    ada1 = ada[1].reshape(batch, 1, 3 * D_MODEL)

    x0 = tok(x)
    q, k, lf, v, zs = _l0_inproj(x0, ada0, norm_w[0:1], a_w_in, alb, a_b_f, batch, seq, tiles.l0_in, act)
    og, s_new = _hgrn_scan(rows(q), rows(k), rows(lf), rows(v), rows(zs), a_gw, state,
                           batch, seq, tiles.scan, tiles.chunk, act, n_seq=0 if state is None else SCAN_SEQS)
    og = og if flat else og.reshape(batch, seq, A_V)
    x1 = _outproj([og], a_w_out, x0, ada0, batch, seq, tiles.out)

    cos, sin = _rope_tables(start, seq)
    qr, kv, zs1, *new_rows = _l1_inproj(x1, ada1, norm_w[1:2], b_w_in, b_qn, b_kn, cos, sin, batch, seq,
                                        tiles.l1_in, act)
    kv3 = kv.reshape(batch, seq, 2 * B_WIDTH)
    if caches is None:
        og1 = _prompt_attn(qr.reshape(batch, seq, B_WIDTH), kv3, zs1.reshape(batch, seq, B_WIDTH), batch, seq)
        og1 = [o.reshape(batch * seq, B_GW) for o in og1]
    else:
        og1 = [_sample_attn(qr, kv, zs1, caches, batch, seq, SAMPLE_NB)]
    x2 = _outproj(og1, b_w_out, x1, ada1, batch, seq, tiles.out)

    kv_rows = [r.reshape(batch, min(window, seq), 2, B_HG, B_HD) for r, (window, _) in zip(new_rows, B_GROUPS)]
    return x2.reshape(batch, seq, D_MODEL), s_new, kv_rows


def kernel(x_prompt, x_sample, state_hgrn, cache_kv_w128, cache_kv_w512, cache_kv_w2048, c_prompt, c_sample,
           norm_w, ada_w, ada_b, a_lower_bounds, a_w_in, a_b_f, a_g_norm_w, a_w_out, b_w_in, b_q_norm_w,
           b_k_norm_w, b_w_out):
    bp, lp, _ = x_prompt.shape
    bs, ls, _ = x_sample.shape

    pad = (-(bp + bs)) % 8
    c_all = jnp.concatenate([c_prompt, c_sample, jnp.zeros((pad, D_MODEL), F32)], axis=0)
    ada = _ada_vectors(c_all, ada_w, ada_b)
    ada_p, ada_s = ada[:, :bp], ada[:, bp:bp + bs]

    weights = (norm_w, a_lower_bounds, a_w_in[0], a_b_f, a_g_norm_w, a_w_out[0],
               b_w_in[0], b_q_norm_w, b_k_norm_w, b_w_out[0])

    y_p, s_p, kv_p = _trunk(x_prompt, ada_p, 0, None, None, weights)
    caches = (cache_kv_w128, cache_kv_w512, cache_kv_w2048)
    y_s, s_s, kv_s = _trunk(x_sample, ada_s, PAST_LEN, state_hgrn[0], caches, weights)

    kv_out = []
    for g in range(len(B_GROUPS)):
        kv_out.append(kv_p[g][None])
        kv_out.append(kv_s[g][None])
    return (y_p, y_s, s_p[None], s_s[None], *kv_out)
```

```python
import functools
import math
from typing import NamedTuple

import numpy as np
import jax
import jax.numpy as jnp
from jax import lax
from jax.experimental import pallas as pl
from jax.experimental.pallas import tpu as pltpu

F32 = jnp.float32
BF16 = jnp.bfloat16

D_MODEL = 1024
EPS = 1e-6
A_HEADS = 8
A_KDIM = 128
A_VDIM = 256
A_QK = A_HEADS * A_KDIM
A_V = A_HEADS * A_VDIM
B_GROUPS = ((128, 1), (512, 4), (2048, 16))
B_HG = 4
B_HD = 128
B_GW = B_HG * B_HD
B_WIDTH = len(B_GROUPS) * B_GW
ROPE_THETA = 10000.0
PAST_LEN = 2048
N_KEYS = 129

COL_TILE = 512
VMEM_LIMIT = 56 * 1024 * 1024

_NT = (((1,), (1,)), ((), ()))
_TN = (((0,), (0,)), ((), ()))


def _sigmoid(x):
    return 1.0 / (1.0 + jnp.exp(-x))


def _silu(x):
    h = 0.5 * x
    return h * jnp.tanh(h) + h


def _dot(a, b, dims=None):
    if dims is None:
        return jnp.dot(a, b, preferred_element_type=F32)
    return lax.dot_general(a, b, dims, preferred_element_type=F32)


def _params(sem):
    return pltpu.CompilerParams(dimension_semantics=sem, vmem_limit_bytes=VMEM_LIMIT)


def _ada_kernel(c_ref, w_ref, b_ref, o_ref):
    a = _silu(c_ref[...]).astype(BF16)
    o_ref[0] = _dot(a, w_ref[0].astype(BF16)) + b_ref[0]


def _ada_vectors(c_all, ada_w, ada_b):
    n_layers, _, width = ada_w.shape
    rows = c_all.shape[0]
    return pl.pallas_call(
        _ada_kernel,
        grid=(n_layers, width // COL_TILE),
        in_specs=[
            pl.BlockSpec((rows, D_MODEL), lambda l, j: (0, 0)),
            pl.BlockSpec((1, D_MODEL, COL_TILE), lambda l, j: (l, 0, j)),
            pl.BlockSpec((1, 1, COL_TILE), lambda l, j: (l, 0, j)),
        ],
        out_specs=pl.BlockSpec((1, rows, COL_TILE), lambda l, j: (l, 0, j)),
        out_shape=jax.ShapeDtypeStruct((n_layers, rows, width), F32),
        compiler_params=_params(("parallel", "parallel")),
        name="ada_vectors",
    )(c_all, ada_w, ada_b.reshape(n_layers, 1, width))


def _modulated_norm(x_ref, ada_ref, nw_ref):
    x = x_ref[...]
    ada = ada_ref[...]
    if x.ndim == 2:
        ada = ada.reshape(1, ada.shape[-1])
    shift = ada[..., :D_MODEL]
    scale = ada[..., D_MODEL:2 * D_MODEL]
    ms = jnp.mean(x * x, axis=-1, keepdims=True)
    y = x * lax.rsqrt(ms + EPS) * nw_ref[...].reshape((1,) * (x.ndim - 1) + (D_MODEL,))
    h = y * (1.0 + scale) + shift
    return h.reshape(-1, D_MODEL).astype(BF16)


def _token_specs(batch, seq, tm):
    if seq >= tm:
        assert seq % tm == 0
        per = seq // tm
        n_tiles = batch * per
        tok_block = lambda w: pl.BlockSpec((tm, w), lambda i: (i, 0))
        ada_block = pl.BlockSpec((1, 1, 3 * D_MODEL), lambda i: (i // per, 0, 0))
        shape = lambda w: (batch * seq, w)
        return n_tiles, tok_block, ada_block, shape, per
    tb = tm // seq
    assert batch % tb == 0
    n_tiles = batch // tb
    tok_block = lambda w: pl.BlockSpec((tb, seq, w), lambda i: (i, 0, 0))
    ada_block = pl.BlockSpec((tb, 1, 3 * D_MODEL), lambda i: (i, 0, 0))
    shape = lambda w: (batch, seq, w)
    return n_tiles, tok_block, ada_block, shape, None


def _resident(shape):
    return pl.BlockSpec(shape, lambda i: (0,) * len(shape), pipeline_mode=pl.Buffered(1))


def _put(ref, c0, val):
    ref[(Ellipsis, slice(c0, c0 + val.shape[-1]))] = val.reshape(ref.shape[:-1] + val.shape[-1:]).astype(ref.dtype)


def _l0_inproj_kernel(x_ref, ada_ref, nw_ref, w_ref, alb_ref, bf_ref, gw_ref, q_ref, k_ref, lf_ref, v_ref, zs_ref):
    h = _modulated_norm(x_ref, ada_ref, nw_ref)
    a = alb_ref[...]
    m = jnp.max(a, axis=0, keepdims=True)
    e = jnp.exp(a - m)
    lb = e[0:1] / jnp.sum(e, axis=0, keepdims=True)
    for c0 in range(0, w_ref.shape[1], COL_TILE):
        acc = _dot(h, w_ref[:, c0:c0 + COL_TILE].astype(BF16))
        if c0 < A_QK:
            _put(q_ref, c0, _silu(acc))
        elif c0 < 2 * A_QK:
            c = c0 - A_QK
            f = lb[:, c:c + COL_TILE] + (1.0 - lb[:, c:c + COL_TILE]) * _sigmoid(acc + bf_ref[:, c:c + COL_TILE])
            _put(lf_ref, c, jnp.log(f))
            _put(k_ref, c, 1.0 - f)
        elif c0 < 2 * A_QK + A_V:
            _put(v_ref, c0 - 2 * A_QK, acc)
        else:
            c = c0 - 2 * A_QK - A_V
            _put(zs_ref, c, _silu(acc) * gw_ref[:, c:c + COL_TILE])


def _l0_inproj(x, ada, nw, w_bf, alb, b_f, gw, batch, seq, tm, act_dtype):
    n_tiles, tok_block, ada_block, shape, _ = _token_specs(batch, seq, tm)
    return pl.pallas_call(
        _l0_inproj_kernel,
        grid=(n_tiles,),
        in_specs=[tok_block(D_MODEL), ada_block, _resident((1, D_MODEL)), _resident(w_bf.shape),
                  _resident(alb.shape), _resident(b_f.shape), _resident(gw.shape)],
        out_specs=[tok_block(A_QK), tok_block(A_QK), tok_block(A_QK), tok_block(A_V), tok_block(A_V)],
        out_shape=[
            jax.ShapeDtypeStruct(shape(A_QK), F32),
            jax.ShapeDtypeStruct(shape(A_QK), F32),
            jax.ShapeDtypeStruct(shape(A_QK), F32),
            jax.ShapeDtypeStruct(shape(A_V), act_dtype),
            jax.ShapeDtypeStruct(shape(A_V), act_dtype),
        ],
        compiler_params=_params(("parallel",)),
        name="l0_inproj",
    )(x, ada, nw, w_bf, alb, b_f, gw)


def _scan_levels(chunk):
    return [1 << l for l in range(int(math.log2(chunk)))]


def _exponent_matrix(chunk):
    r = np.arange(chunk)[:, None]
    c = np.arange(chunk)[None, :]
    blocks = [c <= r, c > r]
    for s in _scan_levels(chunk):
        bound = (r & ~(2 * s - 1)) | s
        upper = (r & s) != 0
        blocks.append(np.where(upper, (c > bound) & (c <= r), (c > r) & (c <= bound)))
    return np.concatenate(blocks, axis=0).astype(np.float32)


def _hgrn_chunk(q, k, lf, v, state, emat, chunk):
    dk = q.shape[-1]
    ex = jnp.exp(jnp.dot(emat, lf, preferred_element_type=F32, precision=lax.Precision.HIGHEST))
    e_b = ex[0:chunk]
    e_u = ex[chunk:2 * chunk]
    row = lax.broadcasted_iota(jnp.int32, (chunk, 1), 0)
    col = lax.broadcasted_iota(jnp.int32, (1, chunk), 1)

    att = jnp.where(row == col, jnp.sum(q * k, axis=-1, keepdims=True), 0.0)
    for l, s in enumerate(_scan_levels(chunk)):
        g = ex[(2 + l) * chunk:(3 + l) * chunk]
        upper = (row & s) != 0
        xb = (jnp.where(upper, q, k) * g).astype(BF16)
        part = _dot(xb, xb, _NT)
        keep = upper & ((col & s) == 0) & ((row >> (l + 1)) == (col >> (l + 1)))
        att = att + jnp.where(keep, part, 0.0)

    o = _dot(att.astype(BF16), v.astype(BF16)) + _dot((q * e_b).astype(BF16), state.astype(BF16))

    e_last = e_b[chunk - 1:chunk]
    eye = lax.broadcasted_iota(jnp.int32, (dk, dk), 0) == lax.broadcasted_iota(jnp.int32, (dk, dk), 1)
    e_col = jnp.sum(jnp.where(eye, e_last, 0.0), axis=-1, keepdims=True)
    new_state = state * e_col + _dot((k * e_u).astype(BF16), v.astype(BF16), _TN)
    return o, new_state


SAFE_DECAY = 80.0
SPLIT_PAD = 32
SCAN_LINK = 2
SCAN_SEQS = 8


def _split_prefix_matrix(chunk, n_seq):
    t = np.kron(np.eye(n_seq, dtype=np.float32), _exponent_matrix(chunk)[:chunk])
    pad = (-3 * t.shape[1]) % SPLIT_PAD
    return np.concatenate([t, t, t, np.zeros((t.shape[0], pad), np.float32)], axis=1)


def _hgrn_chunk_bounded(q, k, lf, v, states, tmat, chunk, n_seq, linked):
    n_rows = n_seq * chunk
    hi = lf.astype(BF16)
    r1 = lf - hi.astype(F32)
    mid = r1.astype(BF16)
    lo = r1 - mid.astype(F32)
    parts = [hi.astype(F32), mid.astype(F32), lo]
    pad = tmat.shape[1] - 3 * n_rows
    if pad:
        parts.append(jnp.zeros((pad, lf.shape[1]), F32))
    b = _dot(tmat, jnp.concatenate(parts, axis=0).astype(BF16))
    e_b = jnp.exp(b)
    qb = q * e_b
    kn_f = k * jnp.exp(-b)
    kn = kn_f.astype(BF16)
    row = lax.broadcasted_iota(jnp.int32, (n_rows, n_rows), 0)
    col = lax.broadcasted_iota(jnp.int32, (n_rows, n_rows), 1)
    causal = row >= col
    if n_seq > 1:
        shift = chunk.bit_length() - 1
        causal = causal & ((row >> shift) == (col >> shift))
    eye = lax.broadcasted_iota(jnp.int32, (A_KDIM, A_KDIM), 0) == lax.broadcasted_iota(jnp.int32, (A_KDIM, A_KDIM), 1)
    ks = [slice(h * A_KDIM, (h + 1) * A_KDIM) for h in range(A_HEADS)]
    vs = [slice(h * A_VDIM, (h + 1) * A_VDIM) for h in range(A_HEADS)]
    sq = [slice(s * chunk, (s + 1) * chunk) for s in range(n_seq)]
    qbb, vb = qb.astype(BF16), v.astype(BF16)
    qb_s = [qbb] if n_seq == 1 else [qb[r].astype(BF16) for r in sq]
    ku_s = [(kn_f[r] * e_b[r.stop - 1:r.stop]).astype(BF16) for r in sq]
    v_s = [vb] if n_seq == 1 else [v[r].astype(BF16) for r in sq]
    atts = [jnp.where(causal, _dot(qbb[:, ks[h]], kn[:, ks[h]], _NT), 0.0).astype(BF16) for h in range(A_HEADS)]
    e_cols = [[jnp.sum(jnp.where(eye, e_b[(s + 1) * chunk - 1:(s + 1) * chunk, ks[h]], 0.0), axis=-1, keepdims=True)
               for h in range(A_HEADS)] for s in range(n_seq)]
    new_states = [[None] * A_HEADS for _ in range(n_seq)]
    carried = [[None] * A_HEADS for _ in range(n_seq)]
    for s in range(n_seq):
        for h in range(A_HEADS):
            state = new_states[s - 1][h] if (linked and s > 0) else states[0 if linked else s][h]
            carried[s][h] = _dot(qb_s[s][:, ks[h]], state.astype(BF16))
            new_states[s][h] = state * e_cols[s][h] + _dot(ku_s[s][:, ks[h]], v_s[s][:, vs[h]], _TN)
    outs = []
    for h in range(A_HEADS):
        inter = carried[0][h] if n_seq == 1 else jnp.concatenate([carried[s][h] for s in range(n_seq)], axis=0)
        outs.append(_dot(atts[h], vb[:, vs[h]]) + inter)
    return outs, new_states


def _hgrn_scan_kernel(*refs, chunk, n_chunks, has_state, n_seq, n_link):
    one_shot = n_seq > 0
    if has_state:
        q_ref, k_ref, lf_ref, v_ref, zs_ref, e_ref, t_ref, s0_ref, og_ref, so_ref, s_scr = refs
    else:
        q_ref, k_ref, lf_ref, v_ref, zs_ref, e_ref, t_ref, og_ref, so_ref, s_scr = refs
    t = pl.program_id(1)

    if not one_shot:
        @pl.when(t == 0)
        def _():
            if has_state:
                s_scr[...] = s0_ref[0]
            else:
                s_scr[...] = jnp.zeros_like(s_scr)

    def load_state(s, h):
        return s0_ref[s, h] if one_shot else s_scr[h]

    def store_state(s, h, val):
        if one_shot:
            so_ref[s, h] = val
        else:
            s_scr[h] = val

    def finish(rows, h, o):
        vc = slice(h * A_VDIM, (h + 1) * A_VDIM)
        ms = jnp.mean(o * o, axis=-1, keepdims=True)
        og_ref[rows, vc] = (o * lax.rsqrt(ms + EPS) * zs_ref[rows, vc]).astype(og_ref.dtype)

    seqs = range(max(n_seq, 1))

    def chunk_rows(c, s=None, n=1):
        if n_chunks > 1:
            return pl.ds(pl.multiple_of(c * (n * chunk), n * chunk), n * chunk)
        return slice(0, len(seqs) * chunk) if s is None else slice(s * chunk, (s + 1) * chunk)

    def bounded_body(c, carry):
        rows = chunk_rows(c, n=n_link)
        states = [[load_state(s, h) for h in range(A_HEADS)] for s in seqs]
        n_at_once = len(seqs) if one_shot else n_link
        outs, new_states = _hgrn_chunk_bounded(q_ref[rows, :], k_ref[rows, :], lf_ref[rows, :], v_ref[rows, :],
                                               states, t_ref[...], chunk, n_at_once, linked=not one_shot)
        for h in range(A_HEADS):
            if one_shot:
                for s in seqs:
                    store_state(s, h, new_states[s][h])
            else:
                store_state(0, h, new_states[-1][h])
            finish(rows, h, outs[h])
        return carry

    def general_body(c, carry):
        for s in seqs:
            rows = chunk_rows(c, s)
            for h in range(A_HEADS):
                kc = slice(h * A_KDIM, (h + 1) * A_KDIM)
                vc = slice(h * A_VDIM, (h + 1) * A_VDIM)
                o, s_new = _hgrn_chunk(q_ref[rows, kc], k_ref[rows, kc], lf_ref[rows, kc],
                                       v_ref[rows, vc], load_state(s, h), e_ref[...], chunk)
                store_state(s, h, s_new)
                finish(rows, h, o)
        return carry

    def run(body, trips):
        def go():
            if trips == 1:
                body(0, 0)
            else:
                lax.fori_loop(0, trips, body, 0)
        return go

    bounded = jnp.min(lf_ref[...]) * chunk >= -SAFE_DECAY
    lax.cond(bounded, run(bounded_body, n_chunks // n_link), run(general_body, n_chunks))

    if not one_shot:
        @pl.when(t == pl.num_programs(1) - 1)
        def _():
            so_ref[0] = s_scr[...]


def _hgrn_scan(q, k, lf, v, zs, s0, batch, seq, tile, chunk, out_dtype, n_seq=0):
    n_t = seq // tile
    per_step = max(n_seq, 1)
    assert batch % per_step == 0 and (n_seq == 0 or (s0 is not None and seq == tile == chunk))
    n_link = SCAN_LINK if (n_seq == 0 and (tile // chunk) % SCAN_LINK == 0) else 1
    emat = jnp.asarray(_exponent_matrix(chunk))
    tmat = jnp.asarray(_split_prefix_matrix(chunk, max(per_step, n_link)), dtype=BF16)
    row_block = lambda w: pl.BlockSpec((tile * per_step, w), lambda b, t: (b * n_t + t, 0))
    state_block = pl.BlockSpec((per_step, A_HEADS, A_KDIM, A_VDIM), lambda b, t: (b, 0, 0, 0))
    in_specs = [row_block(A_QK), row_block(A_QK), row_block(A_QK), row_block(A_V), row_block(A_V),
                pl.BlockSpec(emat.shape, lambda b, t: (0, 0)),
                pl.BlockSpec(tmat.shape, lambda b, t: (0, 0))]
    args = [q, k, lf, v, zs, emat, tmat]
    if s0 is not None:
        in_specs.append(state_block)
        args.append(s0)
    kern = functools.partial(_hgrn_scan_kernel, chunk=chunk, n_chunks=tile // chunk,
                             has_state=s0 is not None, n_seq=n_seq, n_link=n_link)
    return pl.pallas_call(
        kern,
        grid=(batch // per_step, n_t),
        in_specs=in_specs,
        out_specs=[row_block(A_V), state_block],
        out_shape=[jax.ShapeDtypeStruct((batch * seq, A_V), out_dtype),
                   jax.ShapeDtypeStruct((batch, A_HEADS, A_KDIM, A_VDIM), F32)],
        scratch_shapes=[pltpu.VMEM((A_HEADS, A_KDIM, A_VDIM), F32)],
        compiler_params=_params(("parallel", "arbitrary")),
        name="hgrn_scan",
    )(*args)


def _outproj_kernel(*refs):
    *g_refs, w_ref, x_ref, ada_ref, o_ref = refs
    gs = [g_ref[...] for g_ref in g_refs]
    gs = [g.reshape(-1, g.shape[-1]).astype(BF16) for g in gs]
    ada = ada_ref[...]
    if len(x_ref.shape) == 2:
        ada = ada.reshape(1, ada.shape[-1])
    for c0 in range(0, D_MODEL, COL_TILE):
        cols = (Ellipsis, slice(c0, c0 + COL_TILE))
        y, k0 = None, 0
        for g in gs:
            part = _dot(g, w_ref[k0:k0 + g.shape[1], c0:c0 + COL_TILE].astype(BF16))
            y = part if y is None else y + part
            k0 += g.shape[1]
        gate = ada[..., 2 * D_MODEL + c0:2 * D_MODEL + c0 + COL_TILE]
        o_ref[cols] = x_ref[cols] + gate * y.reshape(x_ref.shape[:-1] + (COL_TILE,))


def _outproj(gs, w_bf, x, ada, batch, seq, tm):
    n_tiles, tok_block, ada_block, shape, _ = _token_specs(batch, seq, tm)
    assert sum(g.shape[-1] for g in gs) == w_bf.shape[0]
    return pl.pallas_call(
        _outproj_kernel,
        grid=(n_tiles,),
        in_specs=[tok_block(g.shape[-1]) for g in gs] + [_resident(w_bf.shape), tok_block(D_MODEL), ada_block],
        out_specs=tok_block(D_MODEL),
        out_shape=jax.ShapeDtypeStruct(shape(D_MODEL), F32),
        compiler_params=_params(("parallel",)),
        name="outproj",
    )(*gs, w_bf, x, ada)


def _rope_table_kernel(pos_ref, inv_ref, cos_ref, sin_ref):
    ang = pos_ref[...] * inv_ref[...]
    lane = lax.broadcasted_iota(jnp.int32, ang.shape, 1)
    cos_ref[...] = jnp.cos(ang)
    sin_ref[...] = jnp.where(lane < B_HD // 2, -jnp.sin(ang), jnp.sin(ang))


def _rope_tables(start, seq):
    half = B_HD // 2
    inv = ROPE_THETA ** (-jnp.arange(half, dtype=F32) / half)
    inv2 = jnp.concatenate([inv, inv]).reshape(1, B_HD)
    pos = (start + jnp.arange(seq, dtype=jnp.int32)).astype(F32).reshape(seq, 1)
    return pl.pallas_call(
        _rope_table_kernel,
        out_shape=[jax.ShapeDtypeStruct((seq, B_HD), F32)] * 2,
        name="rope_tables",
    )(pos, inv2)


KV_ROWS = 2 * B_HG


def _l1_inproj_kernel(x_ref, ada_ref, nw_ref, w_ref, qn_ref, kn_ref, cos_ref, sin_ref, q_ref, kv_ref, zs_ref,
                      *row_refs, row_tokens, row_first, per):
    h = _modulated_norm(x_ref, ada_ref, nw_ref)
    tm = h.shape[0]
    cos = cos_ref[...]
    sin = sin_ref[...]
    reps = tm // cos.shape[0]
    if reps > 1:
        cos = jnp.broadcast_to(cos[None], (reps,) + cos.shape).reshape(tm, B_HD)
        sin = jnp.broadcast_to(sin[None], (reps,) + sin.shape).reshape(tm, B_HD)

    def norm_rope(acc, nw):
        outs = []
        for hd in range(acc.shape[1] // B_HD):
            xh = acc[:, hd * B_HD:(hd + 1) * B_HD]
            ms = jnp.mean(xh * xh, axis=-1, keepdims=True)
            y = xh * lax.rsqrt(ms + EPS) * nw
            outs.append(y * cos + pltpu.roll(y, B_HD // 2, 1) * sin)
        return jnp.concatenate(outs, axis=-1)

    for c0 in range(0, w_ref.shape[1], B_GW):
        acc = _dot(h, w_ref[:, c0:c0 + B_GW].astype(BF16))
        seg, g = divmod(c0 // B_GW, len(B_GROUPS))
        if seg == 0:
            _put(q_ref, g * B_GW, norm_rope(acc, qn_ref[...]))
        elif seg == 1:
            _put(kv_ref, 2 * g * B_GW, norm_rope(acc, kn_ref[...]))
        elif seg == 2:
            _put(kv_ref, (2 * g + 1) * B_GW, acc)
        else:
            _put(zs_ref, g * B_GW, _silu(acc))

    for g, rows_ref in enumerate(row_refs):
        @pl.when(pl.program_id(0) % per >= row_first[g])
        def _(g=g, rows_ref=rows_ref):
            n = row_tokens[g]
            for j in range(KV_ROWS):
                cols = (Ellipsis, slice(2 * g * B_GW + j * B_HD, 2 * g * B_GW + (j + 1) * B_HD))
                rows_ref[pl.ds(j, n, stride=KV_ROWS), :] = kv_ref[cols].reshape(tm, B_HD)[tm - n:]


def _l1_inproj(x, ada, nw, w_bf, qn, kn, cos, sin, batch, seq, tm, act_dtype):
    n_tiles, tok_block, ada_block, shape, per = _token_specs(batch, seq, tm)
    out_specs = [tok_block(B_WIDTH), tok_block(2 * B_WIDTH), tok_block(B_WIDTH)]
    out_shape = [jax.ShapeDtypeStruct(shape(B_WIDTH), F32),
                 jax.ShapeDtypeStruct(shape(2 * B_WIDTH), F32),
                 jax.ShapeDtypeStruct(shape(B_WIDTH), act_dtype)]
    row_tokens, row_first = (), ()
    if per is None:
        tab_block = _resident((seq, B_HD))
        per = 1
        for _ in B_GROUPS:
            out_specs.append(pl.BlockSpec((tm * KV_ROWS, B_HD), lambda i: (i, 0)))
            out_shape.append(jax.ShapeDtypeStruct((batch * seq * KV_ROWS, B_HD), F32))
            row_tokens += (tm,)
            row_first += (0,)
    else:
        tab_block = pl.BlockSpec((tm, B_HD), lambda i: (i % per, 0))
        for window, _ in B_GROUPS:
            keep = min(window, seq)
            blk = min(keep, tm)
            n_blk = keep // blk
            first = per - n_blk
            out_specs.append(pl.BlockSpec(
                (blk * KV_ROWS, B_HD),
                lambda i, n_blk=n_blk, first=first: ((i // per) * n_blk + jnp.clip(i % per - first, 0, n_blk - 1), 0)))
            out_shape.append(jax.ShapeDtypeStruct((batch * keep * KV_ROWS, B_HD), F32))
            row_tokens += (blk,)
            row_first += (first,)
    return pl.pallas_call(
        functools.partial(_l1_inproj_kernel, row_tokens=row_tokens, row_first=row_first, per=per),
        grid=(n_tiles,),
        in_specs=[tok_block(D_MODEL), ada_block, _resident((1, D_MODEL)), _resident(w_bf.shape),
                  _resident((1, B_HD)), _resident((1, B_HD)), tab_block, tab_block],
        out_specs=out_specs,
        out_shape=out_shape,
        compiler_params=_params(("arbitrary",)),
        name="l1_inproj",
    )(x, ada, nw, w_bf, qn, kn, cos, sin)


Q_SUB = 128
ATTN_TILE = 2048
ATTN_BATCH = 8
MIX_ROWS = 256


def _attend(qs, k2s, v2s, valids):
    scale = B_HD ** -0.5
    ss = [jnp.where(ok, _dot(q, k2, _NT) * scale, -jnp.inf) for q, k2, ok in zip(qs, k2s, valids)]
    ms = [jnp.max(s, axis=-1, keepdims=True) for s in ss]
    ps = [jnp.exp(s - m) for s, m in zip(ss, ms)]
    ls = [jnp.sum(p, axis=-1, keepdims=True) for p in ps]
    os = [_dot(p.astype(BF16), v2) * (1.0 / l) for p, v2, l in zip(ps, v2s, ls)]
    lses = [m + jnp.log(l) for m, l in zip(ms, ls)]
    return os, lses


def _prompt_attn_kernel(*refs):
    n_g = len(B_GROUPS)
    ins, zs_refs = refs[:5 * n_g], refs[5 * n_g:6 * n_g]
    out_refs, (o_scr, lse_scr) = refs[6 * n_g:7 * n_g], refs[7 * n_g:]
    first_key = jnp.where(pl.program_id(2) == 0, Q_SUB, 0)
    row = lax.broadcasted_iota(jnp.int32, (Q_SUB, 2 * Q_SUB), 0)
    col = lax.broadcasted_iota(jnp.int32, (Q_SUB, 2 * Q_SUB), 1)
    band = (col >= row) & (col <= row + Q_SUB)
    band_first = band & (col >= first_key)

    for g, (_, dil) in enumerate(B_GROUPS):
        q_ref, kc_ref, vc_ref, kp_ref, vp_ref = ins[5 * g:5 * g + 5]
        n_sb = ATTN_TILE // (Q_SUB * dil)

        def rows_of(r, first, n):
            start = r + first * dil
            return pl.ds(start, n) if dil == 1 else pl.ds(start, n, stride=dil)

        work = []
        for r in range(dil):
            cls = rows_of(r, 0, n_sb * Q_SUB)
            kr = jnp.concatenate([kp_ref[0, rows_of(r, 0, Q_SUB), :], kc_ref[0, cls, :]], axis=0).astype(BF16)
            vr = jnp.concatenate([vp_ref[0, rows_of(r, 0, Q_SUB), :], vc_ref[0, cls, :]], axis=0).astype(BF16)
            qr = q_ref[0, cls, :].astype(BF16)
            for sb in range(n_sb):
                work.append((rows_of(r, sb * Q_SUB, Q_SUB), qr[sb * Q_SUB:(sb + 1) * Q_SUB],
                             kr[sb * Q_SUB:(sb + 2) * Q_SUB], vr[sb * Q_SUB:(sb + 2) * Q_SUB],
                             band_first if sb == 0 else band))
        for i in range(0, len(work), ATTN_BATCH):
            part = work[i:i + ATTN_BATCH]
            os, lses = _attend([w[1] for w in part], [w[2] for w in part], [w[3] for w in part],
                               [w[4] for w in part])
            for w, o, lse in zip(part, os, lses):
                o_scr[g, w[0], :] = o
                lse_scr[g, w[0], :] = jnp.broadcast_to(lse, (Q_SUB, B_HD))

    for c in range(ATTN_TILE // MIX_ROWS):
        rows = slice(c * MIX_ROWS, (c + 1) * MIX_ROWS)
        lses = [lse_scr[g, rows, :] for g in range(n_g)]
        m = functools.reduce(jnp.maximum, lses)
        es = [jnp.exp(l - m) for l in lses]
        inv = 1.0 / functools.reduce(lambda a, b: a + b, es)
        for g in range(n_g):
            out_refs[g][0, rows, :] = (o_scr[g, rows, :] * (es[g] * inv) * zs_refs[g][0, rows, :]
                                       ).astype(out_refs[g].dtype)


def _prompt_attn(q, kv, zs, batch, seq):
    assert seq % ATTN_TILE == 0
    n_tiles = seq // ATTN_TILE
    tok = lambda c0: pl.BlockSpec((1, ATTN_TILE, B_HD), lambda b, h, i: (b, i, c0 + h))
    in_specs, args = [], []
    for g, (_, dil) in enumerate(B_GROUPS):
        back = Q_SUB * dil
        assert ATTN_TILE % back == 0
        k0, v0 = 2 * g * B_HG, (2 * g + 1) * B_HG
        prev = lambda c0, back=back: pl.BlockSpec(
            (1, back, B_HD), lambda b, h, i: (b, jnp.maximum(i * (ATTN_TILE // back) - 1, 0), c0 + h))
        in_specs += [tok(g * B_HG), tok(k0), tok(v0), prev(k0), prev(v0)]
        args += [q, kv, kv, kv, kv]
    in_specs += [tok(g * B_HG) for g in range(len(B_GROUPS))]
    args += [zs] * len(B_GROUPS)
    out = pl.BlockSpec((1, ATTN_TILE, B_HD), lambda b, h, i: (b, i, h))
    return pl.pallas_call(
        _prompt_attn_kernel,
        grid=(batch, B_HG, n_tiles),
        in_specs=in_specs,
        out_specs=[out] * len(B_GROUPS),
        out_shape=[jax.ShapeDtypeStruct((batch, seq, B_GW), BF16)] * len(B_GROUPS),
        scratch_shapes=[pltpu.VMEM((len(B_GROUPS), ATTN_TILE, B_HD), F32)] * 2,
        compiler_params=_params(("parallel", "parallel", "arbitrary")),
        name="prompt_attn",
    )(*args)


SAMPLE_NB = 2


def _sample_masks(seq):
    hq = np.arange(B_HG * seq)[:, None] // seq
    iq = np.arange(B_HG * seq)[:, None] % seq
    masks = []
    for window, dil in B_GROUPS:
        n_tok = window if dil < seq else (window // dil) * seq
        col = np.arange(n_tok * KV_ROWS)[None, :]
        tok, is_v, head = col // KV_ROWS, (col // B_HG) % 2, col % B_HG
        if dil < seq:
            ok = (tok >= iq) & ((tok - iq) % dil == 0)
        else:
            ok = (tok % seq) == iq
        masks.append(np.where(ok & (is_v == 0) & (head == hq), 0.0, -np.inf).astype(np.float32))
    col = np.arange(B_HG * seq)[None, :]
    new = []
    for _, dil in B_GROUPS:
        ok = (col // seq == hq) & (col % seq <= iq) & ((iq - col % seq) % dil == 0)
        new.append(np.where(ok, 0.0, -np.inf).astype(np.float32))
    return masks, np.stack(new)


def _sample_attn_kernel(q_ref, kv_ref, zs_ref, c0_ref, c1_ref, c2_ref, m0_ref, m1_ref, m2_ref, mn_ref,
                        out_ref, *, seq, nb):
    scale = B_HD ** -0.5
    caches = (c0_ref, c1_ref, c2_ref)
    masks = (m0_ref, m1_ref, m2_ref)
    groups = range(len(B_GROUPS))

    def heads(ref, e, c0):
        return jnp.concatenate([ref[e, :, c0 + h * B_HD:c0 + (h + 1) * B_HD] for h in range(B_HG)],
                               axis=0).astype(BF16)

    def scores(e):
        qa = [heads(q_ref, e, g * B_GW) for g in groups]
        k_new = [heads(kv_ref, e, 2 * g * B_GW) for g in groups]
        rows = [caches[g][e].reshape(-1, B_HD).astype(BF16) for g in groups]
        s_c = [_dot(qa[g], rows[g], _NT) * scale + masks[g][...] for g in groups]
        s_n = [_dot(qa[g], k_new[g], _NT) * scale + mn_ref[g] for g in groups]
        return rows, s_c, s_n

    def weights(s_c, s_n):
        ms = [jnp.maximum(jnp.max(s_c[g], axis=-1, keepdims=True), jnp.max(s_n[g], axis=-1, keepdims=True))
              for g in groups]
        p_c = [jnp.exp(s_c[g] - ms[g]) for g in groups]
        p_n = [jnp.exp(s_n[g] - ms[g]) for g in groups]
        ls = [jnp.sum(p_c[g], axis=-1, keepdims=True) + jnp.sum(p_n[g], axis=-1, keepdims=True) for g in groups]
        p_v = [pltpu.roll(p_c[g], B_HG, 1).astype(BF16) for g in groups]
        return ms, ls, p_v, [p.astype(BF16) for p in p_n]

    def values(e, rows, p_v, p_n):
        v_new = [heads(kv_ref, e, (2 * g + 1) * B_GW) for g in groups]
        return [_dot(p_v[g], rows[g]) + _dot(p_n[g], v_new[g]) for g in groups]

    scored = [scores(e) for e in range(nb)]
    weighted = [weights(s_c, s_n) for _, s_c, s_n in scored]
    accs = [values(e, scored[e][0], weighted[e][2], weighted[e][3]) for e in range(nb)]
    for e in range(nb):
        ms, ls = weighted[e][0], weighted[e][1]
        outs = [accs[e][g] / ls[g] for g in groups]
        lses = [ms[g] + jnp.log(ls[g]) for g in groups]
        m = functools.reduce(jnp.maximum, lses)
        es = [jnp.exp(l - m) for l in lses]
        inv = 1.0 / (es[0] + es[1] + es[2])
        for g in groups:
            mixed = outs[g] * (es[g] * inv)
            for h in range(B_HG):
                cols = slice(g * B_GW + h * B_HD, g * B_GW + (h + 1) * B_HD)
                out_ref[e, :, cols] = mixed[h * seq:(h + 1) * seq] * zs_ref[e, :, cols]


def _sample_attn(q, kv, zs, caches, batch, seq, nb):
    assert batch % nb == 0
    masks, mask_new = _sample_masks(seq)
    specs, views = [], []
    for (window, dil), c in zip(B_GROUPS, caches):
        assert window == dil * (N_KEYS - 1) and c.shape[2] == window
        if dil >= seq:
            views.append(c.reshape(batch, window // dil, dil * KV_ROWS, B_HD))
            specs.append(pl.BlockSpec((nb, window // dil, seq * KV_ROWS, B_HD), lambda b: (b, 0, 0, 0)))
        else:
            assert seq % dil == 0
            views.append(c.reshape(batch, window * KV_ROWS, B_HD))
            specs.append(pl.BlockSpec((nb, window * KV_ROWS, B_HD), lambda b: (b, 0, 0)))
    const = lambda a: pl.BlockSpec(a.shape, lambda b: (0,) * a.ndim)
    tok = lambda w: pl.BlockSpec((nb, seq, w), lambda b: (b, 0, 0))
    return pl.pallas_call(
        functools.partial(_sample_attn_kernel, seq=seq, nb=nb),
        grid=(batch // nb,),
        in_specs=[tok(B_WIDTH), tok(2 * B_WIDTH), tok(B_WIDTH)] + specs
                 + [const(a) for a in masks] + [const(mask_new)],
        out_specs=tok(B_WIDTH),
        out_shape=jax.ShapeDtypeStruct((batch, seq, B_WIDTH), F32),
        compiler_params=_params(("parallel",)),
        name="sample_attn",
    )(q, kv, zs, *views, *[jnp.asarray(a) for a in masks], jnp.asarray(mask_new))


class _Tiles(NamedTuple):
    l0_in: int
    l1_in: int
    out: int
    scan: int
    chunk: int


LONG_SEQ = 512


def _tiles(seq):
    if seq >= LONG_SEQ:
        return _Tiles(l0_in=512, l1_in=256, out=1024, scan=512, chunk=64)
    return _Tiles(l0_in=256, l1_in=256, out=256, scan=seq, chunk=seq)


def _trunk(x, ada, start, state, caches, weights):
    (norm_w, alb, a_w_in, a_b_f, a_gw, a_w_out, b_w_in, b_qn, b_kn, b_w_out) = weights
    batch, seq, _ = x.shape
    tiles = _tiles(seq)
    flat = seq >= LONG_SEQ
    act = BF16 if flat else F32
    tok = (lambda a: a.reshape(batch * seq, a.shape[-1])) if flat else (lambda a: a)
    rows = lambda a: a.reshape(batch * seq, a.shape[-1])
    ada0 = ada[0].reshape(batch, 1, 3 * D_MODEL)
    ada1 = ada[1].reshape(batch, 1, 3 * D_MODEL)

    x0 = tok(x)
    q, k, lf, v, zs = _l0_inproj(x0, ada0, norm_w[0:1], a_w_in, alb, a_b_f, jnp.tile(a_gw, (1, A_HEADS)),
                                 batch, seq, tiles.l0_in, act)
    og, s_new = _hgrn_scan(rows(q), rows(k), rows(lf), rows(v), rows(zs), state,
                           batch, seq, tiles.scan, tiles.chunk, act, n_seq=0 if state is None else SCAN_SEQS)
    og = og if flat else og.reshape(batch, seq, A_V)
    x1 = _outproj([og], a_w_out, x0, ada0, batch, seq, tiles.out)

    cos, sin = _rope_tables(start, seq)
    qr, kv, zs1, *new_rows = _l1_inproj(x1, ada1, norm_w[1:2], b_w_in, b_qn, b_kn, cos, sin, batch, seq,
                                        tiles.l1_in, act)
    kv3 = kv.reshape(batch, seq, 2 * B_WIDTH)
    if caches is None:
        og1 = _prompt_attn(qr.reshape(batch, seq, B_WIDTH), kv3, zs1.reshape(batch, seq, B_WIDTH), batch, seq)
        og1 = [o.reshape(batch * seq, B_GW) for o in og1]
    else:
        og1 = [_sample_attn(qr, kv, zs1, caches, batch, seq, SAMPLE_NB)]
    x2 = _outproj(og1, b_w_out, x1, ada1, batch, seq, tiles.out)

    kv_rows = [r.reshape(batch, min(window, seq), 2, B_HG, B_HD) for r, (window, _) in zip(new_rows, B_GROUPS)]
    return x2.reshape(batch, seq, D_MODEL), s_new, kv_rows


def kernel(x_prompt, x_sample, state_hgrn, cache_kv_w128, cache_kv_w512, cache_kv_w2048, c_prompt, c_sample,
           norm_w, ada_w, ada_b, a_lower_bounds, a_w_in, a_b_f, a_g_norm_w, a_w_out, b_w_in, b_q_norm_w,
           b_k_norm_w, b_w_out):
    bp, lp, _ = x_prompt.shape
    bs, ls, _ = x_sample.shape

    pad = (-(bp + bs)) % 8
    c_all = jnp.concatenate([c_prompt, c_sample, jnp.zeros((pad, D_MODEL), F32)], axis=0)
    ada = _ada_vectors(c_all, ada_w, ada_b)
    ada_p, ada_s = ada[:, :bp], ada[:, bp:bp + bs]

    weights = (norm_w, a_lower_bounds, a_w_in[0], a_b_f, a_g_norm_w, a_w_out[0],
               b_w_in[0], b_q_norm_w, b_k_norm_w, b_w_out[0])

    y_p, s_p, kv_p = _trunk(x_prompt, ada_p, 0, None, None, weights)
    caches = (cache_kv_w128, cache_kv_w512, cache_kv_w2048)
    y_s, s_s, kv_s = _trunk(x_sample, ada_s, PAST_LEN, state_hgrn[0], caches, weights)

    kv_out = []
    for g in range(len(B_GROUPS)):
        kv_out.append(kv_p[g][None])
        kv_out.append(kv_s[g][None])
    return (y_p, y_s, s_p[None], s_s[None], *kv_out)
```
